```python
import jax, jax.numpy as jnp
from jax import lax
import numpy as np

D_MODEL = 1024
BATCH = 4
SEQ = 4096
DEPTH = 1

CONV_DIM = D_MODEL
CONV_GROUPS = 8
CONV_K = 3
MLSTM_HEADS = 4
MLSTM_DQK = 128
MLSTM_DV = 256
MLSTM_DIM = MLSTM_HEADS * MLSTM_DV
CHUNK = 128
D_FF = 2816
N_MOD = 6
EPS = 1e-6
IN_SIZES = (CONV_DIM, CONV_DIM, CONV_DIM,
            MLSTM_HEADS * MLSTM_DQK, MLSTM_HEADS * MLSTM_DQK,
            MLSTM_DIM, MLSTM_DIM, MLSTM_HEADS, MLSTM_HEADS,
            D_MODEL, D_MODEL)
N_IN = 3 * CONV_DIM + 2 * MLSTM_HEADS * MLSTM_DQK + 2 * MLSTM_DIM + 2 * MLSTM_HEADS + 2 * D_MODEL
F_GATE_OFFSET = 3 * CONV_DIM + 2 * MLSTM_HEADS * MLSTM_DQK + 2 * MLSTM_DIM + MLSTM_HEADS

kernel_name = "hybrid_conv_mlstm_adaln_block"


def _split_points():
    pts, acc = [], 0
    for s in IN_SIZES[:-1]:
        acc += s
        pts.append(acc)
    return tuple(pts)


def rmsnorm(x, g):
    xf = x.astype(jnp.float32)
    y = xf * lax.rsqrt(jnp.mean(xf * xf, axis=-1, keepdims=True) + EPS)
    return (y * g.astype(jnp.float32)).astype(x.dtype)


def causal_dwconv(u, w):
    K = w.shape[0]
    S = u.shape[1]
    up = jnp.pad(u, ((0, 0), (K - 1, 0), (0, 0)))
    y = up[:, 0:S] * w[0]
    for j in range(1, K):
        y = y + up[:, j:j + S] * w[j]
    return y


def mlstm_chunkwise(q, k, v, i_raw, f_raw):
    B, H, S, DK = q.shape
    DV = v.shape[-1]
    L = CHUNK
    NC = S // L
    f32 = jnp.float32
    q = q.astype(f32).reshape(B, H, NC, L, DK) * (DK ** -0.5)
    k = k.astype(f32).reshape(B, H, NC, L, DK)
    v = v.astype(f32).reshape(B, H, NC, L, DV)
    log_i = i_raw.astype(f32).reshape(B, H, NC, L)
    log_f = jax.nn.log_sigmoid(f_raw.astype(f32)).reshape(B, H, NC, L)
    b = jnp.cumsum(log_f, axis=-1)
    g = b[..., -1]
    a = g[..., None] - b + log_i
    m_loc = jnp.max(a, axis=-1)
    w_loc = jnp.exp(a - m_loc[..., None])
    C_loc = jnp.einsum('bhcsk,bhcsv->bhckv', w_loc[..., None] * k, v)
    n_loc = jnp.einsum('bhcs,bhcsk->bhck', w_loc, k)

    def step(carry, xs):
        C, n, m = carry
        g_c, m_c, C_c, n_c = xs
        m_new = jnp.maximum(g_c + m, m_c)
        s_old = jnp.exp(g_c + m - m_new)
        s_new = jnp.exp(m_c - m_new)
        C_next = s_old[..., None, None] * C + s_new[..., None, None] * C_c
        n_next = s_old[..., None] * n + s_new[..., None] * n_c
        return (C_next, n_next, m_new), (C, n, m)

    init = (jnp.zeros((B, H, DK, DV), f32), jnp.zeros((B, H, DK), f32), jnp.zeros((B, H), f32))
    xs = (jnp.moveaxis(g, 2, 0), jnp.moveaxis(m_loc, 2, 0),
          jnp.moveaxis(C_loc, 2, 0), jnp.moveaxis(n_loc, 2, 0))
    _, (C_prev, n_prev, m_prev) = lax.scan(step, init, xs)
    C_prev = jnp.moveaxis(C_prev, 0, 2)
    n_prev = jnp.moveaxis(n_prev, 0, 2)
    m_prev = jnp.moveaxis(m_prev, 0, 2)

    tri = jnp.tril(jnp.ones((L, L), dtype=bool))
    Dlog = b[..., :, None] - b[..., None, :] + log_i[..., None, :]
    Dlog = jnp.where(tri, Dlog, -jnp.inf)
    inter = b + m_prev[..., None]
    m_t = jnp.maximum(inter, jnp.max(Dlog, axis=-1))
    wts = jnp.exp(Dlog - m_t[..., None]) * jnp.einsum('bhctk,bhcsk->bhcts', q, k)
    s_inter = jnp.exp(inter - m_t)
    num = jnp.einsum('bhcts,bhcsv->bhctv', wts, v) \
        + s_inter[..., None] * jnp.einsum('bhctk,bhckv->bhctv', q, C_prev)
    den = jnp.sum(wts, axis=-1) + s_inter * jnp.einsum('bhctk,bhck->bhct', q, n_prev)
    h = num / jnp.maximum(jnp.abs(den), jnp.exp(-m_t))[..., None]
    return h.reshape(B, H, S, DV)


def mixer(h, w_in, b_in, conv_mix_w, mlstm_head_g, w_proj_conv, w_proj_mlstm, w_out):
    B, S, _ = h.shape
    proj = h @ w_in + b_in
    xin, bg, cg, q, k, v, o, ig, fg, gate_c, gate_m = jnp.split(proj, _split_points(), axis=-1)
    y_conv = bg * causal_dwconv(cg * xin, conv_mix_w)
    p_conv = y_conv @ w_proj_conv
    H = MLSTM_HEADS
    to_heads = lambda t, d: t.reshape(B, S, H, d).transpose(0, 2, 1, 3)
    hm = mlstm_chunkwise(to_heads(q, MLSTM_DQK), to_heads(k, MLSTM_DQK), to_heads(v, MLSTM_DV),
                         ig.transpose(0, 2, 1), fg.transpose(0, 2, 1))
    hm = hm.transpose(0, 2, 1, 3).astype(h.dtype)
    hm = rmsnorm(hm, mlstm_head_g)
    hm = jax.nn.sigmoid(o).reshape(B, S, H, MLSTM_DV) * hm
    p_mlstm = hm.reshape(B, S, MLSTM_DIM) @ w_proj_mlstm
    z = jax.nn.sigmoid(gate_c) * p_conv + jax.nn.sigmoid(gate_m) * p_mlstm
    return z @ w_out


def conv_ffn(h, w_up, conv_ffn_w, w_down):
    up = h @ w_up
    a, gt = jnp.split(up, 2, axis=-1)
    a = causal_dwconv(a, conv_ffn_w)
    return (jax.nn.silu(a) * gt) @ w_down


def setup_inputs(seed: int = 0) -> dict:
    key = jax.random.key(seed)
    ks = jax.random.split(key, 20)
    f32 = jnp.float32
    nrm = lambda k, shape, scale: jax.random.normal(k, shape, f32) * scale
    b_in = nrm(ks[6], (DEPTH, N_IN), 0.01)
    f_bias = jnp.linspace(3.0, 6.0, MLSTM_HEADS, dtype=f32)
    b_in = b_in.at[:, F_GATE_OFFSET:F_GATE_OFFSET + MLSTM_HEADS].add(f_bias)
    return {
        "x": nrm(ks[0], (BATCH, SEQ, D_MODEL), 1.0),
        "c": nrm(ks[1], (BATCH, D_MODEL), 1.0),
        "w_ada": nrm(ks[2], (DEPTH, D_MODEL, N_MOD * D_MODEL), 0.5 * D_MODEL ** -0.5),
        "b_ada": nrm(ks[3], (DEPTH, N_MOD * D_MODEL), 0.01),
        "g_norm_mix": 1.0 + nrm(ks[4], (DEPTH, D_MODEL), 0.05),
        "w_in": nrm(ks[5], (DEPTH, D_MODEL, N_IN), D_MODEL ** -0.5),
        "b_in": b_in,
        "conv_mix_w": nrm(ks[7], (DEPTH, CONV_K, CONV_DIM), CONV_K ** -0.5),
        "mlstm_head_g": 1.0 + nrm(ks[8], (DEPTH, MLSTM_HEADS, MLSTM_DV), 0.05),
        "w_proj_conv": nrm(ks[9], (DEPTH, CONV_DIM, D_MODEL), CONV_DIM ** -0.5),
        "w_proj_mlstm": nrm(ks[10], (DEPTH, MLSTM_DIM, D_MODEL), MLSTM_DIM ** -0.5),
        "w_out": nrm(ks[11], (DEPTH, D_MODEL, D_MODEL), D_MODEL ** -0.5),
        "g_norm_ffn": 1.0 + nrm(ks[12], (DEPTH, D_MODEL), 0.05),
        "w_up": nrm(ks[13], (DEPTH, D_MODEL, 2 * D_FF), D_MODEL ** -0.5),
        "conv_ffn_w": nrm(ks[14], (DEPTH, CONV_K, D_FF), CONV_K ** -0.5),
        "w_down": nrm(ks[15], (DEPTH, D_FF, D_MODEL), D_FF ** -0.5),
        "g_final": 1.0 + nrm(ks[16], (D_MODEL,), 0.05),
    }


def reference(x, c, w_ada, b_ada, g_norm_mix, w_in, b_in, conv_mix_w, mlstm_head_g,
              w_proj_conv, w_proj_mlstm, w_out, g_norm_ffn, w_up, conv_ffn_w, w_down, g_final):
    for l in range(DEPTH):
        mod = jax.nn.silu(c) @ w_ada[l] + b_ada[l]
        sh1, sc1, gt1, sh2, sc2, gt2 = jnp.split(mod[:, None, :], N_MOD, axis=-1)
        h = rmsnorm(x, g_norm_mix[l]) * (1.0 + sc1) + sh1
        x = x + gt1 * mixer(h, w_in[l], b_in[l], conv_mix_w[l], mlstm_head_g[l],
                            w_proj_conv[l], w_proj_mlstm[l], w_out[l])
        h = rmsnorm(x, g_norm_ffn[l]) * (1.0 + sc2) + sh2
        x = x + gt2 * conv_ffn(h, w_up[l], conv_ffn_w[l], w_down[l])
    return rmsnorm(x, g_final)
```

```python
import functools

import jax
import jax.numpy as jnp
from jax import lax
from jax.experimental import pallas as pl
from jax.experimental.pallas import tpu as pltpu

F32 = jnp.float32
BF16 = jnp.bfloat16

D_MODEL = 1024
CONV_DIM = 1024
CONV_K = 3
HEADS = 4
DQK = 128
DV = 256
MLSTM_DIM = HEADS * DV
CHUNK = 128
D_FF = 2816
N_MOD = 6
EPS = 1e-6

SUBLANES = 8
TM_MIX = 512
TM_FFN = 512
VMEM_LIMIT_BYTES = 56 * 1024 * 1024


def _dot(a, b):
    return jnp.dot(a, b, preferred_element_type=F32)


def _dot_nt(a, b):
    return lax.dot_general(a, b, (((1,), (1,)), ((), ())), preferred_element_type=F32)


def _rms_scale(x):
    return x * lax.rsqrt(jnp.mean(x * x, axis=-1, keepdims=True) + EPS)


def _log_sigmoid(x):
    return -(jnp.maximum(-x, 0.0) + jnp.log1p(jnp.exp(-jnp.abs(x))))


def _ada_kernel(c_ref, w_ref, b_ref, o_ref):
    c = c_ref[...]
    act = c * jax.nn.sigmoid(c)
    o_ref[...] = _dot(act.astype(BF16), w_ref[...].astype(BF16)) + b_ref[...]


def _ada_call(c_pad, w_ada, b_ada):
    n = w_ada.shape[1]
    tn = 1536
    return pl.pallas_call(
        _ada_kernel,
        grid=(n // tn,),
        in_specs=[
            pl.BlockSpec((SUBLANES, D_MODEL), lambda i: (0, 0)),
            pl.BlockSpec((D_MODEL, tn), lambda i: (0, i)),
            pl.BlockSpec((1, tn), lambda i: (0, i)),
        ],
        out_specs=pl.BlockSpec((SUBLANES, tn), lambda i: (0, i)),
        out_shape=jax.ShapeDtypeStruct((SUBLANES, n), F32),
        compiler_params=pltpu.CompilerParams(dimension_semantics=("arbitrary",)),
        name="adaln_mod",
    )(c_pad, w_ada, b_ada)


def _mixer_kernel(x_ref, mod_ref, g_ref,
                  w_xbc, b_xbc, w_qk, b_qk, w_vo, b_vo, w_ift, b_ift, w_gt, b_gt,
                  cw_ref, hg_ref, w_pc, w_pm, w_o,
                  o_ref,
                  ubuf, qk_s, vo_s, hm_s, if_s, lf_s, c_s, n_s, m_s):
    tm = TM_MIX
    j = pl.program_id(1)

    @pl.when(j == 0)
    def _():
        ubuf[0:SUBLANES, :] = jnp.zeros((SUBLANES, CONV_DIM), F32)
        c_s[...] = jnp.zeros_like(c_s)
        n_s[...] = jnp.zeros_like(n_s)
        m_s[...] = jnp.zeros_like(m_s)

    x = x_ref[0]
    sh1 = mod_ref[0, 0:1, :]
    sc1 = mod_ref[0, 1:2, :]
    gt1 = mod_ref[0, 2:3, :]
    h = (_rms_scale(x) * g_ref[...]) * (1.0 + sc1) + sh1
    hb = h.astype(BF16)

    xbc = _dot(hb, w_xbc[...]) + b_xbc[...]
    xin = xbc[:, 0:CONV_DIM]
    bg = xbc[:, CONV_DIM:2 * CONV_DIM]
    cg = xbc[:, 2 * CONV_DIM:3 * CONV_DIM]
    ubuf[SUBLANES:SUBLANES + tm, :] = cg * xin
    u0 = ubuf[SUBLANES:SUBLANES + tm, :]
    u1 = ubuf[SUBLANES - 1:SUBLANES - 1 + tm, :]
    u2 = ubuf[SUBLANES - 2:SUBLANES - 2 + tm, :]
    y_conv = bg * (u2 * cw_ref[0:1, :] + u1 * cw_ref[1:2, :] + u0 * cw_ref[2:3, :])
    ubuf[0:SUBLANES, :] = ubuf[tm:tm + SUBLANES, :]
    p_conv = _dot(y_conv.astype(BF16), w_pc[...])
    gates = _dot(hb, w_gt[...]) + b_gt[...]
    zc = jax.nn.sigmoid(gates[:, 0:D_MODEL]) * p_conv
    sig_gm = jax.nn.sigmoid(gates[:, D_MODEL:2 * D_MODEL])

    qk_s[...] = _dot(hb, w_qk[...]) + b_qk[...]
    vo_s[...] = _dot(hb, w_vo[...]) + b_vo[...]
    ift = _dot_nt(w_ift[...], hb) + b_ift[...]
    lft = _log_sigmoid(ift)
    for c in range(tm // CHUNK):
        if_s[c] = ift[:, c * CHUNK:(c + 1) * CHUNK]
        lf_s[c] = lft[:, c * CHUNK:(c + 1) * CHUNK]

    row = lax.broadcasted_iota(jnp.int32, (CHUNK, CHUNK), 0)
    col = lax.broadcasted_iota(jnp.int32, (CHUNK, CHUNK), 1)
    tri = col <= row
    eye = col == row
    lane8 = lax.broadcasted_iota(jnp.int32, (SUBLANES, CHUNK), 1)

    def to_col(r):
        return jnp.sum(jnp.where(eye, r, 0.0), axis=1, keepdims=True)

    def chunk_body(ci, carry):
        r0 = pl.multiple_of(ci * CHUNK, CHUNK)
        li8 = if_s[ci]
        b8 = lf_s[ci]
        d = 1
        while d < CHUNK:
            b8 = b8 + jnp.where(lane8 >= d, pltpu.roll(b8, d, axis=1), 0.0)
            d *= 2
        for hd in range(HEADS):
            li_row = li8[hd:hd + 1, :]
            b_row = b8[HEADS + hd:HEADS + hd + 1, :]
            g = b_row[:, CHUNK - 1:CHUNK]
            b_col = to_col(b_row)
            li_col = to_col(li_row)
            m_prev = m_s[hd:hd + 1, 0:1]
            n_prev = n_s[hd:hd + 1, :]
            c_prev = c_s[hd]

            q = qk_s[pl.ds(r0, CHUNK), hd * DQK:(hd + 1) * DQK] * (DQK ** -0.5)
            k = qk_s[pl.ds(r0, CHUNK), HEADS * DQK + hd * DQK:HEADS * DQK + (hd + 1) * DQK]
            v = vo_s[pl.ds(r0, CHUNK), hd * DV:(hd + 1) * DV]
            qb = q.astype(BF16)
            kb = k.astype(BF16)
            vb = v.astype(BF16)

            dlog = jnp.where(tri, b_col - b_row + li_row, -jnp.inf)
            inter = b_col + m_prev
            m_t = jnp.maximum(inter, jnp.max(dlog, axis=1, keepdims=True))
            wts = jnp.exp(dlog - m_t) * _dot_nt(qb, kb)
            s_inter = jnp.exp(inter - m_t)
            num = _dot(wts.astype(BF16), vb) + s_inter * _dot(qb, c_prev.astype(BF16))
            den = (jnp.sum(wts, axis=1, keepdims=True)
                   + s_inter * jnp.sum(q * n_prev, axis=1, keepdims=True))
            hm_s[pl.ds(r0, CHUNK), hd * DV:(hd + 1) * DV] = num / jnp.maximum(jnp.abs(den), jnp.exp(-m_t))

            a_row = g - b_row + li_row
            a_col = g - b_col + li_col
            m_loc = jnp.max(a_row, axis=1, keepdims=True)
            kw = jnp.exp(a_col - m_loc) * k
            c_loc = _dot(kw.T.astype(BF16), vb)
            n_loc = jnp.sum(kw, axis=0, keepdims=True)
            m_new = jnp.maximum(g + m_prev, m_loc)
            s_old = jnp.exp(g + m_prev - m_new)
            s_new = jnp.exp(m_loc - m_new)
            c_s[hd] = s_old * c_prev + s_new * c_loc
            n_s[hd:hd + 1, :] = s_old * n_prev + s_new * n_loc
            m_s[hd:hd + 1, :] = jnp.broadcast_to(m_new, (1, CHUNK))
        return carry

    lax.fori_loop(0, tm // CHUNK, chunk_body, 0)

    o_gate = jax.nn.sigmoid(vo_s[:, MLSTM_DIM:2 * MLSTM_DIM])
    parts = []
    for hd in range(HEADS):
        hm_h = hm_s[:, hd * DV:(hd + 1) * DV]
        parts.append(_rms_scale(hm_h) * hg_ref[:, hd * DV:(hd + 1) * DV])
    hm = o_gate * jnp.concatenate(parts, axis=1)
    p_m = _dot(hm.astype(BF16), w_pm[...])
    z = zc + sig_gm * p_m
    o_ref[0] = x + gt1 * _dot(z.astype(BF16), w_o[...])


def _const_spec(shape):
    nd = len(shape)
    return pl.BlockSpec(shape, lambda b, j: (0,) * nd, pipeline_mode=pl.Buffered(1))


def _mixer_call(x, mod, g_norm, w_xbc, b_xbc, w_qk, b_qk, w_vo, b_vo, w_ift, b_ift,
                w_gt, b_gt, conv_w, head_g, w_pc, w_pm, w_o):
    bsz, seq, d = x.shape
    tm = TM_MIX
    consts = (g_norm, w_xbc, b_xbc, w_qk, b_qk, w_vo, b_vo, w_ift, b_ift, w_gt, b_gt,
              conv_w, head_g, w_pc, w_pm, w_o)
    return pl.pallas_call(
        _mixer_kernel,
        grid=(bsz, seq // tm),
        in_specs=[
            pl.BlockSpec((1, tm, d), lambda b, j: (b, j, 0)),
            pl.BlockSpec((1, N_MOD, d), lambda b, j: (b, 0, 0)),
        ] + [_const_spec(a.shape) for a in consts],
        out_specs=pl.BlockSpec((1, tm, d), lambda b, j: (b, j, 0)),
        out_shape=jax.ShapeDtypeStruct(x.shape, F32),
        scratch_shapes=[
            pltpu.VMEM((tm + SUBLANES, CONV_DIM), F32),
            pltpu.VMEM((tm, 2 * HEADS * DQK), F32),
            pltpu.VMEM((tm, 2 * MLSTM_DIM), F32),
            pltpu.VMEM((tm, MLSTM_DIM), F32),
            pltpu.VMEM((tm // CHUNK, SUBLANES, CHUNK), F32),
            pltpu.VMEM((tm // CHUNK, SUBLANES, CHUNK), F32),
            pltpu.VMEM((HEADS, DQK, DV), F32),
            pltpu.VMEM((SUBLANES, DQK), F32),
            pltpu.VMEM((SUBLANES, CHUNK), F32),
        ],
        compiler_params=pltpu.CompilerParams(
            dimension_semantics=("arbitrary", "arbitrary"),
            vmem_limit_bytes=VMEM_LIMIT_BYTES),
        name="mixer_sublayer",
    )(x, mod, *consts)


def _ffn_kernel(x_ref, mod_ref, g_ref, w_a, w_g, cw_ref, w_d, gf_ref, o_ref, abuf, *, final_norm):
    tm = TM_FFN
    j = pl.program_id(1)

    @pl.when(j == 0)
    def _():
        abuf[0:SUBLANES, :] = jnp.zeros((SUBLANES, D_FF), F32)

    x = x_ref[0]
    sh2 = mod_ref[0, 3:4, :]
    sc2 = mod_ref[0, 4:5, :]
    gt2 = mod_ref[0, 5:6, :]
    h = (_rms_scale(x) * g_ref[...]) * (1.0 + sc2) + sh2
    hb = h.astype(BF16)
    abuf[SUBLANES:SUBLANES + tm, :] = _dot(hb, w_a[...])
    gate = _dot(hb, w_g[...])
    a0 = abuf[SUBLANES:SUBLANES + tm, :]
    a1 = abuf[SUBLANES - 1:SUBLANES - 1 + tm, :]
    a2 = abuf[SUBLANES - 2:SUBLANES - 2 + tm, :]
    ac = a2 * cw_ref[0:1, :] + a1 * cw_ref[1:2, :] + a0 * cw_ref[2:3, :]
    abuf[0:SUBLANES, :] = abuf[tm:tm + SUBLANES, :]
    act = (ac * jax.nn.sigmoid(ac)) * gate
    x2 = x + gt2 * _dot(act.astype(BF16), w_d[...])
    o_ref[0] = _rms_scale(x2) * gf_ref[...] if final_norm else x2


def _ffn_call(x, mod, g_norm, w_a, w_g, conv_w, w_d, g_final, final_norm):
    bsz, seq, d = x.shape
    tm = TM_FFN
    consts = (g_norm, w_a, w_g, conv_w, w_d, g_final)
    return pl.pallas_call(
        functools.partial(_ffn_kernel, final_norm=final_norm),
        grid=(bsz, seq // tm),
        in_specs=[
            pl.BlockSpec((1, tm, d), lambda b, j: (b, j, 0)),
            pl.BlockSpec((1, N_MOD, d), lambda b, j: (b, 0, 0)),
        ] + [_const_spec(a.shape) for a in consts],
        out_specs=pl.BlockSpec((1, tm, d), lambda b, j: (b, j, 0)),
        out_shape=jax.ShapeDtypeStruct(x.shape, F32),
        scratch_shapes=[pltpu.VMEM((tm + SUBLANES, D_FF), F32)],
        compiler_params=pltpu.CompilerParams(
            dimension_semantics=("arbitrary", "arbitrary"),
            vmem_limit_bytes=VMEM_LIMIT_BYTES),
        name="ffn_sublayer",
    )(x, mod, *consts)


def kernel(x, c, w_ada, b_ada, g_norm_mix, w_in, b_in, conv_mix_w, mlstm_head_g,
           w_proj_conv, w_proj_mlstm, w_out, g_norm_ffn, w_up, conv_ffn_w, w_down, g_final):
    depth = w_ada.shape[0]
    bsz = x.shape[0]
    o_xbc = 3 * CONV_DIM
    o_qk = o_xbc + 2 * HEADS * DQK
    o_vo = o_qk + 2 * MLSTM_DIM
    o_if = o_vo + 2 * HEADS
    for l in range(depth):
        c_pad = jnp.pad(c, ((0, SUBLANES - bsz), (0, 0)))
        mod = _ada_call(c_pad, w_ada[l], b_ada[l][None, :])[:bsz]
        mod = mod.reshape(bsz, N_MOD, D_MODEL)

        wi = w_in[l]
        bi = b_in[l]
        x = _mixer_call(
            x, mod, g_norm_mix[l][None, :],
            wi[:, 0:o_xbc].astype(BF16), bi[None, 0:o_xbc],
            wi[:, o_xbc:o_qk].astype(BF16), bi[None, o_xbc:o_qk],
            wi[:, o_qk:o_vo].astype(BF16), bi[None, o_qk:o_vo],
            wi[:, o_vo:o_if].T.astype(BF16), bi[o_vo:o_if, None],
            wi[:, o_if:].astype(BF16), bi[None, o_if:],
            conv_mix_w[l], mlstm_head_g[l].reshape(1, MLSTM_DIM),
            w_proj_conv[l].astype(BF16), w_proj_mlstm[l].astype(BF16), w_out[l].astype(BF16))

        x = _ffn_call(
            x, mod, g_norm_ffn[l][None, :],
            w_up[l][:, 0:D_FF].astype(BF16), w_up[l][:, D_FF:].astype(BF16),
            conv_ffn_w[l], w_down[l].astype(BF16), g_final[None, :],
            final_norm=(l == depth - 1))
    return x
```

```python
import functools

import jax
import jax.numpy as jnp
from jax import lax
from jax.experimental import pallas as pl
from jax.experimental.pallas import tpu as pltpu

F32 = jnp.float32
BF16 = jnp.bfloat16

D_MODEL = 1024
CONV_DIM = 1024
CONV_K = 3
HEADS = 4
DQK = 128
DV = 256
MLSTM_DIM = HEADS * DV
CHUNK = 128
D_FF = 2816
N_MOD = 6
EPS = 1e-6

SUBLANES = 8
TM_MIX = 512
TM_FFN = 512
MIX_BLOCK = 256
VMEM_LIMIT_BYTES = 56 * 1024 * 1024


def _dot(a, b):
    return jnp.dot(a, b, preferred_element_type=F32)


def _dot_nt(a, b):
    return lax.dot_general(a, b, (((1,), (1,)), ((), ())), preferred_element_type=F32)


def _rms_scale(x):
    return x * lax.rsqrt(jnp.mean(x * x, axis=-1, keepdims=True) + EPS)


def _log_sigmoid(x):
    return -(jnp.maximum(-x, 0.0) + jnp.log1p(jnp.exp(-jnp.abs(x))))


def _ada_kernel(c_ref, w_ref, b_ref, o_ref):
    c = c_ref[...]
    act = c * jax.nn.sigmoid(c)
    o_ref[...] = _dot(act.astype(BF16), w_ref[...].astype(BF16)) + b_ref[...]


def _ada_call(c_pad, w_ada, b_ada):
    n = w_ada.shape[1]
    tn = 1536
    return pl.pallas_call(
        _ada_kernel,
        grid=(n // tn,),
        in_specs=[
            pl.BlockSpec((SUBLANES, D_MODEL), lambda i: (0, 0)),
            pl.BlockSpec((D_MODEL, tn), lambda i: (0, i)),
            pl.BlockSpec((1, tn), lambda i: (0, i)),
        ],
        out_specs=pl.BlockSpec((SUBLANES, tn), lambda i: (0, i)),
        out_shape=jax.ShapeDtypeStruct((SUBLANES, n), F32),
        compiler_params=pltpu.CompilerParams(dimension_semantics=("arbitrary",)),
        name="adaln_mod",
    )(c_pad, w_ada, b_ada)


def _mixer_kernel(x_ref, mod_ref, g_ref,
                  w_xbc, b_xbc, w_qk, b_qk, w_vo, b_vo, w_ift, b_ift, w_gt, b_gt,
                  cw_ref, hg_ref, w_pc, w_pm, w_o,
                  o_ref,
                  hb_s, ubuf, y_s, sg_s, qk_s, vo_s, hm_s, c_s, n_s, m_s):
    tm = TM_MIX
    j = pl.program_id(1)

    @pl.when(j == 0)
    def _():
        ubuf[0:SUBLANES, :] = jnp.zeros((SUBLANES, CONV_DIM), F32)
        c_s[...] = jnp.zeros_like(c_s)
        n_s[...] = jnp.zeros_like(n_s)
        m_s[...] = jnp.zeros_like(m_s)

    x = x_ref[0]
    sh1 = mod_ref[0, 0:1, :]
    sc1 = mod_ref[0, 1:2, :]
    gt1 = mod_ref[0, 2:3, :]
    h = (_rms_scale(x) * g_ref[...]) * (1.0 + sc1) + sh1
    hb_s[...] = h.astype(BF16)

    ift = _dot_nt(w_ift[...], hb_s[...]) + b_ift[...]
    lft = _log_sigmoid(ift)

    row = lax.broadcasted_iota(jnp.int32, (CHUNK, CHUNK), 0)
    col = lax.broadcasted_iota(jnp.int32, (CHUNK, CHUNK), 1)
    tri = col <= row
    eye = col == row
    lane8 = lax.broadcasted_iota(jnp.int32, (SUBLANES, CHUNK), 1)

    def to_col(r):
        return jnp.sum(jnp.where(eye, r, 0.0), axis=1, keepdims=True)

    def chunk_decay(ci):
        li8 = ift[:, ci * CHUNK:(ci + 1) * CHUNK]
        b8 = lft[:, ci * CHUNK:(ci + 1) * CHUNK]
        d = 1
        while d < CHUNK:
            b8 = b8 + jnp.where(lane8 >= d, pltpu.roll(b8, d, axis=1), 0.0)
            d *= 2
        return li8, b8

    decays = [chunk_decay(ci) for ci in range(tm // CHUNK)]
    qk_s[...] = _dot(hb_s[...], w_qk[...]) + b_qk[...]
    vo_s[...] = _dot(hb_s[...], w_vo[...]) + b_vo[...]

    def mlstm_pair(ci, hd, li8, b8):
        r0 = ci * CHUNK
        li_row = li8[hd:hd + 1, :]
        b_row = b8[HEADS + hd:HEADS + hd + 1, :]
        g = b_row[:, CHUNK - 1:CHUNK]
        b_col = to_col(b_row)
        li_col = to_col(li_row)
        m_prev = m_s[hd:hd + 1, 0:1]
        n_prev = n_s[hd:hd + 1, :]
        c_prev = c_s[hd]

        q = qk_s[r0:r0 + CHUNK, hd * DQK:(hd + 1) * DQK] * (DQK ** -0.5)
        k = qk_s[r0:r0 + CHUNK, HEADS * DQK + hd * DQK:HEADS * DQK + (hd + 1) * DQK]
        v = vo_s[r0:r0 + CHUNK, hd * DV:(hd + 1) * DV]
        qb = q.astype(BF16)
        kb = k.astype(BF16)
        vb = v.astype(BF16)

        dlog = jnp.where(tri, b_col - b_row + li_row, -jnp.inf)
        inter = b_col + m_prev
        m_t = jnp.maximum(inter, jnp.max(dlog, axis=1, keepdims=True))
        wts = jnp.exp(dlog - m_t) * _dot_nt(qb, kb)
        s_inter = jnp.exp(inter - m_t)
        num = _dot(wts.astype(BF16), vb) + s_inter * _dot(qb, c_prev.astype(BF16))
        den = (jnp.sum(wts, axis=1, keepdims=True)
               + s_inter * jnp.sum(q * n_prev, axis=1, keepdims=True))
        hm_s[r0:r0 + CHUNK, hd * DV:(hd + 1) * DV] = num / jnp.maximum(jnp.abs(den), jnp.exp(-m_t))

        a_row = g - b_row + li_row
        a_col = g - b_col + li_col
        m_loc = jnp.max(a_row, axis=1, keepdims=True)
        kw = jnp.exp(a_col - m_loc) * k
        c_loc = _dot(kw.T.astype(BF16), vb)
        n_loc = jnp.sum(kw, axis=0, keepdims=True)
        m_new = jnp.maximum(g + m_prev, m_loc)
        s_old = jnp.exp(g + m_prev - m_new)
        s_new = jnp.exp(m_loc - m_new)
        c_s[hd] = s_old * c_prev + s_new * c_loc
        n_s[hd:hd + 1, :] = s_old * n_prev + s_new * n_loc
        m_s[hd:hd + 1, :] = jnp.broadcast_to(m_new, (1, CHUNK))

    def conv_block(jb):
        lo = jb * MIX_BLOCK
        cs = slice(lo, lo + MIX_BLOCK)

        def proj(off):
            return (_dot(hb_s[...], w_xbc[:, off + lo:off + lo + MIX_BLOCK])
                    + b_xbc[:, off + lo:off + lo + MIX_BLOCK])

        xin = proj(0)
        bg = proj(CONV_DIM)
        cg = proj(2 * CONV_DIM)
        ubuf[SUBLANES:SUBLANES + tm, cs] = cg * xin
        u0 = ubuf[SUBLANES:SUBLANES + tm, cs]
        u1 = ubuf[SUBLANES - 1:SUBLANES - 1 + tm, cs]
        u2 = ubuf[SUBLANES - 2:SUBLANES - 2 + tm, cs]
        y = bg * (u2 * cw_ref[0:1, cs] + u1 * cw_ref[1:2, cs] + u0 * cw_ref[2:3, cs])
        y_s[:, cs] = y.astype(BF16)
        ubuf[0:SUBLANES, cs] = ubuf[tm:tm + SUBLANES, cs]

    def gate_block(jb):
        cs = slice(jb * MIX_BLOCK, (jb + 1) * MIX_BLOCK)
        sg_s[:, cs] = jax.nn.sigmoid(_dot(hb_s[...], w_gt[:, cs]) + b_gt[:, cs])

    def pconv_block(jb):
        cs = slice(jb * MIX_BLOCK, (jb + 1) * MIX_BLOCK)
        sg_s[:, cs] = sg_s[:, cs] * _dot(y_s[...], w_pc[:, cs])

    nb = CONV_DIM // MIX_BLOCK
    mxu_items = ([functools.partial(conv_block, jb) for jb in range(nb)]
                 + [functools.partial(gate_block, jb) for jb in range(2 * nb)]
                 + [functools.partial(pconv_block, jb) for jb in range(nb)])

    pairs = [(ci, hd) for ci in range(tm // CHUNK) for hd in range(HEADS)]
    for idx, (ci, hd) in enumerate(pairs):
        mlstm_pair(ci, hd, *decays[ci])
        lo = idx * len(mxu_items) // len(pairs)
        hi = (idx + 1) * len(mxu_items) // len(pairs)
        for item in mxu_items[lo:hi]:
            item()

    o_gate = jax.nn.sigmoid(vo_s[:, MLSTM_DIM:2 * MLSTM_DIM])
    parts = []
    for hd in range(HEADS):
        hm_h = hm_s[:, hd * DV:(hd + 1) * DV]
        parts.append(_rms_scale(hm_h) * hg_ref[:, hd * DV:(hd + 1) * DV])
    hm = o_gate * jnp.concatenate(parts, axis=1)
    p_m = _dot(hm.astype(BF16), w_pm[...])
    z = sg_s[:, 0:D_MODEL] + sg_s[:, D_MODEL:2 * D_MODEL] * p_m
    o_ref[0] = x + gt1 * _dot(z.astype(BF16), w_o[...])


def _const_spec(shape):
    nd = len(shape)
    return pl.BlockSpec(shape, lambda b, j: (0,) * nd, pipeline_mode=pl.Buffered(1))


def _mixer_call(x, mod, g_norm, w_xbc, b_xbc, w_qk, b_qk, w_vo, b_vo, w_ift, b_ift,
                w_gt, b_gt, conv_w, head_g, w_pc, w_pm, w_o):
    bsz, seq, d = x.shape
    tm = TM_MIX
    consts = (g_norm, w_xbc, b_xbc, w_qk, b_qk, w_vo, b_vo, w_ift, b_ift, w_gt, b_gt,
              conv_w, head_g, w_pc, w_pm, w_o)
    return pl.pallas_call(
        _mixer_kernel,
        grid=(bsz, seq // tm),
        in_specs=[
            pl.BlockSpec((1, tm, d), lambda b, j: (b, j, 0)),
            pl.BlockSpec((1, N_MOD, d), lambda b, j: (b, 0, 0)),
        ] + [_const_spec(a.shape) for a in consts],
        out_specs=pl.BlockSpec((1, tm, d), lambda b, j: (b, j, 0)),
        out_shape=jax.ShapeDtypeStruct(x.shape, F32),
        scratch_shapes=[
            pltpu.VMEM((tm, D_MODEL), BF16),
            pltpu.VMEM((tm + SUBLANES, CONV_DIM), F32),
            pltpu.VMEM((tm, CONV_DIM), BF16),
            pltpu.VMEM((tm, 2 * D_MODEL), F32),
            pltpu.VMEM((tm, 2 * HEADS * DQK), F32),
            pltpu.VMEM((tm, 2 * MLSTM_DIM), F32),
            pltpu.VMEM((tm, MLSTM_DIM), F32),
            pltpu.VMEM((HEADS, DQK, DV), F32),
            pltpu.VMEM((SUBLANES, DQK), F32),
            pltpu.VMEM((SUBLANES, CHUNK), F32),
        ],
        compiler_params=pltpu.CompilerParams(
            dimension_semantics=("arbitrary", "arbitrary"),
            vmem_limit_bytes=VMEM_LIMIT_BYTES),
        name="mixer_sublayer",
    )(x, mod, *consts)


def _ffn_kernel(x_ref, mod_ref, g_ref, w_a, w_g, cw_ref, w_d, gf_ref, o_ref, abuf, *, final_norm):
    tm = TM_FFN
    j = pl.program_id(1)

    @pl.when(j == 0)
    def _():
        abuf[0:SUBLANES, :] = jnp.zeros((SUBLANES, D_FF), F32)

    x = x_ref[0]
    sh2 = mod_ref[0, 3:4, :]
    sc2 = mod_ref[0, 4:5, :]
    gt2 = mod_ref[0, 5:6, :]
    h = (_rms_scale(x) * g_ref[...]) * (1.0 + sc2) + sh2
    hb = h.astype(BF16)
    abuf[SUBLANES:SUBLANES + tm, :] = _dot(hb, w_a[...])
    gate = _dot(hb, w_g[...])
    a0 = abuf[SUBLANES:SUBLANES + tm, :]
    a1 = abuf[SUBLANES - 1:SUBLANES - 1 + tm, :]
    a2 = abuf[SUBLANES - 2:SUBLANES - 2 + tm, :]
    ac = a2 * cw_ref[0:1, :] + a1 * cw_ref[1:2, :] + a0 * cw_ref[2:3, :]
    abuf[0:SUBLANES, :] = abuf[tm:tm + SUBLANES, :]
    act = (ac * jax.nn.sigmoid(ac)) * gate
    x2 = x + gt2 * _dot(act.astype(BF16), w_d[...])
    o_ref[0] = _rms_scale(x2) * gf_ref[...] if final_norm else x2


def _ffn_call(x, mod, g_norm, w_a, w_g, conv_w, w_d, g_final, final_norm):
    bsz, seq, d = x.shape
    tm = TM_FFN
    consts = (g_norm, w_a, w_g, conv_w, w_d, g_final)
    return pl.pallas_call(
        functools.partial(_ffn_kernel, final_norm=final_norm),
        grid=(bsz, seq // tm),
        in_specs=[
            pl.BlockSpec((1, tm, d), lambda b, j: (b, j, 0)),
            pl.BlockSpec((1, N_MOD, d), lambda b, j: (b, 0, 0)),
        ] + [_const_spec(a.shape) for a in consts],
        out_specs=pl.BlockSpec((1, tm, d), lambda b, j: (b, j, 0)),
        out_shape=jax.ShapeDtypeStruct(x.shape, F32),
        scratch_shapes=[pltpu.VMEM((tm + SUBLANES, D_FF), F32)],
        compiler_params=pltpu.CompilerParams(
            dimension_semantics=("arbitrary", "arbitrary"),
            vmem_limit_bytes=VMEM_LIMIT_BYTES),
        name="ffn_sublayer",
    )(x, mod, *consts)


def kernel(x, c, w_ada, b_ada, g_norm_mix, w_in, b_in, conv_mix_w, mlstm_head_g,
           w_proj_conv, w_proj_mlstm, w_out, g_norm_ffn, w_up, conv_ffn_w, w_down, g_final):
    depth = w_ada.shape[0]
    bsz = x.shape[0]
    o_xbc = 3 * CONV_DIM
    o_qk = o_xbc + 2 * HEADS * DQK
    o_vo = o_qk + 2 * MLSTM_DIM
    o_if = o_vo + 2 * HEADS
    for l in range(depth):
        c_pad = jnp.pad(c, ((0, SUBLANES - bsz), (0, 0)))
        mod = _ada_call(c_pad, w_ada[l], b_ada[l][None, :])[:bsz]
        mod = mod.reshape(bsz, N_MOD, D_MODEL)

        wi = w_in[l]
        bi = b_in[l]
        x = _mixer_call(
            x, mod, g_norm_mix[l][None, :],
            wi[:, 0:o_xbc].astype(BF16), bi[None, 0:o_xbc],
            wi[:, o_xbc:o_qk].astype(BF16), bi[None, o_xbc:o_qk],
            wi[:, o_qk:o_vo].astype(BF16), bi[None, o_qk:o_vo],
            wi[:, o_vo:o_if].T.astype(BF16), bi[o_vo:o_if, None],
            wi[:, o_if:].astype(BF16), bi[None, o_if:],
            conv_mix_w[l], mlstm_head_g[l].reshape(1, MLSTM_DIM),
            w_proj_conv[l].astype(BF16), w_proj_mlstm[l].astype(BF16), w_out[l].astype(BF16))

        x = _ffn_call(
            x, mod, g_norm_ffn[l][None, :],
            w_up[l][:, 0:D_FF].astype(BF16), w_up[l][:, D_FF:].astype(BF16),
            conv_ffn_w[l], w_down[l].astype(BF16), g_final[None, :],
            final_norm=(l == depth - 1))
    return x
```

```python
import functools

import jax
import jax.numpy as jnp
from jax import lax
from jax.experimental import pallas as pl
from jax.experimental.pallas import tpu as pltpu

F32 = jnp.float32
BF16 = jnp.bfloat16

D_MODEL = 1024
CONV_DIM = 1024
CONV_K = 3
HEADS = 4
DQK = 128
DV = 256
MLSTM_DIM = HEADS * DV
CHUNK = 128
D_FF = 2816
N_MOD = 6
EPS = 1e-6

SUBLANES = 8
TM_MIX = 512
TM_FFN = 512
MIX_BLOCK = 256
VMEM_LIMIT_BYTES = 56 * 1024 * 1024


def _dot(a, b):
    return jnp.dot(a, b, preferred_element_type=F32)


def _dot_nt(a, b):
    return lax.dot_general(a, b, (((1,), (1,)), ((), ())), preferred_element_type=F32)


def _rms_scale(x):
    return x * lax.rsqrt(jnp.mean(x * x, axis=-1, keepdims=True) + EPS)


def _log_sigmoid(x):
    return -(jnp.maximum(-x, 0.0) + jnp.log1p(jnp.exp(-jnp.abs(x))))


def _ada_kernel(c_ref, w_ref, b_ref, o_ref):
    c = c_ref[...]
    act = c * jax.nn.sigmoid(c)
    o_ref[...] = _dot(act.astype(BF16), w_ref[...].astype(BF16)) + b_ref[...]


def _ada_call(c_pad, w_ada, b_ada):
    n = w_ada.shape[1]
    tn = 1536
    return pl.pallas_call(
        _ada_kernel,
        grid=(n // tn,),
        in_specs=[
            pl.BlockSpec((SUBLANES, D_MODEL), lambda i: (0, 0)),
            pl.BlockSpec((D_MODEL, tn), lambda i: (0, i)),
            pl.BlockSpec((1, tn), lambda i: (0, i)),
        ],
        out_specs=pl.BlockSpec((SUBLANES, tn), lambda i: (0, i)),
        out_shape=jax.ShapeDtypeStruct((SUBLANES, n), F32),
        compiler_params=pltpu.CompilerParams(dimension_semantics=("arbitrary",)),
        name="adaln_mod",
    )(c_pad, w_ada, b_ada)


def _mixer_kernel(x_ref, mod_ref, g_ref,
                  w_xbc, b_xbc, w_qk, b_qk, w_vo, b_vo, w_gt, b_gt, w_if, b_if,
                  cw_ref, hg_ref, w_pc, w_pm, w_o,
                  o_ref,
                  hb_s, ubuf, y_s, sg_s, qk_s, vo_s, hm_s, c_s, n_s, m_s):
    tm = TM_MIX
    j = pl.program_id(1)

    @pl.when(j == 0)
    def _():
        ubuf[0:SUBLANES, :] = jnp.zeros((SUBLANES, CONV_DIM), F32)
        c_s[...] = jnp.zeros_like(c_s)
        n_s[...] = jnp.zeros_like(n_s)
        m_s[...] = jnp.zeros_like(m_s)

    x = x_ref[0]
    sh1 = mod_ref[0, 0:1, :]
    sc1 = mod_ref[0, 1:2, :]
    gt1 = mod_ref[0, 2:3, :]
    h = (_rms_scale(x) * g_ref[...]) * (1.0 + sc1) + sh1
    hb_s[...] = h.astype(BF16)

    gif = _dot(hb_s[...], w_if[...]) + b_if[...]

    row = lax.broadcasted_iota(jnp.int32, (CHUNK, CHUNK), 0)
    col = lax.broadcasted_iota(jnp.int32, (CHUNK, CHUNK), 1)
    tri = col <= row
    eye = col == row
    lane8 = lax.broadcasted_iota(jnp.int32, (SUBLANES, CHUNK), 1)

    def to_col(r):
        return jnp.sum(jnp.where(eye, r, 0.0), axis=1, keepdims=True)

    def chunk_decay(ci):
        li8 = gif[ci * CHUNK:(ci + 1) * CHUNK, :].T[0:SUBLANES, :]
        b8 = _log_sigmoid(li8)
        d = 1
        while d < CHUNK:
            b8 = b8 + jnp.where(lane8 >= d, pltpu.roll(b8, d, axis=1), 0.0)
            d *= 2
        return li8, b8

    decays = [chunk_decay(ci) for ci in range(tm // CHUNK)]
    qk_s[...] = _dot(hb_s[...], w_qk[...]) + b_qk[...]
    vo_s[...] = _dot(hb_s[...], w_vo[...]) + b_vo[...]

    def mlstm_pair(ci, hd, li8, b8):
        r0 = ci * CHUNK
        li_row = li8[hd:hd + 1, :]
        b_row = b8[HEADS + hd:HEADS + hd + 1, :]
        g = b_row[:, CHUNK - 1:CHUNK]
        b_col = to_col(b_row)
        li_col = to_col(li_row)
        m_prev = m_s[hd:hd + 1, 0:1]
        n_prev = n_s[hd:hd + 1, :]
        c_prev = c_s[hd]

        q = qk_s[r0:r0 + CHUNK, hd * DQK:(hd + 1) * DQK] * (DQK ** -0.5)
        k = qk_s[r0:r0 + CHUNK, HEADS * DQK + hd * DQK:HEADS * DQK + (hd + 1) * DQK]
        v = vo_s[r0:r0 + CHUNK, hd * DV:(hd + 1) * DV]
        qb = q.astype(BF16)
        kb = k.astype(BF16)
        vb = v.astype(BF16)

        dlog = jnp.where(tri, b_col - b_row + li_row, -jnp.inf)
        inter = b_col + m_prev
        m_t = jnp.maximum(inter, jnp.max(dlog, axis=1, keepdims=True))
        wts = jnp.exp(dlog - m_t) * _dot_nt(qb, kb)
        s_inter = jnp.exp(inter - m_t)
        num = _dot(wts.astype(BF16), vb) + s_inter * _dot(qb, c_prev.astype(BF16))
        den = (jnp.sum(wts, axis=1, keepdims=True)
               + s_inter * jnp.sum(q * n_prev, axis=1, keepdims=True))
        hm_s[r0:r0 + CHUNK, hd * DV:(hd + 1) * DV] = num / jnp.maximum(jnp.abs(den), jnp.exp(-m_t))

        a_row = g - b_row + li_row
        a_col = g - b_col + li_col
        m_loc = jnp.max(a_row, axis=1, keepdims=True)
        kw = jnp.exp(a_col - m_loc) * k
        c_loc = _dot(kw.T.astype(BF16), vb)
        n_loc = jnp.sum(kw, axis=0, keepdims=True)
        m_new = jnp.maximum(g + m_prev, m_loc)
        s_old = jnp.exp(g + m_prev - m_new)
        s_new = jnp.exp(m_loc - m_new)
        c_s[hd] = s_old * c_prev + s_new * c_loc
        n_s[hd:hd + 1, :] = s_old * n_prev + s_new * n_loc
        m_s[hd:hd + 1, :] = jnp.broadcast_to(m_new, (1, CHUNK))

    def conv_block(jb):
        lo = jb * MIX_BLOCK
        cs = slice(lo, lo + MIX_BLOCK)

        def proj(off):
            return (_dot(hb_s[...], w_xbc[:, off + lo:off + lo + MIX_BLOCK])
                    + b_xbc[:, off + lo:off + lo + MIX_BLOCK])

        xin = proj(0)
        bg = proj(CONV_DIM)
        cg = proj(2 * CONV_DIM)
        ubuf[SUBLANES:SUBLANES + tm, cs] = cg * xin
        u0 = ubuf[SUBLANES:SUBLANES + tm, cs]
        u1 = ubuf[SUBLANES - 1:SUBLANES - 1 + tm, cs]
        u2 = ubuf[SUBLANES - 2:SUBLANES - 2 + tm, cs]
        y = bg * (u2 * cw_ref[0:1, cs] + u1 * cw_ref[1:2, cs] + u0 * cw_ref[2:3, cs])
        y_s[:, cs] = y.astype(BF16)
        ubuf[0:SUBLANES, cs] = ubuf[tm:tm + SUBLANES, cs]

    def gate_block(jb):
        cs = slice(jb * MIX_BLOCK, (jb + 1) * MIX_BLOCK)
        sg_s[:, cs] = jax.nn.sigmoid(_dot(hb_s[...], w_gt[:, cs]) + b_gt[:, cs])

    def pconv_block(jb):
        cs = slice(jb * MIX_BLOCK, (jb + 1) * MIX_BLOCK)
        sg_s[:, cs] = sg_s[:, cs] * _dot(y_s[...], w_pc[:, cs])

    nb = CONV_DIM // MIX_BLOCK
    mxu_items = ([functools.partial(conv_block, jb) for jb in range(nb)]
                 + [functools.partial(gate_block, jb) for jb in range(2 * nb)]
                 + [functools.partial(pconv_block, jb) for jb in range(nb)])

    pairs = [(ci, hd) for ci in range(tm // CHUNK) for hd in range(HEADS)]
    for idx, (ci, hd) in enumerate(pairs):
        mlstm_pair(ci, hd, *decays[ci])
        lo = idx * len(mxu_items) // len(pairs)
        hi = (idx + 1) * len(mxu_items) // len(pairs)
        for item in mxu_items[lo:hi]:
            item()

    o_gate = jax.nn.sigmoid(vo_s[:, MLSTM_DIM:2 * MLSTM_DIM])
    parts = []
    for hd in range(HEADS):
        hm_h = hm_s[:, hd * DV:(hd + 1) * DV]
        parts.append(_rms_scale(hm_h) * hg_ref[:, hd * DV:(hd + 1) * DV])
    hm = o_gate * jnp.concatenate(parts, axis=1)
    p_m = _dot(hm.astype(BF16), w_pm[...])
    z = sg_s[:, 0:D_MODEL] + sg_s[:, D_MODEL:2 * D_MODEL] * p_m
    o_ref[0] = x + gt1 * _dot(z.astype(BF16), w_o[...])


def _const_spec(shape):
    nd = len(shape)
    return pl.BlockSpec(shape, lambda b, j: (0,) * nd, pipeline_mode=pl.Buffered(1))


def _col_window_specs(width, block):
    return [pl.BlockSpec((rows, width), lambda b, j: (0, block), pipeline_mode=pl.Buffered(1))
            for rows in (D_MODEL, 1)]


def _mixer_call(x, mod, g_norm, w_all, b_all, conv_w, head_g, w_pc, w_pm, w_o):
    bsz, seq, d = x.shape
    tm = TM_MIX
    n_xbc, n_qk, n_vo, n_gt = 3 * CONV_DIM, 2 * HEADS * DQK, 2 * MLSTM_DIM, 2 * D_MODEL
    assert w_all.shape == (d, n_xbc + n_qk + n_vo + n_gt + CHUNK)
    windows = (_col_window_specs(n_xbc, 0) + _col_window_specs(n_qk, n_xbc // n_qk)
               + _col_window_specs(n_vo, (n_xbc + n_qk) // n_vo)
               + _col_window_specs(n_gt, (n_xbc + n_qk + n_vo) // n_gt)
               + _col_window_specs(CHUNK, (n_xbc + n_qk + n_vo + n_gt) // CHUNK))
    head = (g_norm,)
    tail = (conv_w, head_g, w_pc, w_pm, w_o)
    return pl.pallas_call(
        _mixer_kernel,
        grid=(bsz, seq // tm),
        in_specs=[
            pl.BlockSpec((1, tm, d), lambda b, j: (b, j, 0)),
            pl.BlockSpec((1, N_MOD, d), lambda b, j: (b, 0, 0)),
        ] + [_const_spec(a.shape) for a in head] + windows + [_const_spec(a.shape) for a in tail],
        out_specs=pl.BlockSpec((1, tm, d), lambda b, j: (b, j, 0)),
        out_shape=jax.ShapeDtypeStruct(x.shape, F32),
        scratch_shapes=[
            pltpu.VMEM((tm, D_MODEL), BF16),
            pltpu.VMEM((tm + SUBLANES, CONV_DIM), F32),
            pltpu.VMEM((tm, CONV_DIM), BF16),
            pltpu.VMEM((tm, 2 * D_MODEL), F32),
            pltpu.VMEM((tm, 2 * HEADS * DQK), F32),
            pltpu.VMEM((tm, 2 * MLSTM_DIM), F32),
            pltpu.VMEM((tm, MLSTM_DIM), F32),
            pltpu.VMEM((HEADS, DQK, DV), F32),
            pltpu.VMEM((SUBLANES, DQK), F32),
            pltpu.VMEM((SUBLANES, CHUNK), F32),
        ],
        compiler_params=pltpu.CompilerParams(
            dimension_semantics=("arbitrary", "arbitrary"),
            vmem_limit_bytes=VMEM_LIMIT_BYTES),
        name="mixer_sublayer",
    )(x, mod, *head, *([w_all, b_all] * 5), *tail)


def _ffn_kernel(x_ref, mod_ref, g_ref, w_a, w_g, cw_ref, w_d, gf_ref, o_ref, abuf, *, final_norm):
    tm = TM_FFN
    j = pl.program_id(1)

    @pl.when(j == 0)
    def _():
        abuf[0:SUBLANES, :] = jnp.zeros((SUBLANES, D_FF), F32)

    x = x_ref[0]
    sh2 = mod_ref[0, 3:4, :]
    sc2 = mod_ref[0, 4:5, :]
    gt2 = mod_ref[0, 5:6, :]
    h = (_rms_scale(x) * g_ref[...]) * (1.0 + sc2) + sh2
    hb = h.astype(BF16)
    abuf[SUBLANES:SUBLANES + tm, :] = _dot(hb, w_a[...])
    gate = _dot(hb, w_g[...])
    a0 = abuf[SUBLANES:SUBLANES + tm, :]
    a1 = abuf[SUBLANES - 1:SUBLANES - 1 + tm, :]
    a2 = abuf[SUBLANES - 2:SUBLANES - 2 + tm, :]
    ac = a2 * cw_ref[0:1, :] + a1 * cw_ref[1:2, :] + a0 * cw_ref[2:3, :]
    abuf[0:SUBLANES, :] = abuf[tm:tm + SUBLANES, :]
    act = (ac * jax.nn.sigmoid(ac)) * gate
    x2 = x + gt2 * _dot(act.astype(BF16), w_d[...])
    o_ref[0] = _rms_scale(x2) * gf_ref[...] if final_norm else x2


def _ffn_call(x, mod, g_norm, w_up, conv_w, w_d, g_final, final_norm):
    bsz, seq, d = x.shape
    tm = TM_FFN
    up_halves = [pl.BlockSpec((d, D_FF), lambda b, j, half=half: (0, half), pipeline_mode=pl.Buffered(1))
                 for half in range(2)]
    tail = (conv_w, w_d, g_final)
    return pl.pallas_call(
        functools.partial(_ffn_kernel, final_norm=final_norm),
        grid=(bsz, seq // tm),
        in_specs=[
            pl.BlockSpec((1, tm, d), lambda b, j: (b, j, 0)),
            pl.BlockSpec((1, N_MOD, d), lambda b, j: (b, 0, 0)),
            _const_spec(g_norm.shape),
        ] + up_halves + [_const_spec(a.shape) for a in tail],
        out_specs=pl.BlockSpec((1, tm, d), lambda b, j: (b, j, 0)),
        out_shape=jax.ShapeDtypeStruct(x.shape, F32),
        scratch_shapes=[pltpu.VMEM((tm + SUBLANES, D_FF), F32)],
        compiler_params=pltpu.CompilerParams(
            dimension_semantics=("arbitrary", "arbitrary"),
            vmem_limit_bytes=VMEM_LIMIT_BYTES),
        name="ffn_sublayer",
    )(x, mod, g_norm, w_up, w_up, *tail)


def kernel(x, c, w_ada, b_ada, g_norm_mix, w_in, b_in, conv_mix_w, mlstm_head_g,
           w_proj_conv, w_proj_mlstm, w_out, g_norm_ffn, w_up, conv_ffn_w, w_down, g_final):
    depth = w_ada.shape[0]
    bsz = x.shape[0]
    o_if = 3 * CONV_DIM + 2 * HEADS * DQK + 2 * MLSTM_DIM
    o_gt = o_if + 2 * HEADS
    lane_pad = CHUNK - 2 * HEADS

    def gates_last(a):
        pad = jnp.zeros(a.shape[:-1] + (lane_pad,), a.dtype)
        return jnp.concatenate([a[..., :o_if], a[..., o_gt:], a[..., o_if:o_gt], pad], axis=-1)

    for l in range(depth):
        c_pad = jnp.pad(c, ((0, SUBLANES - bsz), (0, 0)))
        mod = _ada_call(c_pad, w_ada[l], b_ada[l][None, :])[:bsz]
        mod = mod.reshape(bsz, N_MOD, D_MODEL)

        x = _mixer_call(
            x, mod, g_norm_mix[l][None, :],
            gates_last(w_in[l]).astype(BF16), gates_last(b_in[l][None, :]),
            conv_mix_w[l], mlstm_head_g[l].reshape(1, MLSTM_DIM),
            w_proj_conv[l].astype(BF16), w_proj_mlstm[l].astype(BF16), w_out[l].astype(BF16))

        x = _ffn_call(
            x, mod, g_norm_ffn[l][None, :], w_up[l].astype(BF16),
            conv_ffn_w[l], w_down[l].astype(BF16), g_final[None, :],
            final_norm=(l == depth - 1))
    return x
```

```python
import functools

import jax
import jax.numpy as jnp
from jax import lax
from jax.experimental import pallas as pl
from jax.experimental.pallas import tpu as pltpu

F32 = jnp.float32
BF16 = jnp.bfloat16

D_MODEL = 1024
CONV_DIM = 1024
CONV_K = 3
HEADS = 4
DQK = 128
DV = 256
MLSTM_DIM = HEADS * DV
CHUNK = 128
D_FF = 2816
N_MOD = 6
EPS = 1e-6

SUBLANES = 8
TM_MIX = 512
TM_FFN = 512
MIX_BLOCK = 256
VMEM_LIMIT_BYTES = 56 * 1024 * 1024


def _dot(a, b):
    return jnp.dot(a, b, preferred_element_type=F32)


def _dot_nt(a, b):
    return lax.dot_general(a, b, (((1,), (1,)), ((), ())), preferred_element_type=F32)


def _rms_scale(x):
    return x * lax.rsqrt(jnp.mean(x * x, axis=-1, keepdims=True) + EPS)


def _log_sigmoid(x):
    return -(jnp.maximum(-x, 0.0) + jnp.log1p(jnp.exp(-jnp.abs(x))))


def _ada_kernel(c_ref, w_ref, b_ref, o_ref):
    c = c_ref[...]
    act = c * jax.nn.sigmoid(c)
    o_ref[...] = _dot(act.astype(BF16), w_ref[...].astype(BF16)) + b_ref[...]


def _ada_call(c_pad, w_ada, b_ada):
    n = w_ada.shape[1]
    tn = 1536
    return pl.pallas_call(
        _ada_kernel,
        grid=(n // tn,),
        in_specs=[
            pl.BlockSpec((SUBLANES, D_MODEL), lambda i: (0, 0)),
            pl.BlockSpec((D_MODEL, tn), lambda i: (0, i)),
            pl.BlockSpec((1, tn), lambda i: (0, i)),
        ],
        out_specs=pl.BlockSpec((SUBLANES, tn), lambda i: (0, i)),
        out_shape=jax.ShapeDtypeStruct((SUBLANES, n), F32),
        compiler_params=pltpu.CompilerParams(dimension_semantics=("arbitrary",)),
        name="adaln_mod",
    )(c_pad, w_ada, b_ada)


def _mixer_kernel(x_ref, mod_ref, g_ref,
                  w_xbc, b_xbc, w_qk, b_qk, w_vo, b_vo, w_gt, b_gt, w_if, b_if,
                  cw_ref, hg_ref, w_pc, w_pm, w_o,
                  o_ref,
                  hb_s, ubuf, y_s, sg_s, qk_s, vo_s, hm_s, c_s, n_s, m_s):
    tm = TM_MIX
    j = pl.program_id(1)

    @pl.when(j == 0)
    def _():
        ubuf[0:SUBLANES, :] = jnp.zeros((SUBLANES, CONV_DIM), F32)
        c_s[...] = jnp.zeros_like(c_s)
        n_s[...] = jnp.zeros_like(n_s)
        m_s[...] = jnp.zeros_like(m_s)

    x = x_ref[0]
    sh1 = mod_ref[0, 0:1, :]
    sc1 = mod_ref[0, 1:2, :]
    gt1 = mod_ref[0, 2:3, :]
    h = (_rms_scale(x) * g_ref[...]) * (1.0 + sc1) + sh1
    hb_s[...] = h.astype(BF16)

    gif = _dot_nt(hb_s[...], w_if[...]) + b_if[...]

    row = lax.broadcasted_iota(jnp.int32, (CHUNK, CHUNK), 0)
    col = lax.broadcasted_iota(jnp.int32, (CHUNK, CHUNK), 1)
    tri = col <= row
    eye = col == row
    lane8 = lax.broadcasted_iota(jnp.int32, (SUBLANES, CHUNK), 1)

    def to_col(r):
        return jnp.sum(jnp.where(eye, r, 0.0), axis=1, keepdims=True)

    def chunk_decay(ci):
        li8 = gif[ci * CHUNK:(ci + 1) * CHUNK, :].T[0:SUBLANES, :]
        b8 = _log_sigmoid(li8)
        d = 1
        while d < CHUNK:
            b8 = b8 + jnp.where(lane8 >= d, pltpu.roll(b8, d, axis=1), 0.0)
            d *= 2
        return li8, b8

    decays = [chunk_decay(ci) for ci in range(tm // CHUNK)]
    qk_s[...] = _dot_nt(hb_s[...], w_qk[...]) + b_qk[...]
    vo_s[...] = _dot_nt(hb_s[...], w_vo[...]) + b_vo[...]

    def mlstm_pair(ci, hd, li8, b8):
        r0 = ci * CHUNK
        li_row = li8[hd:hd + 1, :]
        b_row = b8[HEADS + hd:HEADS + hd + 1, :]
        g = b_row[:, CHUNK - 1:CHUNK]
        b_col = to_col(b_row)
        li_col = to_col(li_row)
        m_prev = m_s[hd:hd + 1, 0:1]
        n_prev = n_s[hd:hd + 1, :]
        c_prev = c_s[hd]

        q = qk_s[r0:r0 + CHUNK, hd * DQK:(hd + 1) * DQK] * (DQK ** -0.5)
        k = qk_s[r0:r0 + CHUNK, HEADS * DQK + hd * DQK:HEADS * DQK + (hd + 1) * DQK]
        v = vo_s[r0:r0 + CHUNK, hd * DV:(hd + 1) * DV]
        qb = q.astype(BF16)
        kb = k.astype(BF16)
        vb = v.astype(BF16)

        dlog = jnp.where(tri, b_col - b_row + li_row, -jnp.inf)
        inter = b_col + m_prev
        m_t = jnp.maximum(inter, jnp.max(dlog, axis=1, keepdims=True))
        wts = jnp.exp(dlog - m_t) * _dot_nt(qb, kb)
        s_inter = jnp.exp(inter - m_t)
        num = _dot(wts.astype(BF16), vb) + s_inter * _dot(qb, c_prev.astype(BF16))
        den = (jnp.sum(wts, axis=1, keepdims=True)
               + s_inter * jnp.sum(q * n_prev, axis=1, keepdims=True))
        hm_s[r0:r0 + CHUNK, hd * DV:(hd + 1) * DV] = num / jnp.maximum(jnp.abs(den), jnp.exp(-m_t))

        a_row = g - b_row + li_row
        a_col = g - b_col + li_col
        m_loc = jnp.max(a_row, axis=1, keepdims=True)
        kw = jnp.exp(a_col - m_loc) * k
        c_loc = _dot(kw.T.astype(BF16), vb)
        n_loc = jnp.sum(kw, axis=0, keepdims=True)
        m_new = jnp.maximum(g + m_prev, m_loc)
        s_old = jnp.exp(g + m_prev - m_new)
        s_new = jnp.exp(m_loc - m_new)
        c_s[hd] = s_old * c_prev + s_new * c_loc
        n_s[hd:hd + 1, :] = s_old * n_prev + s_new * n_loc
        m_s[hd:hd + 1, :] = jnp.broadcast_to(m_new, (1, CHUNK))

    def conv_block(jb):
        lo = jb * MIX_BLOCK
        cs = slice(lo, lo + MIX_BLOCK)

        def proj(off):
            return (_dot_nt(hb_s[...], w_xbc[off + lo:off + lo + MIX_BLOCK, :])
                    + b_xbc[:, off + lo:off + lo + MIX_BLOCK])

        xin = proj(0)
        bg = proj(CONV_DIM)
        cg = proj(2 * CONV_DIM)
        ubuf[SUBLANES:SUBLANES + tm, cs] = cg * xin
        u0 = ubuf[SUBLANES:SUBLANES + tm, cs]
        u1 = ubuf[SUBLANES - 1:SUBLANES - 1 + tm, cs]
        u2 = ubuf[SUBLANES - 2:SUBLANES - 2 + tm, cs]
        y = bg * (u2 * cw_ref[0:1, cs] + u1 * cw_ref[1:2, cs] + u0 * cw_ref[2:3, cs])
        y_s[:, cs] = y.astype(BF16)
        ubuf[0:SUBLANES, cs] = ubuf[tm:tm + SUBLANES, cs]

    def gate_block(jb):
        cs = slice(jb * MIX_BLOCK, (jb + 1) * MIX_BLOCK)
        sg_s[:, cs] = jax.nn.sigmoid(_dot_nt(hb_s[...], w_gt[cs, :]) + b_gt[:, cs])

    def pconv_block(jb):
        cs = slice(jb * MIX_BLOCK, (jb + 1) * MIX_BLOCK)
        sg_s[:, cs] = sg_s[:, cs] * _dot(y_s[...], w_pc[:, cs])

    nb = CONV_DIM // MIX_BLOCK
    mxu_items = ([functools.partial(conv_block, jb) for jb in range(nb)]
                 + [functools.partial(gate_block, jb) for jb in range(2 * nb)]
                 + [functools.partial(pconv_block, jb) for jb in range(nb)])

    pairs = [(ci, hd) for ci in range(tm // CHUNK) for hd in range(HEADS)]
    for idx, (ci, hd) in enumerate(pairs):
        mlstm_pair(ci, hd, *decays[ci])
        lo = idx * len(mxu_items) // len(pairs)
        hi = (idx + 1) * len(mxu_items) // len(pairs)
        for item in mxu_items[lo:hi]:
            item()

    o_gate = jax.nn.sigmoid(vo_s[:, MLSTM_DIM:2 * MLSTM_DIM])
    parts = []
    for hd in range(HEADS):
        hm_h = hm_s[:, hd * DV:(hd + 1) * DV]
        parts.append(_rms_scale(hm_h) * hg_ref[:, hd * DV:(hd + 1) * DV])
    hm = o_gate * jnp.concatenate(parts, axis=1)
    p_m = _dot(hm.astype(BF16), w_pm[...])
    z = sg_s[:, 0:D_MODEL] + sg_s[:, D_MODEL:2 * D_MODEL] * p_m
    o_ref[0] = x + gt1 * _dot(z.astype(BF16), w_o[...])


def _const_spec(shape):
    nd = len(shape)
    return pl.BlockSpec(shape, lambda b, j: (0,) * nd, pipeline_mode=pl.Buffered(1))


def _col_window_specs(width, block):
    return [pl.BlockSpec((width, D_MODEL), lambda b, j: (block, 0), pipeline_mode=pl.Buffered(1)),
            pl.BlockSpec((1, width), lambda b, j: (0, block), pipeline_mode=pl.Buffered(1))]


def _mixer_call(x, mod, g_norm, w_all, b_all, conv_w, head_g, w_pc, w_pm, w_o):
    bsz, seq, d = x.shape
    tm = TM_MIX
    n_xbc, n_qk, n_vo, n_gt = 3 * CONV_DIM, 2 * HEADS * DQK, 2 * MLSTM_DIM, 2 * D_MODEL
    assert w_all.shape == (n_xbc + n_qk + n_vo + n_gt + CHUNK, d)
    windows = (_col_window_specs(n_xbc, 0) + _col_window_specs(n_qk, n_xbc // n_qk)
               + _col_window_specs(n_vo, (n_xbc + n_qk) // n_vo)
               + _col_window_specs(n_gt, (n_xbc + n_qk + n_vo) // n_gt)
               + _col_window_specs(CHUNK, (n_xbc + n_qk + n_vo + n_gt) // CHUNK))
    head = (g_norm,)
    tail = (conv_w, head_g, w_pc, w_pm, w_o)
    return pl.pallas_call(
        _mixer_kernel,
        grid=(bsz, seq // tm),
        in_specs=[
            pl.BlockSpec((1, tm, d), lambda b, j: (b, j, 0)),
            pl.BlockSpec((1, N_MOD, d), lambda b, j: (b, 0, 0)),
        ] + [_const_spec(a.shape) for a in head] + windows + [_const_spec(a.shape) for a in tail],
        out_specs=pl.BlockSpec((1, tm, d), lambda b, j: (b, j, 0)),
        out_shape=jax.ShapeDtypeStruct(x.shape, F32),
        scratch_shapes=[
            pltpu.VMEM((tm, D_MODEL), BF16),
            pltpu.VMEM((tm + SUBLANES, CONV_DIM), F32),
            pltpu.VMEM((tm, CONV_DIM), BF16),
            pltpu.VMEM((tm, 2 * D_MODEL), F32),
            pltpu.VMEM((tm, 2 * HEADS * DQK), F32),
            pltpu.VMEM((tm, 2 * MLSTM_DIM), F32),
            pltpu.VMEM((tm, MLSTM_DIM), F32),
            pltpu.VMEM((HEADS, DQK, DV), F32),
            pltpu.VMEM((SUBLANES, DQK), F32),
            pltpu.VMEM((SUBLANES, CHUNK), F32),
        ],
        compiler_params=pltpu.CompilerParams(
            dimension_semantics=("arbitrary", "arbitrary"),
            vmem_limit_bytes=VMEM_LIMIT_BYTES),
        name="mixer_sublayer",
    )(x, mod, *head, *([w_all, b_all] * 5), *tail)


def _ffn_kernel(x_ref, mod_ref, g_ref, w_a, w_g, cw_ref, w_d, gf_ref, o_ref, abuf, *, final_norm):
    tm = TM_FFN
    j = pl.program_id(1)

    @pl.when(j == 0)
    def _():
        abuf[0:SUBLANES, :] = jnp.zeros((SUBLANES, D_FF), F32)

    x = x_ref[0]
    sh2 = mod_ref[0, 3:4, :]
    sc2 = mod_ref[0, 4:5, :]
    gt2 = mod_ref[0, 5:6, :]
    h = (_rms_scale(x) * g_ref[...]) * (1.0 + sc2) + sh2
    hb = h.astype(BF16)
    abuf[SUBLANES:SUBLANES + tm, :] = _dot(hb, w_a[...])
    gate = _dot(hb, w_g[...])
    a0 = abuf[SUBLANES:SUBLANES + tm, :]
    a1 = abuf[SUBLANES - 1:SUBLANES - 1 + tm, :]
    a2 = abuf[SUBLANES - 2:SUBLANES - 2 + tm, :]
    ac = a2 * cw_ref[0:1, :] + a1 * cw_ref[1:2, :] + a0 * cw_ref[2:3, :]
    abuf[0:SUBLANES, :] = abuf[tm:tm + SUBLANES, :]
    act = (ac * jax.nn.sigmoid(ac)) * gate
    x2 = x + gt2 * _dot(act.astype(BF16), w_d[...])
    o_ref[0] = _rms_scale(x2) * gf_ref[...] if final_norm else x2


def _ffn_call(x, mod, g_norm, w_up, conv_w, w_d, g_final, final_norm):
    bsz, seq, d = x.shape
    tm = TM_FFN
    up_halves = [pl.BlockSpec((d, D_FF), lambda b, j, half=half: (0, half), pipeline_mode=pl.Buffered(1))
                 for half in range(2)]
    tail = (conv_w, w_d, g_final)
    return pl.pallas_call(
        functools.partial(_ffn_kernel, final_norm=final_norm),
        grid=(bsz, seq // tm),
        in_specs=[
            pl.BlockSpec((1, tm, d), lambda b, j: (b, j, 0)),
            pl.BlockSpec((1, N_MOD, d), lambda b, j: (b, 0, 0)),
            _const_spec(g_norm.shape),
        ] + up_halves + [_const_spec(a.shape) for a in tail],
        out_specs=pl.BlockSpec((1, tm, d), lambda b, j: (b, j, 0)),
        out_shape=jax.ShapeDtypeStruct(x.shape, F32),
        scratch_shapes=[pltpu.VMEM((tm + SUBLANES, D_FF), F32)],
        compiler_params=pltpu.CompilerParams(
            dimension_semantics=("arbitrary", "arbitrary"),
            vmem_limit_bytes=VMEM_LIMIT_BYTES),
        name="ffn_sublayer",
    )(x, mod, g_norm, w_up, w_up, *tail)


def kernel(x, c, w_ada, b_ada, g_norm_mix, w_in, b_in, conv_mix_w, mlstm_head_g,
           w_proj_conv, w_proj_mlstm, w_out, g_norm_ffn, w_up, conv_ffn_w, w_down, g_final):
    depth = w_ada.shape[0]
    bsz = x.shape[0]
    o_if = 3 * CONV_DIM + 2 * HEADS * DQK + 2 * MLSTM_DIM
    o_gt = o_if + 2 * HEADS
    lane_pad = CHUNK - 2 * HEADS

    def gates_last(a):
        pad = jnp.zeros((lane_pad,) + a.shape[1:], a.dtype)
        return jnp.concatenate([a[:o_if], a[o_gt:], a[o_if:o_gt], pad], axis=0)

    for l in range(depth):
        c_pad = jnp.pad(c, ((0, SUBLANES - bsz), (0, 0)))
        mod = _ada_call(c_pad, w_ada[l], b_ada[l][None, :])[:bsz]
        mod = mod.reshape(bsz, N_MOD, D_MODEL)

        x = _mixer_call(
            x, mod, g_norm_mix[l][None, :],
            gates_last(w_in[l].T).astype(BF16), gates_last(b_in[l])[None, :],
            conv_mix_w[l], mlstm_head_g[l].reshape(1, MLSTM_DIM),
            w_proj_conv[l].astype(BF16), w_proj_mlstm[l].astype(BF16), w_out[l].astype(BF16))

        x = _ffn_call(
            x, mod, g_norm_ffn[l][None, :], w_up[l].astype(BF16),
            conv_ffn_w[l], w_down[l].astype(BF16), g_final[None, :],
            final_norm=(l == depth - 1))
    return x
```

```python
import functools

import jax
import jax.numpy as jnp
from jax import lax
from jax.experimental import pallas as pl
from jax.experimental.pallas import tpu as pltpu

F32 = jnp.float32
BF16 = jnp.bfloat16

D_MODEL = 1024
CONV_DIM = 1024
CONV_K = 3
HEADS = 4
DQK = 128
DV = 256
MLSTM_DIM = HEADS * DV
CHUNK = 128
D_FF = 2816
N_MOD = 6
EPS = 1e-6

SUBLANES = 8
TM_MIX = 512
TM_FFN = 512
MIX_BLOCK = 256
VMEM_LIMIT_BYTES = 56 * 1024 * 1024


def _dot(a, b):
    return jnp.dot(a, b, preferred_element_type=F32)


def _dot_nt(a, b):
    return lax.dot_general(a, b, (((1,), (1,)), ((), ())), preferred_element_type=F32)


def _rms_scale(x):
    return x * lax.rsqrt(jnp.mean(x * x, axis=-1, keepdims=True) + EPS)


def _log_sigmoid(x):
    return -(jnp.maximum(-x, 0.0) + jnp.log1p(jnp.exp(-jnp.abs(x))))


def _ada_kernel(c_ref, w_ref, b_ref, o_ref):
    c = c_ref[...]
    act = c * jax.nn.sigmoid(c)
    o_ref[...] = _dot(act.astype(BF16), w_ref[...].astype(BF16)) + b_ref[...]


def _ada_call(c_pad, w_ada, b_ada):
    n = w_ada.shape[1]
    tn = 1536
    return pl.pallas_call(
        _ada_kernel,
        grid=(n // tn,),
        in_specs=[
            pl.BlockSpec((SUBLANES, D_MODEL), lambda i: (0, 0)),
            pl.BlockSpec((D_MODEL, tn), lambda i: (0, i)),
            pl.BlockSpec((1, tn), lambda i: (0, i)),
        ],
        out_specs=pl.BlockSpec((SUBLANES, tn), lambda i: (0, i)),
        out_shape=jax.ShapeDtypeStruct((SUBLANES, n), F32),
        compiler_params=pltpu.CompilerParams(dimension_semantics=("arbitrary",)),
        name="adaln_mod",
    )(c_pad, w_ada, b_ada)


def _mixer_kernel(x_ref, mod_ref, g_ref,
                  w_xbc, b_xbc, w_qk, b_qk, w_vo, b_vo, w_gt, b_gt, w_if, b_if,
                  cw_ref, hg_ref, w_pc, w_pm, w_o,
                  o_ref,
                  hb_s, ubuf, y_s, sg_s, qk_s, vo_s, hm_s, c_s, n_s, m_s):
    tm = TM_MIX
    j = pl.program_id(1)

    @pl.when(j == 0)
    def _():
        ubuf[0:SUBLANES, :] = jnp.zeros((SUBLANES, CONV_DIM), F32)
        c_s[...] = jnp.zeros_like(c_s)
        n_s[...] = jnp.zeros_like(n_s)
        m_s[...] = jnp.zeros_like(m_s)

    x = x_ref[0]
    sh1 = mod_ref[0, 0:1, :]
    sc1 = mod_ref[0, 1:2, :]
    gt1 = mod_ref[0, 2:3, :]
    h = (_rms_scale(x) * g_ref[...]) * (1.0 + sc1) + sh1
    hb_s[...] = h.astype(BF16)

    ift = _dot_nt(w_if[...], hb_s[...]) + b_if[...]

    row = lax.broadcasted_iota(jnp.int32, (CHUNK, CHUNK), 0)
    col = lax.broadcasted_iota(jnp.int32, (CHUNK, CHUNK), 1)
    tri = col <= row
    eye = col == row
    lane8 = lax.broadcasted_iota(jnp.int32, (SUBLANES, CHUNK), 1)

    def to_col(r):
        return jnp.sum(jnp.where(eye, r, 0.0), axis=1, keepdims=True)

    def chunk_decay(ci):
        li8 = ift[:, ci * CHUNK:(ci + 1) * CHUNK]
        b8 = _log_sigmoid(li8)
        d = 1
        while d < CHUNK:
            b8 = b8 + jnp.where(lane8 >= d, pltpu.roll(b8, d, axis=1), 0.0)
            d *= 2
        return li8, b8

    decays = [chunk_decay(ci) for ci in range(tm // CHUNK)]
    qk_s[...] = _dot_nt(hb_s[...], w_qk[...]) + b_qk[...]
    vo_s[...] = _dot_nt(hb_s[...], w_vo[...]) + b_vo[...]

    def mlstm_pair(ci, hd, li8, b8):
        r0 = ci * CHUNK
        li_row = li8[hd:hd + 1, :]
        b_row = b8[HEADS + hd:HEADS + hd + 1, :]
        g = b_row[:, CHUNK - 1:CHUNK]
        b_col = to_col(b_row)
        li_col = to_col(li_row)
        m_prev = m_s[hd:hd + 1, 0:1]
        n_prev = n_s[hd:hd + 1, :]
        c_prev = c_s[hd]

        q = qk_s[r0:r0 + CHUNK, hd * DQK:(hd + 1) * DQK] * (DQK ** -0.5)
        k = qk_s[r0:r0 + CHUNK, HEADS * DQK + hd * DQK:HEADS * DQK + (hd + 1) * DQK]
        v = vo_s[r0:r0 + CHUNK, hd * DV:(hd + 1) * DV]
        qb = q.astype(BF16)
        kb = k.astype(BF16)
        vb = v.astype(BF16)

        dlog = jnp.where(tri, b_col - b_row + li_row, -jnp.inf)
        inter = b_col + m_prev
        m_t = jnp.maximum(inter, jnp.max(dlog, axis=1, keepdims=True))
        wts = jnp.exp(dlog - m_t) * _dot_nt(qb, kb)
        s_inter = jnp.exp(inter - m_t)
        num = _dot(wts.astype(BF16), vb) + s_inter * _dot(qb, c_prev.astype(BF16))
        den = (jnp.sum(wts, axis=1, keepdims=True)
               + s_inter * jnp.sum(q * n_prev, axis=1, keepdims=True))
        hm_s[r0:r0 + CHUNK, hd * DV:(hd + 1) * DV] = num / jnp.maximum(jnp.abs(den), jnp.exp(-m_t))

        a_row = g - b_row + li_row
        a_col = g - b_col + li_col
        m_loc = jnp.max(a_row, axis=1, keepdims=True)
        kw = jnp.exp(a_col - m_loc) * k
        c_loc = _dot(kw.T.astype(BF16), vb)
        n_loc = jnp.sum(kw, axis=0, keepdims=True)
        m_new = jnp.maximum(g + m_prev, m_loc)
        s_old = jnp.exp(g + m_prev - m_new)
        s_new = jnp.exp(m_loc - m_new)
        c_s[hd] = s_old * c_prev + s_new * c_loc
        n_s[hd:hd + 1, :] = s_old * n_prev + s_new * n_loc
        m_s[hd:hd + 1, :] = jnp.broadcast_to(m_new, (1, CHUNK))

    def conv_block(jb):
        lo = jb * MIX_BLOCK
        cs = slice(lo, lo + MIX_BLOCK)

        def proj(off):
            return (_dot_nt(hb_s[...], w_xbc[off + lo:off + lo + MIX_BLOCK, :])
                    + b_xbc[:, off + lo:off + lo + MIX_BLOCK])

        xin = proj(0)
        bg = proj(CONV_DIM)
        cg = proj(2 * CONV_DIM)
        ubuf[SUBLANES:SUBLANES + tm, cs] = cg * xin
        u0 = ubuf[SUBLANES:SUBLANES + tm, cs]
        u1 = ubuf[SUBLANES - 1:SUBLANES - 1 + tm, cs]
        u2 = ubuf[SUBLANES - 2:SUBLANES - 2 + tm, cs]
        y = bg * (u2 * cw_ref[0:1, cs] + u1 * cw_ref[1:2, cs] + u0 * cw_ref[2:3, cs])
        y_s[:, cs] = y.astype(BF16)
        ubuf[0:SUBLANES, cs] = ubuf[tm:tm + SUBLANES, cs]

    def gate_block(jb):
        cs = slice(jb * MIX_BLOCK, (jb + 1) * MIX_BLOCK)
        sg_s[:, cs] = jax.nn.sigmoid(_dot_nt(hb_s[...], w_gt[cs, :]) + b_gt[:, cs])

    def pconv_block(jb):
        cs = slice(jb * MIX_BLOCK, (jb + 1) * MIX_BLOCK)
        sg_s[:, cs] = sg_s[:, cs] * _dot(y_s[...], w_pc[:, cs])

    nb = CONV_DIM // MIX_BLOCK
    mxu_items = ([functools.partial(conv_block, jb) for jb in range(nb)]
                 + [functools.partial(gate_block, jb) for jb in range(2 * nb)]
                 + [functools.partial(pconv_block, jb) for jb in range(nb)])

    pairs = [(ci, hd) for ci in range(tm // CHUNK) for hd in range(HEADS)]
    for idx, (ci, hd) in enumerate(pairs):
        mlstm_pair(ci, hd, *decays[ci])
        lo = idx * len(mxu_items) // len(pairs)
        hi = (idx + 1) * len(mxu_items) // len(pairs)
        for item in mxu_items[lo:hi]:
            item()

    o_gate = jax.nn.sigmoid(vo_s[:, MLSTM_DIM:2 * MLSTM_DIM])
    parts = []
    for hd in range(HEADS):
        hm_h = hm_s[:, hd * DV:(hd + 1) * DV]
        parts.append(_rms_scale(hm_h) * hg_ref[:, hd * DV:(hd + 1) * DV])
    hm = o_gate * jnp.concatenate(parts, axis=1)
    p_m = _dot(hm.astype(BF16), w_pm[...])
    z = sg_s[:, 0:D_MODEL] + sg_s[:, D_MODEL:2 * D_MODEL] * p_m
    o_ref[0] = x + gt1 * _dot(z.astype(BF16), w_o[...])


def _const_spec(shape):
    nd = len(shape)
    return pl.BlockSpec(shape, lambda b, j: (0,) * nd, pipeline_mode=pl.Buffered(1))


def _row_window_spec(rows, block):
    return pl.BlockSpec((rows, D_MODEL), lambda b, j: (block, 0), pipeline_mode=pl.Buffered(1))


def _lane_window_spec(width, block):
    return pl.BlockSpec((1, width), lambda b, j: (0, block), pipeline_mode=pl.Buffered(1))


def _mixer_call(x, mod, g_norm, w_t, w_gt, w_if, b_all, b_if, conv_w, head_g, w_pc, w_pm, w_o):
    bsz, seq, d = x.shape
    tm = TM_MIX
    n_xbc, n_qk, n_vo, n_gt = 3 * CONV_DIM, 2 * HEADS * DQK, 2 * MLSTM_DIM, 2 * D_MODEL
    windows = [_row_window_spec(n_xbc, 0), _lane_window_spec(n_xbc, 0),
               _row_window_spec(n_qk, n_xbc // n_qk), _lane_window_spec(n_qk, n_xbc // n_qk),
               _row_window_spec(n_vo, (n_xbc + n_qk) // n_vo), _lane_window_spec(n_vo, (n_xbc + n_qk) // n_vo),
               _const_spec(w_gt.shape), _lane_window_spec(n_gt, (n_xbc + n_qk + n_vo) // n_gt),
               _const_spec(w_if.shape), _const_spec(b_if.shape)]
    window_args = (w_t, b_all, w_t, b_all, w_t, b_all, w_gt, b_all, w_if, b_if)
    head = (g_norm,)
    tail = (conv_w, head_g, w_pc, w_pm, w_o)
    return pl.pallas_call(
        _mixer_kernel,
        grid=(bsz, seq // tm),
        in_specs=[
            pl.BlockSpec((1, tm, d), lambda b, j: (b, j, 0)),
            pl.BlockSpec((1, N_MOD, d), lambda b, j: (b, 0, 0)),
        ] + [_const_spec(a.shape) for a in head] + windows + [_const_spec(a.shape) for a in tail],
        out_specs=pl.BlockSpec((1, tm, d), lambda b, j: (b, j, 0)),
        out_shape=jax.ShapeDtypeStruct(x.shape, F32),
        scratch_shapes=[
            pltpu.VMEM((tm, D_MODEL), BF16),
            pltpu.VMEM((tm + SUBLANES, CONV_DIM), F32),
            pltpu.VMEM((tm, CONV_DIM), BF16),
            pltpu.VMEM((tm, 2 * D_MODEL), F32),
            pltpu.VMEM((tm, 2 * HEADS * DQK), F32),
            pltpu.VMEM((tm, 2 * MLSTM_DIM), F32),
            pltpu.VMEM((tm, MLSTM_DIM), F32),
            pltpu.VMEM((HEADS, DQK, DV), F32),
            pltpu.VMEM((SUBLANES, DQK), F32),
            pltpu.VMEM((SUBLANES, CHUNK), F32),
        ],
        compiler_params=pltpu.CompilerParams(
            dimension_semantics=("arbitrary", "arbitrary"),
            vmem_limit_bytes=VMEM_LIMIT_BYTES),
        name="mixer_sublayer",
    )(x, mod, *head, *window_args, *tail)


def _ffn_kernel(x_ref, mod_ref, g_ref, w_a, w_g, cw_ref, w_d, gf_ref, o_ref, abuf, *, final_norm):
    tm = TM_FFN
    j = pl.program_id(1)

    @pl.when(j == 0)
    def _():
        abuf[0:SUBLANES, :] = jnp.zeros((SUBLANES, D_FF), F32)

    x = x_ref[0]
    sh2 = mod_ref[0, 3:4, :]
    sc2 = mod_ref[0, 4:5, :]
    gt2 = mod_ref[0, 5:6, :]
    h = (_rms_scale(x) * g_ref[...]) * (1.0 + sc2) + sh2
    hb = h.astype(BF16)
    abuf[SUBLANES:SUBLANES + tm, :] = _dot(hb, w_a[...])
    gate = _dot(hb, w_g[...])
    a0 = abuf[SUBLANES:SUBLANES + tm, :]
    a1 = abuf[SUBLANES - 1:SUBLANES - 1 + tm, :]
    a2 = abuf[SUBLANES - 2:SUBLANES - 2 + tm, :]
    ac = a2 * cw_ref[0:1, :] + a1 * cw_ref[1:2, :] + a0 * cw_ref[2:3, :]
    abuf[0:SUBLANES, :] = abuf[tm:tm + SUBLANES, :]
    act = (ac * jax.nn.sigmoid(ac)) * gate
    x2 = x + gt2 * _dot(act.astype(BF16), w_d[...])
    o_ref[0] = _rms_scale(x2) * gf_ref[...] if final_norm else x2


def _ffn_call(x, mod, g_norm, w_up, conv_w, w_d, g_final, final_norm):
    bsz, seq, d = x.shape
    tm = TM_FFN
    up_halves = [pl.BlockSpec((d, D_FF), lambda b, j, half=half: (0, half), pipeline_mode=pl.Buffered(1))
                 for half in range(2)]
    tail = (conv_w, w_d, g_final)
    return pl.pallas_call(
        functools.partial(_ffn_kernel, final_norm=final_norm),
        grid=(bsz, seq // tm),
        in_specs=[
            pl.BlockSpec((1, tm, d), lambda b, j: (b, j, 0)),
            pl.BlockSpec((1, N_MOD, d), lambda b, j: (b, 0, 0)),
            _const_spec(g_norm.shape),
        ] + up_halves + [_const_spec(a.shape) for a in tail],
        out_specs=pl.BlockSpec((1, tm, d), lambda b, j: (b, j, 0)),
        out_shape=jax.ShapeDtypeStruct(x.shape, F32),
        scratch_shapes=[pltpu.VMEM((tm + SUBLANES, D_FF), F32)],
        compiler_params=pltpu.CompilerParams(
            dimension_semantics=("arbitrary", "arbitrary"),
            vmem_limit_bytes=VMEM_LIMIT_BYTES),
        name="ffn_sublayer",
    )(x, mod, g_norm, w_up, w_up, *tail)


def kernel(x, c, w_ada, b_ada, g_norm_mix, w_in, b_in, conv_mix_w, mlstm_head_g,
           w_proj_conv, w_proj_mlstm, w_out, g_norm_ffn, w_up, conv_ffn_w, w_down, g_final):
    depth = w_ada.shape[0]
    bsz = x.shape[0]
    o_if = 3 * CONV_DIM + 2 * HEADS * DQK + 2 * MLSTM_DIM
    o_gt = o_if + 2 * HEADS

    for l in range(depth):
        c_pad = jnp.pad(c, ((0, SUBLANES - bsz), (0, 0)))
        mod = _ada_call(c_pad, w_ada[l], b_ada[l][None, :])[:bsz]
        mod = mod.reshape(bsz, N_MOD, D_MODEL)

        w_t = w_in[l].T.astype(BF16)
        bi = b_in[l]
        b_all = jnp.concatenate([bi[:o_if], bi[o_gt:]])[None, :]
        x = _mixer_call(
            x, mod, g_norm_mix[l][None, :],
            w_t, w_t[o_gt:], w_t[o_if:o_gt], b_all, bi[o_if:o_gt, None],
            conv_mix_w[l], mlstm_head_g[l].reshape(1, MLSTM_DIM),
            w_proj_conv[l].astype(BF16), w_proj_mlstm[l].astype(BF16), w_out[l].astype(BF16))

        x = _ffn_call(
            x, mod, g_norm_ffn[l][None, :], w_up[l].astype(BF16),
            conv_ffn_w[l], w_down[l].astype(BF16), g_final[None, :],
            final_norm=(l == depth - 1))
    return x
```

```python
import functools

import jax
import jax.numpy as jnp
from jax import lax
from jax.experimental import pallas as pl
from jax.experimental.pallas import tpu as pltpu

F32 = jnp.float32
BF16 = jnp.bfloat16

D_MODEL = 1024
CONV_DIM = 1024
CONV_K = 3
HEADS = 4
DQK = 128
DV = 256
MLSTM_DIM = HEADS * DV
CHUNK = 128
D_FF = 2816
N_MOD = 6
EPS = 1e-6

SUBLANES = 8
TM_MIX = 512
TM_FFN = 512
MIX_BLOCK = 256
VMEM_LIMIT_BYTES = 56 * 1024 * 1024


def _dot(a, b):
    return jnp.dot(a, b, preferred_element_type=F32)


def _dot_nt(a, b):
    return lax.dot_general(a, b, (((1,), (1,)), ((), ())), preferred_element_type=F32)


def _rms_scale(x):
    return x * lax.rsqrt(jnp.mean(x * x, axis=-1, keepdims=True) + EPS)


def _log_sigmoid(x):
    return -(jnp.maximum(-x, 0.0) + jnp.log1p(jnp.exp(-jnp.abs(x))))


def _ada_kernel(c_ref, w_ref, b_ref, o_ref):
    c = c_ref[...]
    act = c * jax.nn.sigmoid(c)
    o_ref[...] = _dot(act.astype(BF16), w_ref[...].astype(BF16)) + b_ref[...]


def _ada_call(c_pad, w_ada, b_ada):
    n = w_ada.shape[1]
    tn = 1536
    return pl.pallas_call(
        _ada_kernel,
        grid=(n // tn,),
        in_specs=[
            pl.BlockSpec((SUBLANES, D_MODEL), lambda i: (0, 0)),
            pl.BlockSpec((D_MODEL, tn), lambda i: (0, i)),
            pl.BlockSpec((1, tn), lambda i: (0, i)),
        ],
        out_specs=pl.BlockSpec((SUBLANES, tn), lambda i: (0, i)),
        out_shape=jax.ShapeDtypeStruct((SUBLANES, n), F32),
        compiler_params=pltpu.CompilerParams(dimension_semantics=("arbitrary",)),
        name="adaln_mod",
    )(c_pad, w_ada, b_ada)


def _mixer_kernel(x_ref, mod_ref, g_ref,
                  w_xbc, b_xbc, w_qk, b_qk, w_vo, b_vo, w_gt, b_gt, w_if, b_if,
                  cw_ref, hg_ref, w_pc, w_pm, w_o,
                  o_ref,
                  hb_s, ubuf, y_s, sg_s, qs_s, k_s, qb_s, kb_s, vb_s, og_s, hm_s, c_s, n_s, m_s):
    tm = TM_MIX
    n_chunks = tm // CHUNK
    j = pl.program_id(1)

    @pl.when(j == 0)
    def _():
        ubuf[0:SUBLANES, :] = jnp.zeros((SUBLANES, CONV_DIM), F32)
        c_s[...] = jnp.zeros_like(c_s)
        n_s[...] = jnp.zeros_like(n_s)
        m_s[...] = jnp.zeros_like(m_s)

    x = x_ref[0]
    sh1 = mod_ref[0, 0:1, :]
    sc1 = mod_ref[0, 1:2, :]
    gt1 = mod_ref[0, 2:3, :]
    h = (_rms_scale(x) * g_ref[...]) * (1.0 + sc1) + sh1
    hb_s[...] = h.astype(BF16)

    gif = _dot_nt(hb_s[...], w_if[...]) + b_if[...]
    lane8 = lax.broadcasted_iota(jnp.int32, (SUBLANES, CHUNK), 1)
    head_rows = lax.broadcasted_iota(jnp.int32, (SUBLANES, CHUNK), 0) < HEADS

    def lane_scan(v, op, fill):
        d = 1
        while d < CHUNK:
            v = op(v, jnp.where(lane8 >= d, pltpu.roll(v, d, axis=1), fill))
            d *= 2
        return v

    m_run = m_s[...]
    gate_rows = []
    for ci in range(n_chunks):
        gi = gif[ci * CHUNK:(ci + 1) * CHUNK, :].T[0:SUBLANES, :]
        li = jnp.where(head_rows, gi, 0.0)
        lf = jnp.where(head_rows, _log_sigmoid(pltpu.roll(gi, HEADS, axis=0)), 0.0)
        b = lane_scan(lf, jnp.add, 0.0)
        g = b[:, CHUNK - 1:CHUNK]
        r = li - b
        cm = lane_scan(r, jnp.maximum, -jnp.inf)
        r_max = cm[:, CHUNK - 1:CHUNK]
        m_prev = m_run[:, 0:1]
        big_m = jnp.maximum(m_prev, cm)
        m_loc = g + r_max
        m_new = jnp.maximum(g + m_prev, m_loc)
        packed = jnp.concatenate(
            [big_m, jnp.exp(r - r_max), jnp.exp(-(b + big_m)), jnp.exp(m_prev - big_m),
             jnp.zeros((CHUNK - 4 * SUBLANES, CHUNK), F32)], axis=0)
        gate_rows.append(dict(r=r, cols=packed.T,
                              s_old=jnp.exp(g + m_prev - m_new), s_new=jnp.exp(m_loc - m_new)))
        m_run = jnp.broadcast_to(m_new, (SUBLANES, CHUNK))
    m_s[...] = m_run

    nq = HEADS * DQK
    qs = (_dot_nt(hb_s[...], w_qk[0:nq, :]) + b_qk[:, 0:nq]) * (DQK ** -0.5)
    qs_s[...] = qs
    qb_s[...] = qs.astype(BF16)
    k = _dot_nt(hb_s[...], w_qk[nq:2 * nq, :]) + b_qk[:, nq:2 * nq]
    k_s[...] = k
    kb_s[...] = k.astype(BF16)
    vb_s[...] = (_dot_nt(hb_s[...], w_vo[0:MLSTM_DIM, :]) + b_vo[:, 0:MLSTM_DIM]).astype(BF16)
    og_s[...] = jax.nn.sigmoid(_dot_nt(hb_s[...], w_vo[MLSTM_DIM:2 * MLSTM_DIM, :])
                               + b_vo[:, MLSTM_DIM:2 * MLSTM_DIM])

    row = lax.broadcasted_iota(jnp.int32, (CHUNK, CHUNK), 0)
    col = lax.broadcasted_iota(jnp.int32, (CHUNK, CHUNK), 1)
    tri = col <= row

    def pair_weights(ci, hd):
        rs = slice(ci * CHUNK, (ci + 1) * CHUNK)
        hq = slice(hd * DQK, (hd + 1) * DQK)
        gr = gate_rows[ci]
        cols = gr["cols"]
        m_col = cols[:, hd:hd + 1]
        w_col = cols[:, SUBLANES + hd:SUBLANES + hd + 1]
        r_row = gr["r"][hd:hd + 1, :]
        wts = jnp.where(tri, jnp.exp(r_row - m_col), 0.0) * _dot_nt(qb_s[rs, hq], kb_s[rs, hq])
        kw = w_col * k_s[rs, hq]
        return wts, kw.T.astype(BF16), jnp.sum(kw, axis=0, keepdims=True)

    def mlstm_pair(ci, hd, wts, kw_t, n_loc):
        rs = slice(ci * CHUNK, (ci + 1) * CHUNK)
        hq = slice(hd * DQK, (hd + 1) * DQK)
        hv = slice(hd * DV, (hd + 1) * DV)
        gr = gate_rows[ci]
        cols = gr["cols"]
        em_col = cols[:, 2 * SUBLANES + hd:2 * SUBLANES + hd + 1]
        si_col = cols[:, 3 * SUBLANES + hd:3 * SUBLANES + hd + 1]
        s_old = gr["s_old"][hd:hd + 1, :]
        s_new = gr["s_new"][hd:hd + 1, :]
        n_prev = n_s[hd:hd + 1, :]
        c_prev = c_s[hd]
        qb = qb_s[rs, hq]
        vb = vb_s[rs, hv]

        num = _dot(wts.astype(BF16), vb) + si_col * _dot(qb, c_prev.astype(BF16))
        den = (jnp.sum(wts, axis=1, keepdims=True)
               + si_col * jnp.sum(qs_s[rs, hq] * n_prev, axis=1, keepdims=True))
        hm_s[rs, hv] = num * (1.0 / jnp.maximum(jnp.abs(den), em_col))

        c_loc = _dot(kw_t, vb)
        c_s[hd] = s_old * c_prev + s_new * c_loc
        n_s[hd:hd + 1, :] = s_old * n_prev + s_new * n_loc

    def conv_block(jb):
        lo = jb * MIX_BLOCK
        cs = slice(lo, lo + MIX_BLOCK)

        def proj(off):
            return (_dot_nt(hb_s[...], w_xbc[off + lo:off + lo + MIX_BLOCK, :])
                    + b_xbc[:, off + lo:off + lo + MIX_BLOCK])

        xin = proj(0)
        bg = proj(CONV_DIM)
        cg = proj(2 * CONV_DIM)
        ubuf[SUBLANES:SUBLANES + tm, cs] = cg * xin
        u0 = ubuf[SUBLANES:SUBLANES + tm, cs]
        u1 = ubuf[SUBLANES - 1:SUBLANES - 1 + tm, cs]
        u2 = ubuf[SUBLANES - 2:SUBLANES - 2 + tm, cs]
        y = bg * (u2 * cw_ref[0:1, cs] + u1 * cw_ref[1:2, cs] + u0 * cw_ref[2:3, cs])
        y_s[:, cs] = y.astype(BF16)
        ubuf[0:SUBLANES, cs] = ubuf[tm:tm + SUBLANES, cs]

    def gate_block(jb):
        cs = slice(jb * MIX_BLOCK, (jb + 1) * MIX_BLOCK)
        sg_s[:, cs] = jax.nn.sigmoid(_dot_nt(hb_s[...], w_gt[cs, :]) + b_gt[:, cs])

    def pconv_block(jb):
        cs = slice(jb * MIX_BLOCK, (jb + 1) * MIX_BLOCK)
        sg_s[:, cs] = sg_s[:, cs] * _dot(y_s[...], w_pc[:, cs])

    nb = CONV_DIM // MIX_BLOCK
    mxu_items = ([functools.partial(conv_block, jb) for jb in range(nb)]
                 + [functools.partial(gate_block, jb) for jb in range(2 * nb)]
                 + [functools.partial(pconv_block, jb) for jb in range(nb)])

    pairs = [(ci, hd) for ci in range(n_chunks) for hd in range(HEADS)]
    ahead = pair_weights(*pairs[0])
    for idx, (ci, hd) in enumerate(pairs):
        ready = ahead
        if idx + 1 < len(pairs):
            ahead = pair_weights(*pairs[idx + 1])
        mlstm_pair(ci, hd, *ready)
        lo = idx * len(mxu_items) // len(pairs)
        hi = (idx + 1) * len(mxu_items) // len(pairs)
        for item in mxu_items[lo:hi]:
            item()

    parts = []
    for hd in range(HEADS):
        hm_h = hm_s[:, hd * DV:(hd + 1) * DV]
        parts.append(_rms_scale(hm_h) * hg_ref[:, hd * DV:(hd + 1) * DV])
    hm = og_s[...] * jnp.concatenate(parts, axis=1)
    p_m = _dot(hm.astype(BF16), w_pm[...])
    z = sg_s[:, 0:D_MODEL] + sg_s[:, D_MODEL:2 * D_MODEL] * p_m
    o_ref[0] = x + gt1 * _dot(z.astype(BF16), w_o[...])


def _const_spec(shape):
    nd = len(shape)
    return pl.BlockSpec(shape, lambda b, j: (0,) * nd, pipeline_mode=pl.Buffered(1))


def _row_window_spec(rows, block):
    return pl.BlockSpec((rows, D_MODEL), lambda b, j: (block, 0), pipeline_mode=pl.Buffered(1))


def _lane_window_spec(width, block):
    return pl.BlockSpec((1, width), lambda b, j: (0, block), pipeline_mode=pl.Buffered(1))


def _mixer_call(x, mod, g_norm, w_t, w_gt, b_in, b_all, conv_w, head_g, w_pc, w_pm, w_o):
    bsz, seq, d = x.shape
    tm = TM_MIX
    n_xbc, n_qk, n_vo, n_gt = 3 * CONV_DIM, 2 * HEADS * DQK, 2 * MLSTM_DIM, 2 * D_MODEL
    windows = [_row_window_spec(n_xbc, 0), _lane_window_spec(n_xbc, 0),
               _row_window_spec(n_qk, n_xbc // n_qk), _lane_window_spec(n_qk, n_xbc // n_qk),
               _row_window_spec(n_vo, (n_xbc + n_qk) // n_vo), _lane_window_spec(n_vo, (n_xbc + n_qk) // n_vo),
               _const_spec(w_gt.shape), _lane_window_spec(n_gt, (n_xbc + n_qk + n_vo) // n_gt),
               _row_window_spec(CHUNK, (n_xbc + n_qk + n_vo) // CHUNK),
               _lane_window_spec(CHUNK, (n_xbc + n_qk + n_vo) // CHUNK)]
    window_args = (w_t, b_all, w_t, b_all, w_t, b_all, w_gt, b_all, w_t, b_in)
    head = (g_norm,)
    tail = (conv_w, head_g, w_pc, w_pm, w_o)
    return pl.pallas_call(
        _mixer_kernel,
        grid=(bsz, seq // tm),
        in_specs=[
            pl.BlockSpec((1, tm, d), lambda b, j: (b, j, 0)),
            pl.BlockSpec((1, N_MOD, d), lambda b, j: (b, 0, 0)),
        ] + [_const_spec(a.shape) for a in head] + windows + [_const_spec(a.shape) for a in tail],
        out_specs=pl.BlockSpec((1, tm, d), lambda b, j: (b, j, 0)),
        out_shape=jax.ShapeDtypeStruct(x.shape, F32),
        scratch_shapes=[
            pltpu.VMEM((tm, D_MODEL), BF16),
            pltpu.VMEM((tm + SUBLANES, CONV_DIM), F32),
            pltpu.VMEM((tm, CONV_DIM), BF16),
            pltpu.VMEM((tm, 2 * D_MODEL), F32),
            pltpu.VMEM((tm, HEADS * DQK), F32),
            pltpu.VMEM((tm, HEADS * DQK), F32),
            pltpu.VMEM((tm, HEADS * DQK), BF16),
            pltpu.VMEM((tm, HEADS * DQK), BF16),
            pltpu.VMEM((tm, MLSTM_DIM), BF16),
            pltpu.VMEM((tm, MLSTM_DIM), F32),
            pltpu.VMEM((tm, MLSTM_DIM), F32),
            pltpu.VMEM((HEADS, DQK, DV), F32),
            pltpu.VMEM((SUBLANES, DQK), F32),
            pltpu.VMEM((SUBLANES, CHUNK), F32),
        ],
        compiler_params=pltpu.CompilerParams(
            dimension_semantics=("arbitrary", "arbitrary"),
            vmem_limit_bytes=VMEM_LIMIT_BYTES),
        name="mixer_sublayer",
    )(x, mod, *head, *window_args, *tail)


def _ffn_kernel(x_ref, mod_ref, g_ref, w_a, w_g, cw_ref, w_d, gf_ref, o_ref, abuf, *, final_norm):
    tm = TM_FFN
    j = pl.program_id(1)

    @pl.when(j == 0)
    def _():
        abuf[0:SUBLANES, :] = jnp.zeros((SUBLANES, D_FF), F32)

    x = x_ref[0]
    sh2 = mod_ref[0, 3:4, :]
    sc2 = mod_ref[0, 4:5, :]
    gt2 = mod_ref[0, 5:6, :]
    h = (_rms_scale(x) * g_ref[...]) * (1.0 + sc2) + sh2
    hb = h.astype(BF16)
    abuf[SUBLANES:SUBLANES + tm, :] = _dot(hb, w_a[...])
    gate = _dot(hb, w_g[...])
    a0 = abuf[SUBLANES:SUBLANES + tm, :]
    a1 = abuf[SUBLANES - 1:SUBLANES - 1 + tm, :]
    a2 = abuf[SUBLANES - 2:SUBLANES - 2 + tm, :]
    ac = a2 * cw_ref[0:1, :] + a1 * cw_ref[1:2, :] + a0 * cw_ref[2:3, :]
    abuf[0:SUBLANES, :] = abuf[tm:tm + SUBLANES, :]
    act = (ac * jax.nn.sigmoid(ac)) * gate
    x2 = x + gt2 * _dot(act.astype(BF16), w_d[...])
    o_ref[0] = _rms_scale(x2) * gf_ref[...] if final_norm else x2


def _ffn_call(x, mod, g_norm, w_up, conv_w, w_d, g_final, final_norm):
    bsz, seq, d = x.shape
    tm = TM_FFN
    up_halves = [pl.BlockSpec((d, D_FF), lambda b, j, half=half: (0, half), pipeline_mode=pl.Buffered(1))
                 for half in range(2)]
    tail = (conv_w, w_d, g_final)
    return pl.pallas_call(
        functools.partial(_ffn_kernel, final_norm=final_norm),
        grid=(bsz, seq // tm),
        in_specs=[
            pl.BlockSpec((1, tm, d), lambda b, j: (b, j, 0)),
            pl.BlockSpec((1, N_MOD, d), lambda b, j: (b, 0, 0)),
            _const_spec(g_norm.shape),
        ] + up_halves + [_const_spec(a.shape) for a in tail],
        out_specs=pl.BlockSpec((1, tm, d), lambda b, j: (b, j, 0)),
        out_shape=jax.ShapeDtypeStruct(x.shape, F32),
        scratch_shapes=[pltpu.VMEM((tm + SUBLANES, D_FF), F32)],
        compiler_params=pltpu.CompilerParams(
            dimension_semantics=("arbitrary", "arbitrary"),
            vmem_limit_bytes=VMEM_LIMIT_BYTES),
        name="ffn_sublayer",
    )(x, mod, g_norm, w_up, w_up, *tail)


def kernel(x, c, w_ada, b_ada, g_norm_mix, w_in, b_in, conv_mix_w, mlstm_head_g,
           w_proj_conv, w_proj_mlstm, w_out, g_norm_ffn, w_up, conv_ffn_w, w_down, g_final):
    depth = w_ada.shape[0]
    bsz = x.shape[0]
    o_if = 3 * CONV_DIM + 2 * HEADS * DQK + 2 * MLSTM_DIM
    o_gt = o_if + 2 * HEADS

    for l in range(depth):
        c_pad = jnp.pad(c, ((0, SUBLANES - bsz), (0, 0)))
        mod = _ada_call(c_pad, w_ada[l], b_ada[l][None, :])[:bsz]
        mod = mod.reshape(bsz, N_MOD, D_MODEL)

        w_t = w_in[l].T.astype(BF16)
        bi = b_in[l]
        b_all = jnp.concatenate([bi[:o_if], bi[o_gt:]])[None, :]
        x = _mixer_call(
            x, mod, g_norm_mix[l][None, :],
            w_t, w_t[o_gt:], bi[None, :], b_all,
            conv_mix_w[l], mlstm_head_g[l].reshape(1, MLSTM_DIM),
            w_proj_conv[l].astype(BF16), w_proj_mlstm[l].astype(BF16), w_out[l].astype(BF16))

        x = _ffn_call(
            x, mod, g_norm_ffn[l][None, :], w_up[l].astype(BF16),
            conv_ffn_w[l], w_down[l].astype(BF16), g_final[None, :],
            final_norm=(l == depth - 1))
    return x
```

```python
import functools

import jax
import jax.numpy as jnp
from jax import lax
from jax.experimental import pallas as pl
from jax.experimental.pallas import tpu as pltpu

F32 = jnp.float32
BF16 = jnp.bfloat16

D_MODEL = 1024
CONV_DIM = 1024
CONV_K = 3
HEADS = 4
DQK = 128
DV = 256
MLSTM_DIM = HEADS * DV
CHUNK = 128
D_FF = 2816
N_MOD = 6
EPS = 1e-6

SUBLANES = 8
TM_MIX = 512
TM_FFN = 512
MIX_BLOCK = 256
STAGE_SLOTS = 4
STAGE_ROWS = 256
FFN_STAGE_COLS = D_FF // 2
VMEM_LIMIT_BYTES = 56 * 1024 * 1024


def _dot(a, b):
    return jnp.dot(a, b, preferred_element_type=F32)


def _dot_nt(a, b):
    return lax.dot_general(a, b, (((1,), (1,)), ((), ())), preferred_element_type=F32)


def _rms_scale(x):
    return x * lax.rsqrt(jnp.mean(x * x, axis=-1, keepdims=True) + EPS)


def _log_sigmoid(x):
    return -(jnp.maximum(-x, 0.0) + jnp.log1p(jnp.exp(-jnp.abs(x))))


def _stage_to_bf16(jobs, stage, sem):
    def copy(i):
        src = jobs[i][0]
        rows, cols = src.shape
        slot = i % STAGE_SLOTS
        return pltpu.make_async_copy(src, stage.at[slot, pl.ds(0, rows), pl.ds(0, cols)], sem.at[slot])

    for i in range(min(STAGE_SLOTS, len(jobs))):
        copy(i).start()
    for i, (src, dst) in enumerate(jobs):
        rows, cols = src.shape
        copy(i).wait()
        dst[...] = stage[i % STAGE_SLOTS, 0:rows, 0:cols].astype(BF16)
        if i + STAGE_SLOTS < len(jobs):
            copy(i + STAGE_SLOTS).start()


def _row_jobs(src, src_row0, dst, n_rows):
    step = min(STAGE_ROWS, n_rows)
    return [(src.at[pl.ds(src_row0 + r, step), :], dst.at[pl.ds(r, step), :])
            for r in range(0, n_rows, step)]


def _ada_kernel(c_ref, w_ref, b_ref, o_ref):
    c = c_ref[...]
    act = c * jax.nn.sigmoid(c)
    o_ref[...] = _dot(act.astype(BF16), w_ref[...].astype(BF16)) + b_ref[...]


def _ada_call(c_pad, w_ada, b_ada):
    n = w_ada.shape[1]
    tn = 1536
    return pl.pallas_call(
        _ada_kernel,
        grid=(n // tn,),
        in_specs=[
            pl.BlockSpec((SUBLANES, D_MODEL), lambda i: (0, 0)),
            pl.BlockSpec((D_MODEL, tn), lambda i: (0, i)),
            pl.BlockSpec((1, tn), lambda i: (0, i)),
        ],
        out_specs=pl.BlockSpec((SUBLANES, tn), lambda i: (0, i)),
        out_shape=jax.ShapeDtypeStruct((SUBLANES, n), F32),
        compiler_params=pltpu.CompilerParams(dimension_semantics=("arbitrary",)),
        name="adaln_mod",
    )(c_pad, w_ada, b_ada)


def _mixer_kernel(x_ref, mod_ref, g_ref, w_in_hbm, b_xbc, b_qk, b_vo, b_gt, b_if,
                  cw_ref, hg_ref, w_pc_hbm, w_pm_hbm, w_o_hbm,
                  o_ref,
                  w_xbc, w_qk, w_vo, w_gt, w_if, w_pc, w_pm, w_o, stage, stage_sem,
                  hb_s, ubuf, y_s, sg_s, qs_s, k_s, qb_s, kb_s, vb_s, og_s, hm_s, c_s, n_s, m_s):
    tm = TM_MIX
    n_chunks = tm // CHUNK
    j = pl.program_id(1)
    n_xbc, n_qk, n_vo, n_gt = 3 * CONV_DIM, 2 * HEADS * DQK, 2 * MLSTM_DIM, 2 * D_MODEL

    @pl.when(jnp.logical_and(pl.program_id(0) == 0, j == 0))
    def _():
        o_if = n_xbc + n_qk + n_vo
        jobs = (_row_jobs(w_in_hbm, o_if, w_if, CHUNK)
                + _row_jobs(w_in_hbm, n_xbc, w_qk, n_qk)
                + _row_jobs(w_in_hbm, n_xbc + n_qk, w_vo, n_vo)
                + _row_jobs(w_in_hbm, 0, w_xbc, n_xbc)
                + _row_jobs(w_in_hbm, o_if + 2 * HEADS, w_gt, n_gt)
                + _row_jobs(w_pc_hbm, 0, w_pc, CONV_DIM)
                + _row_jobs(w_pm_hbm, 0, w_pm, MLSTM_DIM)
                + _row_jobs(w_o_hbm, 0, w_o, D_MODEL))
        _stage_to_bf16(jobs, stage, stage_sem)

    @pl.when(j == 0)
    def _():
        ubuf[0:SUBLANES, :] = jnp.zeros((SUBLANES, CONV_DIM), F32)
        c_s[...] = jnp.zeros_like(c_s)
        n_s[...] = jnp.zeros_like(n_s)
        m_s[...] = jnp.zeros_like(m_s)

    x = x_ref[0]
    sh1 = mod_ref[0, 0:1, :]
    sc1 = mod_ref[0, 1:2, :]
    gt1 = mod_ref[0, 2:3, :]
    h = (_rms_scale(x) * g_ref[...]) * (1.0 + sc1) + sh1
    hb_s[...] = h.astype(BF16)

    gif = _dot_nt(hb_s[...], w_if[...]) + b_if[...]
    lane8 = lax.broadcasted_iota(jnp.int32, (SUBLANES, CHUNK), 1)
    head_rows = lax.broadcasted_iota(jnp.int32, (SUBLANES, CHUNK), 0) < HEADS

    def lane_scan(v, op, fill):
        d = 1
        while d < CHUNK:
            v = op(v, jnp.where(lane8 >= d, pltpu.roll(v, d, axis=1), fill))
            d *= 2
        return v

    m_run = m_s[...]
    gate_rows = []
    for ci in range(n_chunks):
        gi = gif[ci * CHUNK:(ci + 1) * CHUNK, :].T[0:SUBLANES, :]
        li = jnp.where(head_rows, gi, 0.0)
        lf = jnp.where(head_rows, _log_sigmoid(pltpu.roll(gi, HEADS, axis=0)), 0.0)
        b = lane_scan(lf, jnp.add, 0.0)
        g = b[:, CHUNK - 1:CHUNK]
        r = li - b
        cm = lane_scan(r, jnp.maximum, -jnp.inf)
        r_max = cm[:, CHUNK - 1:CHUNK]
        m_prev = m_run[:, 0:1]
        big_m = jnp.maximum(m_prev, cm)
        m_loc = g + r_max
        m_new = jnp.maximum(g + m_prev, m_loc)
        packed = jnp.concatenate(
            [big_m, jnp.exp(r - r_max), jnp.exp(-(b + big_m)), jnp.exp(m_prev - big_m),
             jnp.zeros((CHUNK - 4 * SUBLANES, CHUNK), F32)], axis=0)
        gate_rows.append(dict(r=r, cols=packed.T,
                              s_old=jnp.exp(g + m_prev - m_new), s_new=jnp.exp(m_loc - m_new)))
        m_run = jnp.broadcast_to(m_new, (SUBLANES, CHUNK))
    m_s[...] = m_run

    nq = HEADS * DQK
    qs = (_dot_nt(hb_s[...], w_qk[0:nq, :]) + b_qk[:, 0:nq]) * (DQK ** -0.5)
    qs_s[...] = qs
    qb_s[...] = qs.astype(BF16)
    k = _dot_nt(hb_s[...], w_qk[nq:2 * nq, :]) + b_qk[:, nq:2 * nq]
    k_s[...] = k
    kb_s[...] = k.astype(BF16)
    vb_s[...] = (_dot_nt(hb_s[...], w_vo[0:MLSTM_DIM, :]) + b_vo[:, 0:MLSTM_DIM]).astype(BF16)
    og_s[...] = jax.nn.sigmoid(_dot_nt(hb_s[...], w_vo[MLSTM_DIM:2 * MLSTM_DIM, :])
                               + b_vo[:, MLSTM_DIM:2 * MLSTM_DIM])

    row = lax.broadcasted_iota(jnp.int32, (CHUNK, CHUNK), 0)
    col = lax.broadcasted_iota(jnp.int32, (CHUNK, CHUNK), 1)
    tri = col <= row

    def pair_weights(ci, hd):
        rs = slice(ci * CHUNK, (ci + 1) * CHUNK)
        hq = slice(hd * DQK, (hd + 1) * DQK)
        gr = gate_rows[ci]
        cols = gr["cols"]
        m_col = cols[:, hd:hd + 1]
        w_col = cols[:, SUBLANES + hd:SUBLANES + hd + 1]
        r_row = gr["r"][hd:hd + 1, :]
        wts = jnp.where(tri, jnp.exp(r_row - m_col), 0.0) * _dot_nt(qb_s[rs, hq], kb_s[rs, hq])
        kw = w_col * k_s[rs, hq]
        return wts, kw.T.astype(BF16), jnp.sum(kw, axis=0, keepdims=True)

    def mlstm_pair(ci, hd, wts, kw_t, n_loc):
        rs = slice(ci * CHUNK, (ci + 1) * CHUNK)
        hq = slice(hd * DQK, (hd + 1) * DQK)
        hv = slice(hd * DV, (hd + 1) * DV)
        gr = gate_rows[ci]
        cols = gr["cols"]
        em_col = cols[:, 2 * SUBLANES + hd:2 * SUBLANES + hd + 1]
        si_col = cols[:, 3 * SUBLANES + hd:3 * SUBLANES + hd + 1]
        s_old = gr["s_old"][hd:hd + 1, :]
        s_new = gr["s_new"][hd:hd + 1, :]
        n_prev = n_s[hd:hd + 1, :]
        c_prev = c_s[hd]
        qb = qb_s[rs, hq]
        vb = vb_s[rs, hv]

        num = _dot(wts.astype(BF16), vb) + si_col * _dot(qb, c_prev.astype(BF16))
        den = (jnp.sum(wts, axis=1, keepdims=True)
               + si_col * jnp.sum(qs_s[rs, hq] * n_prev, axis=1, keepdims=True))
        hm_s[rs, hv] = num * (1.0 / jnp.maximum(jnp.abs(den), em_col))

        c_loc = _dot(kw_t, vb)
        c_s[hd] = s_old * c_prev + s_new * c_loc
        n_s[hd:hd + 1, :] = s_old * n_prev + s_new * n_loc

    def conv_block(jb):
        lo = jb * MIX_BLOCK
        cs = slice(lo, lo + MIX_BLOCK)

        def proj(off):
            return (_dot_nt(hb_s[...], w_xbc[off + lo:off + lo + MIX_BLOCK, :])
                    + b_xbc[:, off + lo:off + lo + MIX_BLOCK])

        xin = proj(0)
        bg = proj(CONV_DIM)
        cg = proj(2 * CONV_DIM)
        ubuf[SUBLANES:SUBLANES + tm, cs] = cg * xin
        u0 = ubuf[SUBLANES:SUBLANES + tm, cs]
        u1 = ubuf[SUBLANES - 1:SUBLANES - 1 + tm, cs]
        u2 = ubuf[SUBLANES - 2:SUBLANES - 2 + tm, cs]
        y = bg * (u2 * cw_ref[0:1, cs] + u1 * cw_ref[1:2, cs] + u0 * cw_ref[2:3, cs])
        y_s[:, cs] = y.astype(BF16)
        ubuf[0:SUBLANES, cs] = ubuf[tm:tm + SUBLANES, cs]

    def gate_block(jb):
        cs = slice(jb * MIX_BLOCK, (jb + 1) * MIX_BLOCK)
        sg_s[:, cs] = jax.nn.sigmoid(_dot_nt(hb_s[...], w_gt[cs, :]) + b_gt[:, cs])

    def pconv_block(jb):
        cs = slice(jb * MIX_BLOCK, (jb + 1) * MIX_BLOCK)
        sg_s[:, cs] = sg_s[:, cs] * _dot(y_s[...], w_pc[:, cs])

    nb = CONV_DIM // MIX_BLOCK
    mxu_items = ([functools.partial(conv_block, jb) for jb in range(nb)]
                 + [functools.partial(gate_block, jb) for jb in range(2 * nb)]
                 + [functools.partial(pconv_block, jb) for jb in range(nb)])

    pairs = [(ci, hd) for ci in range(n_chunks) for hd in range(HEADS)]
    ahead = pair_weights(*pairs[0])
    for idx, (ci, hd) in enumerate(pairs):
        ready = ahead
        if idx + 1 < len(pairs):
            ahead = pair_weights(*pairs[idx + 1])
        mlstm_pair(ci, hd, *ready)
        lo = idx * len(mxu_items) // len(pairs)
        hi = (idx + 1) * len(mxu_items) // len(pairs)
        for item in mxu_items[lo:hi]:
            item()

    parts = []
    for hd in range(HEADS):
        hm_h = hm_s[:, hd * DV:(hd + 1) * DV]
        parts.append(_rms_scale(hm_h) * hg_ref[:, hd * DV:(hd + 1) * DV])
    hm = og_s[...] * jnp.concatenate(parts, axis=1)
    p_m = _dot(hm.astype(BF16), w_pm[...])
    z = sg_s[:, 0:D_MODEL] + sg_s[:, D_MODEL:2 * D_MODEL] * p_m
    o_ref[0] = x + gt1 * _dot(z.astype(BF16), w_o[...])


def _const_spec(shape):
    nd = len(shape)
    return pl.BlockSpec(shape, lambda b, j: (0,) * nd, pipeline_mode=pl.Buffered(1))


def _lane_window_spec(width, block):
    return pl.BlockSpec((1, width), lambda b, j: (0, block), pipeline_mode=pl.Buffered(1))


def _mixer_call(x, mod, g_norm, w_t, b_in, b_all, conv_w, head_g, w_pc, w_pm, w_o):
    bsz, seq, d = x.shape
    tm = TM_MIX
    n_xbc, n_qk, n_vo, n_gt = 3 * CONV_DIM, 2 * HEADS * DQK, 2 * MLSTM_DIM, 2 * D_MODEL
    bias_windows = [_lane_window_spec(n_xbc, 0), _lane_window_spec(n_qk, n_xbc // n_qk),
                    _lane_window_spec(n_vo, (n_xbc + n_qk) // n_vo),
                    _lane_window_spec(n_gt, (n_xbc + n_qk + n_vo) // n_gt),
                    _lane_window_spec(CHUNK, (n_xbc + n_qk + n_vo) // CHUNK)]
    hbm = pl.BlockSpec(memory_space=pl.ANY)
    return pl.pallas_call(
        _mixer_kernel,
        grid=(bsz, seq // tm),
        in_specs=[
            pl.BlockSpec((1, tm, d), lambda b, j: (b, j, 0)),
            pl.BlockSpec((1, N_MOD, d), lambda b, j: (b, 0, 0)),
            _const_spec(g_norm.shape), hbm,
        ] + bias_windows + [_const_spec(conv_w.shape), _const_spec(head_g.shape), hbm, hbm, hbm],
        out_specs=pl.BlockSpec((1, tm, d), lambda b, j: (b, j, 0)),
        out_shape=jax.ShapeDtypeStruct(x.shape, F32),
        scratch_shapes=[
            pltpu.VMEM((n_xbc, d), BF16),
            pltpu.VMEM((n_qk, d), BF16),
            pltpu.VMEM((n_vo, d), BF16),
            pltpu.VMEM((n_gt, d), BF16),
            pltpu.VMEM((CHUNK, d), BF16),
            pltpu.VMEM((CONV_DIM, d), BF16),
            pltpu.VMEM((MLSTM_DIM, d), BF16),
            pltpu.VMEM((d, d), BF16),
            pltpu.VMEM((STAGE_SLOTS, STAGE_ROWS, d), F32),
            pltpu.SemaphoreType.DMA((STAGE_SLOTS,)),
            pltpu.VMEM((tm, D_MODEL), BF16),
            pltpu.VMEM((tm + SUBLANES, CONV_DIM), F32),
            pltpu.VMEM((tm, CONV_DIM), BF16),
            pltpu.VMEM((tm, 2 * D_MODEL), F32),
            pltpu.VMEM((tm, HEADS * DQK), F32),
            pltpu.VMEM((tm, HEADS * DQK), F32),
            pltpu.VMEM((tm, HEADS * DQK), BF16),
            pltpu.VMEM((tm, HEADS * DQK), BF16),
            pltpu.VMEM((tm, MLSTM_DIM), BF16),
            pltpu.VMEM((tm, MLSTM_DIM), F32),
            pltpu.VMEM((tm, MLSTM_DIM), F32),
            pltpu.VMEM((HEADS, DQK, DV), F32),
            pltpu.VMEM((SUBLANES, DQK), F32),
            pltpu.VMEM((SUBLANES, CHUNK), F32),
        ],
        compiler_params=pltpu.CompilerParams(
            dimension_semantics=("arbitrary", "arbitrary"),
            vmem_limit_bytes=VMEM_LIMIT_BYTES),
        name="mixer_sublayer",
    )(x, mod, g_norm, w_t, b_all, b_all, b_all, b_all, b_in, conv_w, head_g, w_pc, w_pm, w_o)


def _ffn_kernel(x_ref, mod_ref, g_ref, w_up_hbm, cw_ref, w_d_hbm, gf_ref, o_ref,
                w_a, w_g, w_d, stage, stage_sem, abuf, *, final_norm):
    tm = TM_FFN
    j = pl.program_id(1)

    @pl.when(jnp.logical_and(pl.program_id(0) == 0, j == 0))
    def _():
        jobs = []
        for half, dst in enumerate((w_a, w_g)):
            for c in range(0, D_FF, FFN_STAGE_COLS):
                for r in range(0, D_MODEL, STAGE_ROWS):
                    jobs.append((w_up_hbm.at[pl.ds(r, STAGE_ROWS), pl.ds(half * D_FF + c, FFN_STAGE_COLS)],
                                 dst.at[pl.ds(r, STAGE_ROWS), pl.ds(c, FFN_STAGE_COLS)]))
        jobs += _row_jobs(w_d_hbm, 0, w_d, D_FF)
        _stage_to_bf16(jobs, stage, stage_sem)

    @pl.when(j == 0)
    def _():
        abuf[0:SUBLANES, :] = jnp.zeros((SUBLANES, D_FF), F32)

    x = x_ref[0]
    sh2 = mod_ref[0, 3:4, :]
    sc2 = mod_ref[0, 4:5, :]
    gt2 = mod_ref[0, 5:6, :]
    h = (_rms_scale(x) * g_ref[...]) * (1.0 + sc2) + sh2
    hb = h.astype(BF16)
    abuf[SUBLANES:SUBLANES + tm, :] = _dot(hb, w_a[...])
    gate = _dot(hb, w_g[...])
    a0 = abuf[SUBLANES:SUBLANES + tm, :]
    a1 = abuf[SUBLANES - 1:SUBLANES - 1 + tm, :]
    a2 = abuf[SUBLANES - 2:SUBLANES - 2 + tm, :]
    ac = a2 * cw_ref[0:1, :] + a1 * cw_ref[1:2, :] + a0 * cw_ref[2:3, :]
    abuf[0:SUBLANES, :] = abuf[tm:tm + SUBLANES, :]
    act = (ac * jax.nn.sigmoid(ac)) * gate
    x2 = x + gt2 * _dot(act.astype(BF16), w_d[...])
    o_ref[0] = _rms_scale(x2) * gf_ref[...] if final_norm else x2


def _ffn_call(x, mod, g_norm, w_up, conv_w, w_d, g_final, final_norm):
    bsz, seq, d = x.shape
    tm = TM_FFN
    hbm = pl.BlockSpec(memory_space=pl.ANY)
    return pl.pallas_call(
        functools.partial(_ffn_kernel, final_norm=final_norm),
        grid=(bsz, seq // tm),
        in_specs=[
            pl.BlockSpec((1, tm, d), lambda b, j: (b, j, 0)),
            pl.BlockSpec((1, N_MOD, d), lambda b, j: (b, 0, 0)),
            _const_spec(g_norm.shape), hbm, _const_spec(conv_w.shape), hbm, _const_spec(g_final.shape),
        ],
        out_specs=pl.BlockSpec((1, tm, d), lambda b, j: (b, j, 0)),
        out_shape=jax.ShapeDtypeStruct(x.shape, F32),
        scratch_shapes=[
            pltpu.VMEM((d, D_FF), BF16),
            pltpu.VMEM((d, D_FF), BF16),
            pltpu.VMEM((D_FF, d), BF16),
            pltpu.VMEM((STAGE_SLOTS, STAGE_ROWS, FFN_STAGE_COLS), F32),
            pltpu.SemaphoreType.DMA((STAGE_SLOTS,)),
            pltpu.VMEM((tm + SUBLANES, D_FF), F32),
        ],
        compiler_params=pltpu.CompilerParams(
            dimension_semantics=("arbitrary", "arbitrary"),
            vmem_limit_bytes=VMEM_LIMIT_BYTES),
        name="ffn_sublayer",
    )(x, mod, g_norm, w_up, conv_w, w_d, g_final)


def kernel(x, c, w_ada, b_ada, g_norm_mix, w_in, b_in, conv_mix_w, mlstm_head_g,
           w_proj_conv, w_proj_mlstm, w_out, g_norm_ffn, w_up, conv_ffn_w, w_down, g_final):
    depth = w_ada.shape[0]
    bsz = x.shape[0]
    o_if = 3 * CONV_DIM + 2 * HEADS * DQK + 2 * MLSTM_DIM
    o_gt = o_if + 2 * HEADS

    for l in range(depth):
        c_pad = jnp.pad(c, ((0, SUBLANES - bsz), (0, 0)))
        mod = _ada_call(c_pad, w_ada[l], b_ada[l][None, :])[:bsz]
        mod = mod.reshape(bsz, N_MOD, D_MODEL)

        bi = b_in[l]
        b_all = jnp.concatenate([bi[:o_if], bi[o_gt:]])[None, :]
        x = _mixer_call(
            x, mod, g_norm_mix[l][None, :],
            w_in[l].T, bi[None, :], b_all,
            conv_mix_w[l], mlstm_head_g[l].reshape(1, MLSTM_DIM),
            w_proj_conv[l], w_proj_mlstm[l], w_out[l])

        x = _ffn_call(
            x, mod, g_norm_ffn[l][None, :], w_up[l],
            conv_ffn_w[l], w_down[l], g_final[None, :],
            final_norm=(l == depth - 1))
    return x
```

```python
import functools

import jax
import jax.numpy as jnp
from jax import lax
from jax.experimental import pallas as pl
from jax.experimental.pallas import tpu as pltpu

F32 = jnp.float32
BF16 = jnp.bfloat16

D_MODEL = 1024
CONV_DIM = 1024
CONV_K = 3
HEADS = 4
DQK = 128
DV = 256
MLSTM_DIM = HEADS * DV
CHUNK = 128
D_FF = 2816
N_MOD = 6
EPS = 1e-6

SUBLANES = 8
TM_MIX = 512
TM_FFN = 512
MIX_BLOCK = 256
STAGE_SLOTS = 8
STAGE_ROWS = 256
FFN_STAGE_COLS = D_FF // 2
VMEM_LIMIT_BYTES = 56 * 1024 * 1024


def _dot(a, b):
    return jnp.dot(a, b, preferred_element_type=F32)


def _dot_nt(a, b):
    return lax.dot_general(a, b, (((1,), (1,)), ((), ())), preferred_element_type=F32)


def _rms_scale(x):
    return x * lax.rsqrt(jnp.mean(x * x, axis=-1, keepdims=True) + EPS)


def _log_sigmoid(x):
    return -(jnp.maximum(-x, 0.0) + jnp.log1p(jnp.exp(-jnp.abs(x))))


def _stage_to_bf16(jobs, stage, sem):
    def copy(i):
        src = jobs[i][0]
        rows, cols = src.shape
        slot = i % STAGE_SLOTS
        return pltpu.make_async_copy(src, stage.at[slot, pl.ds(0, rows), pl.ds(0, cols)], sem.at[slot])

    for i in range(min(STAGE_SLOTS, len(jobs))):
        copy(i).start()
    for i, (src, dst) in enumerate(jobs):
        rows, cols = src.shape
        copy(i).wait()
        dst[...] = stage[i % STAGE_SLOTS, 0:rows, 0:cols].astype(BF16)
        if i + STAGE_SLOTS < len(jobs):
            copy(i + STAGE_SLOTS).start()


def _row_jobs(src, src_row0, dst, n_rows):
    step = min(STAGE_ROWS, n_rows)
    return [(src.at[pl.ds(src_row0 + r, step), :], dst.at[pl.ds(r, step), :])
            for r in range(0, n_rows, step)]


def _ada_kernel(c_ref, w_ref, b_ref, o_ref):
    c = c_ref[...]
    act = c * jax.nn.sigmoid(c)
    o_ref[...] = _dot(act.astype(BF16), w_ref[...].astype(BF16)) + b_ref[...]


def _ada_call(c_pad, w_ada, b_ada):
    n = w_ada.shape[1]
    tn = 1536
    return pl.pallas_call(
        _ada_kernel,
        grid=(n // tn,),
        in_specs=[
            pl.BlockSpec((SUBLANES, D_MODEL), lambda i: (0, 0)),
            pl.BlockSpec((D_MODEL, tn), lambda i: (0, i)),
            pl.BlockSpec((1, tn), lambda i: (0, i)),
        ],
        out_specs=pl.BlockSpec((SUBLANES, tn), lambda i: (0, i)),
        out_shape=jax.ShapeDtypeStruct((SUBLANES, n), F32),
        compiler_params=pltpu.CompilerParams(dimension_semantics=("arbitrary",)),
        name="adaln_mod",
    )(c_pad, w_ada, b_ada)


def _mixer_kernel(x_ref, mod_ref, g_ref, w_in_hbm, b_xbc, b_qk, b_vo, b_gt, b_if,
                  cw_ref, hg_ref, w_pc_hbm, w_pm_hbm, w_o_hbm,
                  o_ref,
                  w_xbc, w_qk, w_vo, w_gt, w_if, w_pc, w_pm, w_o, stage, stage_sem,
                  hb_s, ubuf, y_s, sg_s, qs_s, k_s, qb_s, kb_s, vb_s, og_s, hm_s, c_s, n_s, m_s):
    tm = TM_MIX
    n_chunks = tm // CHUNK
    j = pl.program_id(1)
    n_xbc, n_qk, n_vo, n_gt = 3 * CONV_DIM, 2 * HEADS * DQK, 2 * MLSTM_DIM, 2 * D_MODEL

    @pl.when(jnp.logical_and(pl.program_id(0) == 0, j == 0))
    def _():
        o_if = n_xbc + n_qk + n_vo
        jobs = (_row_jobs(w_in_hbm, o_if, w_if, CHUNK)
                + _row_jobs(w_in_hbm, n_xbc, w_qk, n_qk)
                + _row_jobs(w_in_hbm, n_xbc + n_qk, w_vo, n_vo)
                + _row_jobs(w_in_hbm, 0, w_xbc, n_xbc)
                + _row_jobs(w_in_hbm, o_if + 2 * HEADS, w_gt, n_gt)
                + _row_jobs(w_pc_hbm, 0, w_pc, CONV_DIM)
                + _row_jobs(w_pm_hbm, 0, w_pm, MLSTM_DIM)
                + _row_jobs(w_o_hbm, 0, w_o, D_MODEL))
        _stage_to_bf16(jobs, stage, stage_sem)

    @pl.when(j == 0)
    def _():
        ubuf[0:SUBLANES, :] = jnp.zeros((SUBLANES, CONV_DIM), F32)
        c_s[...] = jnp.zeros_like(c_s)
        n_s[...] = jnp.zeros_like(n_s)
        m_s[...] = jnp.zeros_like(m_s)

    x = x_ref[0]
    mod_row = pl.ds(pl.program_id(0), 1)
    sh1 = mod_ref[mod_row, 0:D_MODEL]
    sc1 = mod_ref[mod_row, D_MODEL:2 * D_MODEL]
    gt1 = mod_ref[mod_row, 2 * D_MODEL:3 * D_MODEL]
    h = (_rms_scale(x) * g_ref[...]) * (1.0 + sc1) + sh1
    hb_s[...] = h.astype(BF16)

    gif = _dot_nt(hb_s[...], w_if[...]) + b_if[...]
    lane8 = lax.broadcasted_iota(jnp.int32, (SUBLANES, CHUNK), 1)
    head_rows = lax.broadcasted_iota(jnp.int32, (SUBLANES, CHUNK), 0) < HEADS

    def lane_scan(v, op, fill):
        d = 1
        while d < CHUNK:
            v = op(v, jnp.where(lane8 >= d, pltpu.roll(v, d, axis=1), fill))
            d *= 2
        return v

    m_run = m_s[...]
    gate_rows = []
    for ci in range(n_chunks):
        gi = gif[ci * CHUNK:(ci + 1) * CHUNK, :].T[0:SUBLANES, :]
        li = jnp.where(head_rows, gi, 0.0)
        lf = jnp.where(head_rows, _log_sigmoid(pltpu.roll(gi, HEADS, axis=0)), 0.0)
        b = lane_scan(lf, jnp.add, 0.0)
        g = b[:, CHUNK - 1:CHUNK]
        r = li - b
        cm = lane_scan(r, jnp.maximum, -jnp.inf)
        r_max = cm[:, CHUNK - 1:CHUNK]
        m_prev = m_run[:, 0:1]
        big_m = jnp.maximum(m_prev, cm)
        m_loc = g + r_max
        m_new = jnp.maximum(g + m_prev, m_loc)
        packed = jnp.concatenate(
            [big_m, jnp.exp(r - r_max), jnp.exp(-(b + big_m)), jnp.exp(m_prev - big_m),
             jnp.zeros((CHUNK - 4 * SUBLANES, CHUNK), F32)], axis=0)
        gate_rows.append(dict(r=r, cols=packed.T,
                              s_old=jnp.exp(g + m_prev - m_new), s_new=jnp.exp(m_loc - m_new)))
        m_run = jnp.broadcast_to(m_new, (SUBLANES, CHUNK))
    m_s[...] = m_run

    nq = HEADS * DQK
    qs = (_dot_nt(hb_s[...], w_qk[0:nq, :]) + b_qk[:, 0:nq]) * (DQK ** -0.5)
    qs_s[...] = qs
    qb_s[...] = qs.astype(BF16)
    k = _dot_nt(hb_s[...], w_qk[nq:2 * nq, :]) + b_qk[:, nq:2 * nq]
    k_s[...] = k
    kb_s[...] = k.astype(BF16)
    vb_s[...] = (_dot_nt(hb_s[...], w_vo[0:MLSTM_DIM, :]) + b_vo[:, 0:MLSTM_DIM]).astype(BF16)
    og_s[...] = jax.nn.sigmoid(_dot_nt(hb_s[...], w_vo[MLSTM_DIM:2 * MLSTM_DIM, :])
                               + b_vo[:, MLSTM_DIM:2 * MLSTM_DIM])

    row = lax.broadcasted_iota(jnp.int32, (CHUNK, CHUNK), 0)
    col = lax.broadcasted_iota(jnp.int32, (CHUNK, CHUNK), 1)
    tri = col <= row

    def pair_weights(ci, hd):
        rs = slice(ci * CHUNK, (ci + 1) * CHUNK)
        hq = slice(hd * DQK, (hd + 1) * DQK)
        gr = gate_rows[ci]
        cols = gr["cols"]
        m_col = cols[:, hd:hd + 1]
        w_col = cols[:, SUBLANES + hd:SUBLANES + hd + 1]
        r_row = gr["r"][hd:hd + 1, :]
        wts = jnp.where(tri, jnp.exp(r_row - m_col), 0.0) * _dot_nt(qb_s[rs, hq], kb_s[rs, hq])
        kw = w_col * k_s[rs, hq]
        return wts, kw.T.astype(BF16), jnp.sum(kw, axis=0, keepdims=True)

    def mlstm_pair(ci, hd, wts, kw_t, n_loc):
        rs = slice(ci * CHUNK, (ci + 1) * CHUNK)
        hq = slice(hd * DQK, (hd + 1) * DQK)
        hv = slice(hd * DV, (hd + 1) * DV)
        gr = gate_rows[ci]
        cols = gr["cols"]
        em_col = cols[:, 2 * SUBLANES + hd:2 * SUBLANES + hd + 1]
        si_col = cols[:, 3 * SUBLANES + hd:3 * SUBLANES + hd + 1]
        s_old = gr["s_old"][hd:hd + 1, :]
        s_new = gr["s_new"][hd:hd + 1, :]
        n_prev = n_s[hd:hd + 1, :]
        c_prev = c_s[hd]
        qb = qb_s[rs, hq]
        vb = vb_s[rs, hv]

        num = _dot(wts.astype(BF16), vb) + si_col * _dot(qb, c_prev.astype(BF16))
        den = (jnp.sum(wts, axis=1, keepdims=True)
               + si_col * jnp.sum(qs_s[rs, hq] * n_prev, axis=1, keepdims=True))
        hm_s[rs, hv] = num * (1.0 / jnp.maximum(jnp.abs(den), em_col))

        c_loc = _dot(kw_t, vb)
        c_s[hd] = s_old * c_prev + s_new * c_loc
        n_s[hd:hd + 1, :] = s_old * n_prev + s_new * n_loc

    def conv_block(jb):
        lo = jb * MIX_BLOCK
        cs = slice(lo, lo + MIX_BLOCK)

        def proj(off):
            return (_dot_nt(hb_s[...], w_xbc[off + lo:off + lo + MIX_BLOCK, :])
                    + b_xbc[:, off + lo:off + lo + MIX_BLOCK])

        xin = proj(0)
        bg = proj(CONV_DIM)
        cg = proj(2 * CONV_DIM)
        ubuf[SUBLANES:SUBLANES + tm, cs] = cg * xin
        u0 = ubuf[SUBLANES:SUBLANES + tm, cs]
        u1 = ubuf[SUBLANES - 1:SUBLANES - 1 + tm, cs]
        u2 = ubuf[SUBLANES - 2:SUBLANES - 2 + tm, cs]
        y = bg * (u2 * cw_ref[0:1, cs] + u1 * cw_ref[1:2, cs] + u0 * cw_ref[2:3, cs])
        y_s[:, cs] = y.astype(BF16)
        ubuf[0:SUBLANES, cs] = ubuf[tm:tm + SUBLANES, cs]

    def gate_block(jb):
        cs = slice(jb * MIX_BLOCK, (jb + 1) * MIX_BLOCK)
        sg_s[:, cs] = jax.nn.sigmoid(_dot_nt(hb_s[...], w_gt[cs, :]) + b_gt[:, cs])

    def pconv_block(jb):
        cs = slice(jb * MIX_BLOCK, (jb + 1) * MIX_BLOCK)
        sg_s[:, cs] = sg_s[:, cs] * _dot(y_s[...], w_pc[:, cs])

    nb = CONV_DIM // MIX_BLOCK
    mxu_items = ([functools.partial(conv_block, jb) for jb in range(nb)]
                 + [functools.partial(gate_block, jb) for jb in range(2 * nb)]
                 + [functools.partial(pconv_block, jb) for jb in range(nb)])

    pairs = [(ci, hd) for ci in range(n_chunks) for hd in range(HEADS)]
    ahead = pair_weights(*pairs[0])
    for idx, (ci, hd) in enumerate(pairs):
        ready = ahead
        if idx + 1 < len(pairs):
            ahead = pair_weights(*pairs[idx + 1])
        mlstm_pair(ci, hd, *ready)
        lo = idx * len(mxu_items) // len(pairs)
        hi = (idx + 1) * len(mxu_items) // len(pairs)
        for item in mxu_items[lo:hi]:
            item()

    parts = []
    for hd in range(HEADS):
        hm_h = hm_s[:, hd * DV:(hd + 1) * DV]
        parts.append(_rms_scale(hm_h) * hg_ref[:, hd * DV:(hd + 1) * DV])
    hm = og_s[...] * jnp.concatenate(parts, axis=1)
    p_m = _dot(hm.astype(BF16), w_pm[...])
    z = sg_s[:, 0:D_MODEL] + sg_s[:, D_MODEL:2 * D_MODEL] * p_m
    o_ref[0] = x + gt1 * _dot(z.astype(BF16), w_o[...])


def _const_spec(shape):
    nd = len(shape)
    return pl.BlockSpec(shape, lambda b, j: (0,) * nd, pipeline_mode=pl.Buffered(1))


def _lane_window_spec(width, block):
    return pl.BlockSpec((1, width), lambda b, j: (0, block), pipeline_mode=pl.Buffered(1))


def _mixer_call(x, mod, g_norm, w_t, b_in, b_all, conv_w, head_g, w_pc, w_pm, w_o):
    bsz, seq, d = x.shape
    tm = TM_MIX
    n_xbc, n_qk, n_vo, n_gt = 3 * CONV_DIM, 2 * HEADS * DQK, 2 * MLSTM_DIM, 2 * D_MODEL
    bias_windows = [_lane_window_spec(n_xbc, 0), _lane_window_spec(n_qk, n_xbc // n_qk),
                    _lane_window_spec(n_vo, (n_xbc + n_qk) // n_vo),
                    _lane_window_spec(n_gt, (n_xbc + n_qk + n_vo) // n_gt),
                    _lane_window_spec(CHUNK, (n_xbc + n_qk + n_vo) // CHUNK)]
    hbm = pl.BlockSpec(memory_space=pl.ANY)
    return pl.pallas_call(
        _mixer_kernel,
        grid=(bsz, seq // tm),
        in_specs=[
            pl.BlockSpec((1, tm, d), lambda b, j: (b, j, 0)),
            _const_spec(mod.shape), _const_spec(g_norm.shape), hbm,
        ] + bias_windows + [_const_spec(conv_w.shape), _const_spec(head_g.shape), hbm, hbm, hbm],
        out_specs=pl.BlockSpec((1, tm, d), lambda b, j: (b, j, 0)),
        out_shape=jax.ShapeDtypeStruct(x.shape, F32),
        scratch_shapes=[
            pltpu.VMEM((n_xbc, d), BF16),
            pltpu.VMEM((n_qk, d), BF16),
            pltpu.VMEM((n_vo, d), BF16),
            pltpu.VMEM((n_gt, d), BF16),
            pltpu.VMEM((CHUNK, d), BF16),
            pltpu.VMEM((CONV_DIM, d), BF16),
            pltpu.VMEM((MLSTM_DIM, d), BF16),
            pltpu.VMEM((d, d), BF16),
            pltpu.VMEM((STAGE_SLOTS, STAGE_ROWS, d), F32),
            pltpu.SemaphoreType.DMA((STAGE_SLOTS,)),
            pltpu.VMEM((tm, D_MODEL), BF16),
            pltpu.VMEM((tm + SUBLANES, CONV_DIM), F32),
            pltpu.VMEM((tm, CONV_DIM), BF16),
            pltpu.VMEM((tm, 2 * D_MODEL), F32),
            pltpu.VMEM((tm, HEADS * DQK), F32),
            pltpu.VMEM((tm, HEADS * DQK), F32),
            pltpu.VMEM((tm, HEADS * DQK), BF16),
            pltpu.VMEM((tm, HEADS * DQK), BF16),
            pltpu.VMEM((tm, MLSTM_DIM), BF16),
            pltpu.VMEM((tm, MLSTM_DIM), F32),
            pltpu.VMEM((tm, MLSTM_DIM), F32),
            pltpu.VMEM((HEADS, DQK, DV), F32),
            pltpu.VMEM((SUBLANES, DQK), F32),
            pltpu.VMEM((SUBLANES, CHUNK), F32),
        ],
        compiler_params=pltpu.CompilerParams(
            dimension_semantics=("arbitrary", "arbitrary"),
            vmem_limit_bytes=VMEM_LIMIT_BYTES),
        name="mixer_sublayer",
    )(x, mod, g_norm, w_t, b_all, b_all, b_all, b_all, b_in, conv_w, head_g, w_pc, w_pm, w_o)


def _ffn_kernel(x_ref, mod_ref, g_ref, w_up_hbm, cw_ref, w_d_hbm, gf_ref, o_ref,
                w_a, w_g, w_d, stage, stage_sem, abuf, *, final_norm):
    tm = TM_FFN
    j = pl.program_id(1)

    @pl.when(jnp.logical_and(pl.program_id(0) == 0, j == 0))
    def _():
        jobs = []
        for half, dst in enumerate((w_a, w_g)):
            for c in range(0, D_FF, FFN_STAGE_COLS):
                for r in range(0, D_MODEL, STAGE_ROWS):
                    jobs.append((w_up_hbm.at[pl.ds(r, STAGE_ROWS), pl.ds(half * D_FF + c, FFN_STAGE_COLS)],
                                 dst.at[pl.ds(r, STAGE_ROWS), pl.ds(c, FFN_STAGE_COLS)]))
        jobs += _row_jobs(w_d_hbm, 0, w_d, D_FF)
        _stage_to_bf16(jobs, stage, stage_sem)

    @pl.when(j == 0)
    def _():
        abuf[0:SUBLANES, :] = jnp.zeros((SUBLANES, D_FF), F32)

    x = x_ref[0]
    mod_row = pl.ds(pl.program_id(0), 1)
    sh2 = mod_ref[mod_row, 3 * D_MODEL:4 * D_MODEL]
    sc2 = mod_ref[mod_row, 4 * D_MODEL:5 * D_MODEL]
    gt2 = mod_ref[mod_row, 5 * D_MODEL:6 * D_MODEL]
    h = (_rms_scale(x) * g_ref[...]) * (1.0 + sc2) + sh2
    hb = h.astype(BF16)
    abuf[SUBLANES:SUBLANES + tm, :] = _dot(hb, w_a[...])
    gate = _dot(hb, w_g[...])
    a0 = abuf[SUBLANES:SUBLANES + tm, :]
    a1 = abuf[SUBLANES - 1:SUBLANES - 1 + tm, :]
    a2 = abuf[SUBLANES - 2:SUBLANES - 2 + tm, :]
    ac = a2 * cw_ref[0:1, :] + a1 * cw_ref[1:2, :] + a0 * cw_ref[2:3, :]
    abuf[0:SUBLANES, :] = abuf[tm:tm + SUBLANES, :]
    act = (ac * jax.nn.sigmoid(ac)) * gate
    x2 = x + gt2 * _dot(act.astype(BF16), w_d[...])
    o_ref[0] = _rms_scale(x2) * gf_ref[...] if final_norm else x2


def _ffn_call(x, mod, g_norm, w_up, conv_w, w_d, g_final, final_norm):
    bsz, seq, d = x.shape
    tm = TM_FFN
    hbm = pl.BlockSpec(memory_space=pl.ANY)
    return pl.pallas_call(
        functools.partial(_ffn_kernel, final_norm=final_norm),
        grid=(bsz, seq // tm),
        in_specs=[
            pl.BlockSpec((1, tm, d), lambda b, j: (b, j, 0)),
            _const_spec(mod.shape), _const_spec(g_norm.shape), hbm, _const_spec(conv_w.shape), hbm,
            _const_spec(g_final.shape),
        ],
        out_specs=pl.BlockSpec((1, tm, d), lambda b, j: (b, j, 0)),
        out_shape=jax.ShapeDtypeStruct(x.shape, F32),
        scratch_shapes=[
            pltpu.VMEM((d, D_FF), BF16),
            pltpu.VMEM((d, D_FF), BF16),
            pltpu.VMEM((D_FF, d), BF16),
            pltpu.VMEM((STAGE_SLOTS, STAGE_ROWS, FFN_STAGE_COLS), F32),
            pltpu.SemaphoreType.DMA((STAGE_SLOTS,)),
            pltpu.VMEM((tm + SUBLANES, D_FF), F32),
        ],
        compiler_params=pltpu.CompilerParams(
            dimension_semantics=("arbitrary", "arbitrary"),
            vmem_limit_bytes=VMEM_LIMIT_BYTES),
        name="ffn_sublayer",
    )(x, mod, g_norm, w_up, conv_w, w_d, g_final)


def kernel(x, c, w_ada, b_ada, g_norm_mix, w_in, b_in, conv_mix_w, mlstm_head_g,
           w_proj_conv, w_proj_mlstm, w_out, g_norm_ffn, w_up, conv_ffn_w, w_down, g_final):
    depth = w_ada.shape[0]
    bsz = x.shape[0]
    assert bsz <= SUBLANES
    o_if = 3 * CONV_DIM + 2 * HEADS * DQK + 2 * MLSTM_DIM
    o_gt = o_if + 2 * HEADS

    for l in range(depth):
        c_pad = jnp.pad(c, ((0, SUBLANES - bsz), (0, 0)))
        mod = _ada_call(c_pad, w_ada[l], b_ada[l][None, :])

        bi = b_in[l]
        b_all = jnp.concatenate([bi[:o_if], bi[o_gt:]])[None, :]
        x = _mixer_call(
            x, mod, g_norm_mix[l][None, :],
            w_in[l].T, bi[None, :], b_all,
            conv_mix_w[l], mlstm_head_g[l].reshape(1, MLSTM_DIM),
            w_proj_conv[l], w_proj_mlstm[l], w_out[l])

        x = _ffn_call(
            x, mod, g_norm_ffn[l][None, :], w_up[l],
            conv_ffn_w[l], w_down[l], g_final[None, :],
            final_norm=(l == depth - 1))
    return x
```

```python
import functools

import jax
import jax.numpy as jnp
from jax import lax
from jax.experimental import pallas as pl
from jax.experimental.pallas import tpu as pltpu

F32 = jnp.float32
BF16 = jnp.bfloat16

D_MODEL = 1024
CONV_DIM = 1024
CONV_K = 3
HEADS = 4
DQK = 128
DV = 256
MLSTM_DIM = HEADS * DV
CHUNK = 128
D_FF = 2816
N_MOD = 6
EPS = 1e-6

SUBLANES = 8
TM_MIX = 512
TM_FFN = 512
MIX_BLOCK = 256
STAGE_SLOTS = 8
STAGE_ROWS = 256
FFN_STAGE_COLS = D_FF // 2
VMEM_LIMIT_BYTES = 56 * 1024 * 1024


def _dot(a, b):
    return jnp.dot(a, b, preferred_element_type=F32)


def _dot_nt(a, b):
    return lax.dot_general(a, b, (((1,), (1,)), ((), ())), preferred_element_type=F32)


def _rms_scale(x):
    return x * lax.rsqrt(jnp.mean(x * x, axis=-1, keepdims=True) + EPS)


def _log_sigmoid(x):
    return -(jnp.maximum(-x, 0.0) + jnp.log1p(jnp.exp(-jnp.abs(x))))


def _stage_to_bf16(jobs, stage, sem):
    def copy(i):
        src = jobs[i][0]
        rows, cols = src.shape
        slot = i % STAGE_SLOTS
        return pltpu.make_async_copy(src, stage.at[slot, pl.ds(0, rows), pl.ds(0, cols)], sem.at[slot])

    for i in range(min(STAGE_SLOTS, len(jobs))):
        copy(i).start()
    for i, (src, dst) in enumerate(jobs):
        rows, cols = src.shape
        copy(i).wait()
        dst[...] = stage[i % STAGE_SLOTS, 0:rows, 0:cols].astype(BF16)
        if i + STAGE_SLOTS < len(jobs):
            copy(i + STAGE_SLOTS).start()


def _row_jobs(src, src_row0, dst, n_rows):
    step = min(STAGE_ROWS, n_rows)
    return [(src.at[pl.ds(src_row0 + r, step), :], dst.at[pl.ds(r, step), :])
            for r in range(0, n_rows, step)]


def _ada_kernel(c_ref, w_ref, b_ref, o_ref):
    c = c_ref[...]
    act = c * jax.nn.sigmoid(c)
    pad = jnp.zeros((SUBLANES - act.shape[0], act.shape[1]), F32)
    act = jnp.concatenate([act, pad], axis=0)
    o_ref[...] = _dot(act.astype(BF16), w_ref[...].astype(BF16)) + b_ref[...]


def _ada_call(c, w_ada, b_ada):
    n = w_ada.shape[1]
    tn = 1536
    return pl.pallas_call(
        _ada_kernel,
        grid=(n // tn,),
        in_specs=[
            pl.BlockSpec(c.shape, lambda i: (0, 0)),
            pl.BlockSpec((D_MODEL, tn), lambda i: (0, i)),
            pl.BlockSpec((1, tn), lambda i: (0, i)),
        ],
        out_specs=pl.BlockSpec((SUBLANES, tn), lambda i: (0, i)),
        out_shape=jax.ShapeDtypeStruct((SUBLANES, n), F32),
        compiler_params=pltpu.CompilerParams(dimension_semantics=("arbitrary",)),
        name="adaln_mod",
    )(c, w_ada, b_ada)


def _mixer_kernel(x_ref, mod_ref, g_ref, w_in_hbm, b_xbc, b_qk, b_vo, b_gt, b_if,
                  cw_ref, hg_ref, w_pc_hbm, w_pm_hbm, w_o_hbm,
                  o_ref,
                  w_xbc, w_qk, w_vo, w_gt, w_if, w_pc, w_pm, w_o, stage, stage_sem,
                  hb_s, ubuf, y_s, sg_s, qs_s, k_s, qb_s, kb_s, vb_s, og_s, hm_s, c_s, n_s, m_s):
    tm = TM_MIX
    n_chunks = tm // CHUNK
    j = pl.program_id(1)
    n_xbc, n_qk, n_vo, n_gt = 3 * CONV_DIM, 2 * HEADS * DQK, 2 * MLSTM_DIM, 2 * D_MODEL

    @pl.when(jnp.logical_and(pl.program_id(0) == 0, j == 0))
    def _():
        o_if = n_xbc + n_qk + n_vo
        jobs = (_row_jobs(w_in_hbm, o_if, w_if, CHUNK)
                + _row_jobs(w_in_hbm, n_xbc, w_qk, n_qk)
                + _row_jobs(w_in_hbm, n_xbc + n_qk, w_vo, n_vo)
                + _row_jobs(w_in_hbm, 0, w_xbc, n_xbc)
                + _row_jobs(w_in_hbm, o_if + 2 * HEADS, w_gt, n_gt)
                + _row_jobs(w_pc_hbm, 0, w_pc, CONV_DIM)
                + _row_jobs(w_pm_hbm, 0, w_pm, MLSTM_DIM)
                + _row_jobs(w_o_hbm, 0, w_o, D_MODEL))
        _stage_to_bf16(jobs, stage, stage_sem)

    @pl.when(j == 0)
    def _():
        ubuf[0:SUBLANES, :] = jnp.zeros((SUBLANES, CONV_DIM), F32)
        c_s[...] = jnp.zeros_like(c_s)
        n_s[...] = jnp.zeros_like(n_s)
        m_s[...] = jnp.zeros_like(m_s)

    x = x_ref[0]
    mod_row = pl.ds(pl.program_id(0), 1)
    sh1 = mod_ref[mod_row, 0:D_MODEL]
    sc1 = mod_ref[mod_row, D_MODEL:2 * D_MODEL]
    gt1 = mod_ref[mod_row, 2 * D_MODEL:3 * D_MODEL]
    h = (_rms_scale(x) * g_ref[...]) * (1.0 + sc1) + sh1
    hb_s[...] = h.astype(BF16)

    gif = _dot_nt(hb_s[...], w_if[...]) + b_if[...]
    lane8 = lax.broadcasted_iota(jnp.int32, (SUBLANES, CHUNK), 1)
    head_rows = lax.broadcasted_iota(jnp.int32, (SUBLANES, CHUNK), 0) < HEADS

    def lane_scan(v, op, fill):
        d = 1
        while d < CHUNK:
            v = op(v, jnp.where(lane8 >= d, pltpu.roll(v, d, axis=1), fill))
            d *= 2
        return v

    m_run = m_s[...]
    gate_rows = []
    for ci in range(n_chunks):
        gi = gif[ci * CHUNK:(ci + 1) * CHUNK, :].T[0:SUBLANES, :]
        li = jnp.where(head_rows, gi, 0.0)
        lf = jnp.where(head_rows, _log_sigmoid(pltpu.roll(gi, HEADS, axis=0)), 0.0)
        b = lane_scan(lf, jnp.add, 0.0)
        g = b[:, CHUNK - 1:CHUNK]
        r = li - b
        cm = lane_scan(r, jnp.maximum, -jnp.inf)
        r_max = cm[:, CHUNK - 1:CHUNK]
        m_prev = m_run[:, 0:1]
        big_m = jnp.maximum(m_prev, cm)
        m_loc = g + r_max
        m_new = jnp.maximum(g + m_prev, m_loc)
        gate_rows.append(dict(r=r, big_m=big_m, w=jnp.exp(r - r_max), em=jnp.exp(-(b + big_m)), m_prev=m_prev,
                              s_old=jnp.exp(g + m_prev - m_new), s_new=jnp.exp(m_loc - m_new)))
        m_run = jnp.broadcast_to(m_new, (SUBLANES, CHUNK))
    m_s[...] = m_run

    nq = HEADS * DQK
    qs = (_dot_nt(hb_s[...], w_qk[0:nq, :]) + b_qk[:, 0:nq]) * (DQK ** -0.5)
    qs_s[...] = qs
    qb_s[...] = qs.astype(BF16)
    k = _dot_nt(hb_s[...], w_qk[nq:2 * nq, :]) + b_qk[:, nq:2 * nq]
    k_s[...] = k
    kb_s[...] = k.astype(BF16)
    vb_s[...] = (_dot_nt(hb_s[...], w_vo[0:MLSTM_DIM, :]) + b_vo[:, 0:MLSTM_DIM]).astype(BF16)
    og_s[...] = jax.nn.sigmoid(_dot_nt(hb_s[...], w_vo[MLSTM_DIM:2 * MLSTM_DIM, :])
                               + b_vo[:, MLSTM_DIM:2 * MLSTM_DIM])

    row = lax.broadcasted_iota(jnp.int32, (CHUNK, CHUNK), 0)
    col = lax.broadcasted_iota(jnp.int32, (CHUNK, CHUNK), 1)
    tri = col <= row

    def token_major(row_vec):
        return jnp.broadcast_to(row_vec, (CHUNK, CHUNK)).T

    def pair_weights(ci, hd):
        rs = slice(ci * CHUNK, (ci + 1) * CHUNK)
        hq = slice(hd * DQK, (hd + 1) * DQK)
        gr = gate_rows[ci]
        m_bc = token_major(gr["big_m"][hd:hd + 1, :])
        w_bc = token_major(gr["w"][hd:hd + 1, :])
        r_row = gr["r"][hd:hd + 1, :]
        wts = jnp.where(tri, jnp.exp(r_row - m_bc), 0.0) * _dot_nt(qb_s[rs, hq], kb_s[rs, hq])
        kw = w_bc * k_s[rs, hq]
        return wts, kw.T.astype(BF16), jnp.sum(kw, axis=0, keepdims=True), m_bc

    def mlstm_pair(ci, hd, wts, kw_t, n_loc, m_bc):
        rs = slice(ci * CHUNK, (ci + 1) * CHUNK)
        hq = slice(hd * DQK, (hd + 1) * DQK)
        hv = slice(hd * DV, (hd + 1) * DV)
        gr = gate_rows[ci]
        em_bc = token_major(gr["em"][hd:hd + 1, :])
        si_bc = jnp.exp(gr["m_prev"][hd:hd + 1, :] - m_bc)
        s_old = gr["s_old"][hd:hd + 1, :]
        s_new = gr["s_new"][hd:hd + 1, :]
        n_prev = n_s[hd:hd + 1, :]
        c_prev = c_s[hd]
        qb = qb_s[rs, hq]
        vb = vb_s[rs, hv]

        num = (_dot(wts.astype(BF16), vb)
               + jnp.concatenate([si_bc, si_bc], axis=1) * _dot(qb, c_prev.astype(BF16)))
        den = (jnp.sum(wts, axis=1, keepdims=True)
               + si_bc * jnp.sum(qs_s[rs, hq] * n_prev, axis=1, keepdims=True))
        inv = 1.0 / jnp.maximum(jnp.abs(den), em_bc)
        hm_s[rs, hv] = num * jnp.concatenate([inv, inv], axis=1)

        c_loc = _dot(kw_t, vb)
        c_s[hd] = s_old * c_prev + s_new * c_loc
        n_s[hd:hd + 1, :] = s_old * n_prev + s_new * n_loc

    def conv_block(jb):
        lo = jb * MIX_BLOCK
        cs = slice(lo, lo + MIX_BLOCK)

        def proj(off):
            return (_dot_nt(hb_s[...], w_xbc[off + lo:off + lo + MIX_BLOCK, :])
                    + b_xbc[:, off + lo:off + lo + MIX_BLOCK])

        xin = proj(0)
        bg = proj(CONV_DIM)
        cg = proj(2 * CONV_DIM)
        ubuf[SUBLANES:SUBLANES + tm, cs] = cg * xin
        u0 = ubuf[SUBLANES:SUBLANES + tm, cs]
        u1 = ubuf[SUBLANES - 1:SUBLANES - 1 + tm, cs]
        u2 = ubuf[SUBLANES - 2:SUBLANES - 2 + tm, cs]
        y = bg * (u2 * cw_ref[0, :, cs] + u1 * cw_ref[1, :, cs] + u0 * cw_ref[2, :, cs])
        y_s[:, cs] = y.astype(BF16)
        ubuf[0:SUBLANES, cs] = ubuf[tm:tm + SUBLANES, cs]

    def gate_block(jb):
        cs = slice(jb * MIX_BLOCK, (jb + 1) * MIX_BLOCK)
        sg_s[:, cs] = jax.nn.sigmoid(_dot_nt(hb_s[...], w_gt[cs, :]) + b_gt[:, cs])

    def pconv_block(jb):
        cs = slice(jb * MIX_BLOCK, (jb + 1) * MIX_BLOCK)
        sg_s[:, cs] = sg_s[:, cs] * _dot(y_s[...], w_pc[:, cs])

    nb = CONV_DIM // MIX_BLOCK
    mxu_items = ([functools.partial(conv_block, jb) for jb in range(nb)]
                 + [functools.partial(gate_block, jb) for jb in range(2 * nb)]
                 + [functools.partial(pconv_block, jb) for jb in range(nb)])

    pairs = [(ci, hd) for ci in range(n_chunks) for hd in range(HEADS)]
    ahead = pair_weights(*pairs[0])
    for idx, (ci, hd) in enumerate(pairs):
        ready = ahead
        if idx + 1 < len(pairs):
            ahead = pair_weights(*pairs[idx + 1])
        mlstm_pair(ci, hd, *ready)
        lo = idx * len(mxu_items) // len(pairs)
        hi = (idx + 1) * len(mxu_items) // len(pairs)
        for item in mxu_items[lo:hi]:
            item()

    parts = []
    for hd in range(HEADS):
        hm_h = hm_s[:, hd * DV:(hd + 1) * DV]
        parts.append(_rms_scale(hm_h) * hg_ref[hd:hd + 1, :])
    hm = og_s[...] * jnp.concatenate(parts, axis=1)
    p_m = _dot(hm.astype(BF16), w_pm[...])
    z = sg_s[:, 0:D_MODEL] + sg_s[:, D_MODEL:2 * D_MODEL] * p_m
    o_ref[0] = x + gt1 * _dot(z.astype(BF16), w_o[...])


def _const_spec(shape):
    nd = len(shape)
    return pl.BlockSpec(shape, lambda b, j: (0,) * nd, pipeline_mode=pl.Buffered(1))


def _lane_window_spec(width, block):
    return pl.BlockSpec((1, width), lambda b, j: (0, block), pipeline_mode=pl.Buffered(1))


def _mixer_call(x, mod, g_norm, w_t, b_in, b_gt, conv_w, head_g, w_pc, w_pm, w_o):
    bsz, seq, d = x.shape
    tm = TM_MIX
    n_xbc, n_qk, n_vo, n_gt = 3 * CONV_DIM, 2 * HEADS * DQK, 2 * MLSTM_DIM, 2 * D_MODEL
    bias_windows = [_lane_window_spec(n_xbc, 0), _lane_window_spec(n_qk, n_xbc // n_qk),
                    _lane_window_spec(n_vo, (n_xbc + n_qk) // n_vo),
                    _const_spec(b_gt.shape),
                    _lane_window_spec(CHUNK, (n_xbc + n_qk + n_vo) // CHUNK)]
    hbm = pl.BlockSpec(memory_space=pl.ANY)
    return pl.pallas_call(
        _mixer_kernel,
        grid=(bsz, seq // tm),
        in_specs=[
            pl.BlockSpec((1, tm, d), lambda b, j: (b, j, 0)),
            _const_spec(mod.shape), _const_spec(g_norm.shape), hbm,
        ] + bias_windows + [_const_spec(conv_w.shape), _const_spec(head_g.shape), hbm, hbm, hbm],
        out_specs=pl.BlockSpec((1, tm, d), lambda b, j: (b, j, 0)),
        out_shape=jax.ShapeDtypeStruct(x.shape, F32),
        scratch_shapes=[
            pltpu.VMEM((n_xbc, d), BF16),
            pltpu.VMEM((n_qk, d), BF16),
            pltpu.VMEM((n_vo, d), BF16),
            pltpu.VMEM((n_gt, d), BF16),
            pltpu.VMEM((CHUNK, d), BF16),
            pltpu.VMEM((CONV_DIM, d), BF16),
            pltpu.VMEM((MLSTM_DIM, d), BF16),
            pltpu.VMEM((d, d), BF16),
            pltpu.VMEM((STAGE_SLOTS, STAGE_ROWS, d), F32),
            pltpu.SemaphoreType.DMA((STAGE_SLOTS,)),
            pltpu.VMEM((tm, D_MODEL), BF16),
            pltpu.VMEM((tm + SUBLANES, CONV_DIM), F32),
            pltpu.VMEM((tm, CONV_DIM), BF16),
            pltpu.VMEM((tm, 2 * D_MODEL), F32),
            pltpu.VMEM((tm, HEADS * DQK), F32),
            pltpu.VMEM((tm, HEADS * DQK), F32),
            pltpu.VMEM((tm, HEADS * DQK), BF16),
            pltpu.VMEM((tm, HEADS * DQK), BF16),
            pltpu.VMEM((tm, MLSTM_DIM), BF16),
            pltpu.VMEM((tm, MLSTM_DIM), F32),
            pltpu.VMEM((tm, MLSTM_DIM), F32),
            pltpu.VMEM((HEADS, DQK, DV), F32),
            pltpu.VMEM((SUBLANES, DQK), F32),
            pltpu.VMEM((SUBLANES, CHUNK), F32),
        ],
        compiler_params=pltpu.CompilerParams(
            dimension_semantics=("arbitrary", "arbitrary"),
            vmem_limit_bytes=VMEM_LIMIT_BYTES),
        name="mixer_sublayer",
    )(x, mod, g_norm, w_t, b_in, b_in, b_in, b_gt, b_in, conv_w, head_g, w_pc, w_pm, w_o)


def _ffn_kernel(x_ref, mod_ref, g_ref, w_up_hbm, cw_ref, w_d_hbm, gf_ref, o_ref,
                w_a, w_g, w_d, stage, stage_sem, abuf, *, final_norm):
    tm = TM_FFN
    j = pl.program_id(1)

    @pl.when(jnp.logical_and(pl.program_id(0) == 0, j == 0))
    def _():
        jobs = []
        for half, dst in enumerate((w_a, w_g)):
            for c in range(0, D_FF, FFN_STAGE_COLS):
                for r in range(0, D_MODEL, STAGE_ROWS):
                    jobs.append((w_up_hbm.at[pl.ds(r, STAGE_ROWS), pl.ds(half * D_FF + c, FFN_STAGE_COLS)],
                                 dst.at[pl.ds(r, STAGE_ROWS), pl.ds(c, FFN_STAGE_COLS)]))
        jobs += _row_jobs(w_d_hbm, 0, w_d, D_FF)
        _stage_to_bf16(jobs, stage, stage_sem)

    @pl.when(j == 0)
    def _():
        abuf[0:SUBLANES, :] = jnp.zeros((SUBLANES, D_FF), F32)

    x = x_ref[0]
    mod_row = pl.ds(pl.program_id(0), 1)
    sh2 = mod_ref[mod_row, 3 * D_MODEL:4 * D_MODEL]
    sc2 = mod_ref[mod_row, 4 * D_MODEL:5 * D_MODEL]
    gt2 = mod_ref[mod_row, 5 * D_MODEL:6 * D_MODEL]
    h = (_rms_scale(x) * g_ref[...]) * (1.0 + sc2) + sh2
    hb = h.astype(BF16)
    abuf[SUBLANES:SUBLANES + tm, :] = _dot(hb, w_a[...])
    gate = _dot(hb, w_g[...])
    a0 = abuf[SUBLANES:SUBLANES + tm, :]
    a1 = abuf[SUBLANES - 1:SUBLANES - 1 + tm, :]
    a2 = abuf[SUBLANES - 2:SUBLANES - 2 + tm, :]
    ac = a2 * cw_ref[0] + a1 * cw_ref[1] + a0 * cw_ref[2]
    abuf[0:SUBLANES, :] = abuf[tm:tm + SUBLANES, :]
    act = (ac * jax.nn.sigmoid(ac)) * gate
    x2 = x + gt2 * _dot(act.astype(BF16), w_d[...])
    o_ref[0] = _rms_scale(x2) * gf_ref[...] if final_norm else x2


def _ffn_call(x, mod, g_norm, w_up, conv_w, w_d, g_final, final_norm):
    bsz, seq, d = x.shape
    tm = TM_FFN
    hbm = pl.BlockSpec(memory_space=pl.ANY)
    return pl.pallas_call(
        functools.partial(_ffn_kernel, final_norm=final_norm),
        grid=(bsz, seq // tm),
        in_specs=[
            pl.BlockSpec((1, tm, d), lambda b, j: (b, j, 0)),
            _const_spec(mod.shape), _const_spec(g_norm.shape), hbm, _const_spec(conv_w.shape), hbm,
            _const_spec(g_final.shape),
        ],
        out_specs=pl.BlockSpec((1, tm, d), lambda b, j: (b, j, 0)),
        out_shape=jax.ShapeDtypeStruct(x.shape, F32),
        scratch_shapes=[
            pltpu.VMEM((d, D_FF), BF16),
            pltpu.VMEM((d, D_FF), BF16),
            pltpu.VMEM((D_FF, d), BF16),
            pltpu.VMEM((STAGE_SLOTS, STAGE_ROWS, FFN_STAGE_COLS), F32),
            pltpu.SemaphoreType.DMA((STAGE_SLOTS,)),
            pltpu.VMEM((tm + SUBLANES, D_FF), F32),
        ],
        compiler_params=pltpu.CompilerParams(
            dimension_semantics=("arbitrary", "arbitrary"),
            vmem_limit_bytes=VMEM_LIMIT_BYTES),
        name="ffn_sublayer",
    )(x, mod, g_norm, w_up, conv_w, w_d, g_final)


def kernel(x, c, w_ada, b_ada, g_norm_mix, w_in, b_in, conv_mix_w, mlstm_head_g,
           w_proj_conv, w_proj_mlstm, w_out, g_norm_ffn, w_up, conv_ffn_w, w_down, g_final):
    depth = w_ada.shape[0]
    bsz = x.shape[0]
    assert bsz <= SUBLANES
    o_gt = 3 * CONV_DIM + 2 * HEADS * DQK + 2 * MLSTM_DIM + 2 * HEADS

    for l in range(depth):
        mod = _ada_call(c, w_ada[l], b_ada[l][None, :])

        bi = b_in[l]
        x = _mixer_call(
            x, mod, g_norm_mix[l][None, :],
            w_in[l].T, bi[None, :], bi[None, o_gt:],
            conv_mix_w[l][:, None, :], mlstm_head_g[l],
            w_proj_conv[l], w_proj_mlstm[l], w_out[l])

        x = _ffn_call(
            x, mod, g_norm_ffn[l][None, :], w_up[l],
            conv_ffn_w[l][:, None, :], w_down[l], g_final[None, :],
            final_norm=(l == depth - 1))
    return x
```

```python
import functools

import jax
import jax.numpy as jnp
from jax import lax
from jax.experimental import pallas as pl
from jax.experimental.pallas import tpu as pltpu

F32 = jnp.float32
BF16 = jnp.bfloat16

D_MODEL = 1024
CONV_DIM = 1024
CONV_K = 3
HEADS = 4
DQK = 128
DV = 256
MLSTM_DIM = HEADS * DV
CHUNK = 128
D_FF = 2816
N_MOD = 6
EPS = 1e-6

SUBLANES = 8
TM_MIX = 512
TM_FFN = 512
MIX_BLOCK = 256
STAGE_SLOTS = 8
STAGE_ROWS = 256
FFN_STAGE_COLS = D_FF // 2
VMEM_LIMIT_BYTES = 56 * 1024 * 1024


def _dot(a, b):
    return jnp.dot(a, b, preferred_element_type=F32)


def _dot_nt(a, b):
    return lax.dot_general(a, b, (((1,), (1,)), ((), ())), preferred_element_type=F32)


def _rms_scale(x):
    return x * lax.rsqrt(jnp.mean(x * x, axis=-1, keepdims=True) + EPS)


def _log_sigmoid(x):
    return -(jnp.maximum(-x, 0.0) + jnp.log1p(jnp.exp(-jnp.abs(x))))


def _stage_to_bf16(jobs, stage, sem):
    def copy(i):
        src = jobs[i][0]
        rows, cols = src.shape
        slot = i % STAGE_SLOTS
        return pltpu.make_async_copy(src, stage.at[slot, pl.ds(0, rows), pl.ds(0, cols)], sem.at[slot])

    for i in range(min(STAGE_SLOTS, len(jobs))):
        copy(i).start()
    for i, (src, dst) in enumerate(jobs):
        rows, cols = src.shape
        copy(i).wait()
        dst[...] = stage[i % STAGE_SLOTS, 0:rows, 0:cols].astype(BF16)
        if i + STAGE_SLOTS < len(jobs):
            copy(i + STAGE_SLOTS).start()


def _row_jobs(src, src_row0, dst, n_rows):
    step = min(STAGE_ROWS, n_rows)
    return [(src.at[pl.ds(src_row0 + r, step), :], dst.at[pl.ds(r, step), :])
            for r in range(0, n_rows, step)]


def _ada_kernel(c_ref, w_ref, b_ref, o_ref):
    c = c_ref[...]
    act = c * jax.nn.sigmoid(c)
    pad = jnp.zeros((SUBLANES - act.shape[0], act.shape[1]), F32)
    act = jnp.concatenate([act, pad], axis=0)
    o_ref[...] = _dot(act.astype(BF16), w_ref[...].astype(BF16)) + b_ref[...]


def _ada_call(c, w_ada, b_ada):
    n = w_ada.shape[1]
    tn = 1536
    return pl.pallas_call(
        _ada_kernel,
        grid=(n // tn,),
        in_specs=[
            pl.BlockSpec(c.shape, lambda i: (0, 0)),
            pl.BlockSpec((D_MODEL, tn), lambda i: (0, i)),
            pl.BlockSpec((1, tn), lambda i: (0, i)),
        ],
        out_specs=pl.BlockSpec((SUBLANES, tn), lambda i: (0, i)),
        out_shape=jax.ShapeDtypeStruct((SUBLANES, n), F32),
        compiler_params=pltpu.CompilerParams(dimension_semantics=("arbitrary",)),
        name="adaln_mod",
    )(c, w_ada, b_ada)


def _mixer_kernel(x_ref, mod_ref, g_ref, w_in_hbm, b_xbc, b_qk, b_vo, b_gt, b_if,
                  cw_ref, hg_ref, w_pc_hbm, w_pm_hbm, w_o_hbm,
                  o_ref,
                  w_xbc, w_qk, w_vo, w_gt, w_if, w_pc, w_pm, w_o, stage, stage_sem,
                  hb_s, ubuf, y_s, sg_s, qs_s, k_s, qb_s, kb_s, vb_s, og_s, hm_s, c_s, n_s, m_s):
    tm = TM_MIX
    n_chunks = tm // CHUNK
    j = pl.program_id(1)
    n_xbc, n_qk, n_vo, n_gt = 3 * CONV_DIM, 2 * HEADS * DQK, 2 * MLSTM_DIM, 2 * D_MODEL

    @pl.when(jnp.logical_and(pl.program_id(0) == 0, j == 0))
    def _():
        o_if = n_xbc + n_qk + n_vo
        jobs = (_row_jobs(w_in_hbm, o_if, w_if, CHUNK)
                + _row_jobs(w_in_hbm, n_xbc, w_qk, n_qk)
                + _row_jobs(w_in_hbm, n_xbc + n_qk, w_vo, n_vo)
                + _row_jobs(w_in_hbm, 0, w_xbc, n_xbc)
                + _row_jobs(w_in_hbm, o_if + 2 * HEADS, w_gt, n_gt)
                + _row_jobs(w_pc_hbm, 0, w_pc, CONV_DIM)
                + _row_jobs(w_pm_hbm, 0, w_pm, MLSTM_DIM)
                + _row_jobs(w_o_hbm, 0, w_o, D_MODEL))
        _stage_to_bf16(jobs, stage, stage_sem)

    @pl.when(j == 0)
    def _():
        ubuf[0:SUBLANES, :] = jnp.zeros((SUBLANES, CONV_DIM), F32)
        c_s[...] = jnp.zeros_like(c_s)
        n_s[...] = jnp.zeros_like(n_s)
        m_s[...] = jnp.zeros_like(m_s)

    x = x_ref[0]
    mod_row = pl.ds(pl.program_id(0), 1)
    sh1 = mod_ref[mod_row, 0:D_MODEL]
    sc1 = mod_ref[mod_row, D_MODEL:2 * D_MODEL]
    gt1 = mod_ref[mod_row, 2 * D_MODEL:3 * D_MODEL]
    h = (_rms_scale(x) * g_ref[...]) * (1.0 + sc1) + sh1
    hb_s[...] = h.astype(BF16)

    gif = _dot_nt(hb_s[...], w_if[...]) + b_if[...]
    head_rows = lax.broadcasted_iota(jnp.int32, (SUBLANES, CHUNK), 0) < HEADS
    row = lax.broadcasted_iota(jnp.int32, (CHUNK, CHUNK), 0)
    col = lax.broadcasted_iota(jnp.int32, (CHUNK, CHUNK), 1)
    tri = col <= row

    lane8 = lax.broadcasted_iota(jnp.int32, (SUBLANES, CHUNK), 1)

    def lane_cumsum(v):
        d = 1
        while d < CHUNK:
            v = v + jnp.where(lane8 >= d, pltpu.roll(v, d, axis=1), 0.0)
            d *= 2
        return v

    m_run = m_s[...]
    gate_rows = []
    for ci in range(n_chunks):
        gi = gif[ci * CHUNK:(ci + 1) * CHUNK, :].T[0:SUBLANES, :]
        li = jnp.where(head_rows, gi, 0.0)
        lf = jnp.where(head_rows, _log_sigmoid(pltpu.roll(gi, HEADS, axis=0)), 0.0)
        b = lane_cumsum(lf)
        g = b[:, CHUNK - 1:CHUNK]
        r = li - b
        r_max = jnp.max(r, axis=1, keepdims=True)
        m_prev = m_run[:, 0:1]
        m_loc = g + r_max
        m_new = jnp.maximum(g + m_prev, m_loc)
        gate_rows.append(dict(r=r, b=b, w=jnp.exp(r - r_max), m_prev=m_prev,
                              s_old=jnp.exp(g + m_prev - m_new), s_new=jnp.exp(m_loc - m_new)))
        m_run = jnp.broadcast_to(m_new, (SUBLANES, CHUNK))
    m_s[...] = m_run

    nq = HEADS * DQK
    qs = (_dot_nt(hb_s[...], w_qk[0:nq, :]) + b_qk[:, 0:nq]) * (DQK ** -0.5)
    qs_s[...] = qs
    qb_s[...] = qs.astype(BF16)
    k = _dot_nt(hb_s[...], w_qk[nq:2 * nq, :]) + b_qk[:, nq:2 * nq]
    k_s[...] = k
    kb_s[...] = k.astype(BF16)
    vb_s[...] = (_dot_nt(hb_s[...], w_vo[0:MLSTM_DIM, :]) + b_vo[:, 0:MLSTM_DIM]).astype(BF16)
    og_s[...] = jax.nn.sigmoid(_dot_nt(hb_s[...], w_vo[MLSTM_DIM:2 * MLSTM_DIM, :])
                               + b_vo[:, MLSTM_DIM:2 * MLSTM_DIM])

    def token_major(row_vec):
        return jnp.broadcast_to(row_vec, (CHUNK, CHUNK)).T

    def pair_weights(ci, hd):
        rs = slice(ci * CHUNK, (ci + 1) * CHUNK)
        hq = slice(hd * DQK, (hd + 1) * DQK)
        gr = gate_rows[ci]
        w_bc = token_major(gr["w"][hd:hd + 1, :])
        r_row = gr["r"][hd:hd + 1, :]
        m_col = jnp.maximum(gr["m_prev"][hd:hd + 1, :],
                            jnp.max(jnp.where(tri, r_row, -jnp.inf), axis=1, keepdims=True))
        m_bc = jnp.broadcast_to(m_col, (CHUNK, CHUNK))
        wts = jnp.where(tri, jnp.exp(r_row - m_bc), 0.0) * _dot_nt(qb_s[rs, hq], kb_s[rs, hq])
        kw = w_bc * k_s[rs, hq]
        return wts, kw.T.astype(BF16), jnp.sum(kw, axis=0, keepdims=True), m_bc

    def mlstm_pair(ci, hd, wts, kw_t, n_loc, m_bc):
        rs = slice(ci * CHUNK, (ci + 1) * CHUNK)
        hq = slice(hd * DQK, (hd + 1) * DQK)
        hv = slice(hd * DV, (hd + 1) * DV)
        gr = gate_rows[ci]
        em_bc = jnp.exp(-(token_major(gr["b"][hd:hd + 1, :]) + m_bc))
        si_bc = jnp.exp(gr["m_prev"][hd:hd + 1, :] - m_bc)
        s_old = gr["s_old"][hd:hd + 1, :]
        s_new = gr["s_new"][hd:hd + 1, :]
        n_prev = n_s[hd:hd + 1, :]
        c_prev = c_s[hd]
        qb = qb_s[rs, hq]
        vb = vb_s[rs, hv]

        num = (_dot(wts.astype(BF16), vb)
               + jnp.concatenate([si_bc, si_bc], axis=1) * _dot(qb, c_prev.astype(BF16)))
        den = (jnp.sum(wts, axis=1, keepdims=True)
               + si_bc * jnp.sum(qs_s[rs, hq] * n_prev, axis=1, keepdims=True))
        inv = 1.0 / jnp.maximum(jnp.abs(den), em_bc)
        hm_s[rs, hv] = num * jnp.concatenate([inv, inv], axis=1)

        c_loc = _dot(kw_t, vb)
        c_s[hd] = s_old * c_prev + s_new * c_loc
        n_s[hd:hd + 1, :] = s_old * n_prev + s_new * n_loc

    def conv_block(jb):
        lo = jb * MIX_BLOCK
        cs = slice(lo, lo + MIX_BLOCK)

        def proj(off):
            return (_dot_nt(hb_s[...], w_xbc[off + lo:off + lo + MIX_BLOCK, :])
                    + b_xbc[:, off + lo:off + lo + MIX_BLOCK])

        xin = proj(0)
        bg = proj(CONV_DIM)
        cg = proj(2 * CONV_DIM)
        ubuf[SUBLANES:SUBLANES + tm, cs] = cg * xin
        u0 = ubuf[SUBLANES:SUBLANES + tm, cs]
        u1 = ubuf[SUBLANES - 1:SUBLANES - 1 + tm, cs]
        u2 = ubuf[SUBLANES - 2:SUBLANES - 2 + tm, cs]
        y = bg * (u2 * cw_ref[0, :, cs] + u1 * cw_ref[1, :, cs] + u0 * cw_ref[2, :, cs])
        y_s[:, cs] = y.astype(BF16)
        ubuf[0:SUBLANES, cs] = ubuf[tm:tm + SUBLANES, cs]

    def gate_block(jb):
        cs = slice(jb * MIX_BLOCK, (jb + 1) * MIX_BLOCK)
        sg_s[:, cs] = jax.nn.sigmoid(_dot_nt(hb_s[...], w_gt[cs, :]) + b_gt[:, cs])

    def pconv_block(jb):
        cs = slice(jb * MIX_BLOCK, (jb + 1) * MIX_BLOCK)
        sg_s[:, cs] = sg_s[:, cs] * _dot(y_s[...], w_pc[:, cs])

    nb = CONV_DIM // MIX_BLOCK
    mxu_items = ([functools.partial(conv_block, jb) for jb in range(nb)]
                 + [functools.partial(gate_block, jb) for jb in range(2 * nb)]
                 + [functools.partial(pconv_block, jb) for jb in range(nb)])

    pairs = [(ci, hd) for ci in range(n_chunks) for hd in range(HEADS)]
    ahead = pair_weights(*pairs[0])
    for idx, (ci, hd) in enumerate(pairs):
        ready = ahead
        if idx + 1 < len(pairs):
            ahead = pair_weights(*pairs[idx + 1])
        mlstm_pair(ci, hd, *ready)
        lo = idx * len(mxu_items) // len(pairs)
        hi = (idx + 1) * len(mxu_items) // len(pairs)
        for item in mxu_items[lo:hi]:
            item()

    parts = []
    for hd in range(HEADS):
        hm_h = hm_s[:, hd * DV:(hd + 1) * DV]
        parts.append(_rms_scale(hm_h) * hg_ref[hd:hd + 1, :])
    hm = og_s[...] * jnp.concatenate(parts, axis=1)
    p_m = _dot(hm.astype(BF16), w_pm[...])
    z = sg_s[:, 0:D_MODEL] + sg_s[:, D_MODEL:2 * D_MODEL] * p_m
    o_ref[0] = x + gt1 * _dot(z.astype(BF16), w_o[...])


def _const_spec(shape):
    nd = len(shape)
    return pl.BlockSpec(shape, lambda b, j: (0,) * nd, pipeline_mode=pl.Buffered(1))


def _lane_window_spec(width, block):
    return pl.BlockSpec((1, width), lambda b, j: (0, block), pipeline_mode=pl.Buffered(1))


def _mixer_call(x, mod, g_norm, w_t, b_in, b_gt, conv_w, head_g, w_pc, w_pm, w_o):
    bsz, seq, d = x.shape
    tm = TM_MIX
    n_xbc, n_qk, n_vo, n_gt = 3 * CONV_DIM, 2 * HEADS * DQK, 2 * MLSTM_DIM, 2 * D_MODEL
    bias_windows = [_lane_window_spec(n_xbc, 0), _lane_window_spec(n_qk, n_xbc // n_qk),
                    _lane_window_spec(n_vo, (n_xbc + n_qk) // n_vo),
                    _const_spec(b_gt.shape),
                    _lane_window_spec(CHUNK, (n_xbc + n_qk + n_vo) // CHUNK)]
    hbm = pl.BlockSpec(memory_space=pl.ANY)
    return pl.pallas_call(
        _mixer_kernel,
        grid=(bsz, seq // tm),
        in_specs=[
            pl.BlockSpec((1, tm, d), lambda b, j: (b, j, 0)),
            _const_spec(mod.shape), _const_spec(g_norm.shape), hbm,
        ] + bias_windows + [_const_spec(conv_w.shape), _const_spec(head_g.shape), hbm, hbm, hbm],
        out_specs=pl.BlockSpec((1, tm, d), lambda b, j: (b, j, 0)),
        out_shape=jax.ShapeDtypeStruct(x.shape, F32),
        scratch_shapes=[
            pltpu.VMEM((n_xbc, d), BF16),
            pltpu.VMEM((n_qk, d), BF16),
            pltpu.VMEM((n_vo, d), BF16),
            pltpu.VMEM((n_gt, d), BF16),
            pltpu.VMEM((CHUNK, d), BF16),
            pltpu.VMEM((CONV_DIM, d), BF16),
            pltpu.VMEM((MLSTM_DIM, d), BF16),
            pltpu.VMEM((d, d), BF16),
            pltpu.VMEM((STAGE_SLOTS, STAGE_ROWS, d), F32),
            pltpu.SemaphoreType.DMA((STAGE_SLOTS,)),
            pltpu.VMEM((tm, D_MODEL), BF16),
            pltpu.VMEM((tm + SUBLANES, CONV_DIM), F32),
            pltpu.VMEM((tm, CONV_DIM), BF16),
            pltpu.VMEM((tm, 2 * D_MODEL), F32),
            pltpu.VMEM((tm, HEADS * DQK), F32),
            pltpu.VMEM((tm, HEADS * DQK), F32),
            pltpu.VMEM((tm, HEADS * DQK), BF16),
            pltpu.VMEM((tm, HEADS * DQK), BF16),
            pltpu.VMEM((tm, MLSTM_DIM), BF16),
            pltpu.VMEM((tm, MLSTM_DIM), F32),
            pltpu.VMEM((tm, MLSTM_DIM), F32),
            pltpu.VMEM((HEADS, DQK, DV), F32),
            pltpu.VMEM((SUBLANES, DQK), F32),
            pltpu.VMEM((SUBLANES, CHUNK), F32),
        ],
        compiler_params=pltpu.CompilerParams(
            dimension_semantics=("arbitrary", "arbitrary"),
            vmem_limit_bytes=VMEM_LIMIT_BYTES),
        name="mixer_sublayer",
    )(x, mod, g_norm, w_t, b_in, b_in, b_in, b_gt, b_in, conv_w, head_g, w_pc, w_pm, w_o)


def _ffn_kernel(x_ref, mod_ref, g_ref, w_up_hbm, cw_ref, w_d_hbm, gf_ref, o_ref,
                w_a, w_g, w_d, stage, stage_sem, abuf, *, final_norm):
    tm = TM_FFN
    j = pl.program_id(1)

    @pl.when(jnp.logical_and(pl.program_id(0) == 0, j == 0))
    def _():
        jobs = []
        for half, dst in enumerate((w_a, w_g)):
            for c in range(0, D_FF, FFN_STAGE_COLS):
                for r in range(0, D_MODEL, STAGE_ROWS):
                    jobs.append((w_up_hbm.at[pl.ds(r, STAGE_ROWS), pl.ds(half * D_FF + c, FFN_STAGE_COLS)],
                                 dst.at[pl.ds(r, STAGE_ROWS), pl.ds(c, FFN_STAGE_COLS)]))
        jobs += _row_jobs(w_d_hbm, 0, w_d, D_FF)
        _stage_to_bf16(jobs, stage, stage_sem)

    @pl.when(j == 0)
    def _():
        abuf[0:SUBLANES, :] = jnp.zeros((SUBLANES, D_FF), F32)

    x = x_ref[0]
    mod_row = pl.ds(pl.program_id(0), 1)
    sh2 = mod_ref[mod_row, 3 * D_MODEL:4 * D_MODEL]
    sc2 = mod_ref[mod_row, 4 * D_MODEL:5 * D_MODEL]
    gt2 = mod_ref[mod_row, 5 * D_MODEL:6 * D_MODEL]
    h = (_rms_scale(x) * g_ref[...]) * (1.0 + sc2) + sh2
    hb = h.astype(BF16)
    abuf[SUBLANES:SUBLANES + tm, :] = _dot(hb, w_a[...])
    gate = _dot(hb, w_g[...])
    a0 = abuf[SUBLANES:SUBLANES + tm, :]
    a1 = abuf[SUBLANES - 1:SUBLANES - 1 + tm, :]
    a2 = abuf[SUBLANES - 2:SUBLANES - 2 + tm, :]
    ac = a2 * cw_ref[0] + a1 * cw_ref[1] + a0 * cw_ref[2]
    abuf[0:SUBLANES, :] = abuf[tm:tm + SUBLANES, :]
    act = (ac * jax.nn.sigmoid(ac)) * gate
    x2 = x + gt2 * _dot(act.astype(BF16), w_d[...])
    o_ref[0] = _rms_scale(x2) * gf_ref[...] if final_norm else x2


def _ffn_call(x, mod, g_norm, w_up, conv_w, w_d, g_final, final_norm):
    bsz, seq, d = x.shape
    tm = TM_FFN
    hbm = pl.BlockSpec(memory_space=pl.ANY)
    return pl.pallas_call(
        functools.partial(_ffn_kernel, final_norm=final_norm),
        grid=(bsz, seq // tm),
        in_specs=[
            pl.BlockSpec((1, tm, d), lambda b, j: (b, j, 0)),
            _const_spec(mod.shape), _const_spec(g_norm.shape), hbm, _const_spec(conv_w.shape), hbm,
            _const_spec(g_final.shape),
        ],
        out_specs=pl.BlockSpec((1, tm, d), lambda b, j: (b, j, 0)),
        out_shape=jax.ShapeDtypeStruct(x.shape, F32),
        scratch_shapes=[
            pltpu.VMEM((d, D_FF), BF16),
            pltpu.VMEM((d, D_FF), BF16),
            pltpu.VMEM((D_FF, d), BF16),
            pltpu.VMEM((STAGE_SLOTS, STAGE_ROWS, FFN_STAGE_COLS), F32),
            pltpu.SemaphoreType.DMA((STAGE_SLOTS,)),
            pltpu.VMEM((tm + SUBLANES, D_FF), F32),
        ],
        compiler_params=pltpu.CompilerParams(
            dimension_semantics=("arbitrary", "arbitrary"),
            vmem_limit_bytes=VMEM_LIMIT_BYTES),
        name="ffn_sublayer",
    )(x, mod, g_norm, w_up, conv_w, w_d, g_final)


def kernel(x, c, w_ada, b_ada, g_norm_mix, w_in, b_in, conv_mix_w, mlstm_head_g,
           w_proj_conv, w_proj_mlstm, w_out, g_norm_ffn, w_up, conv_ffn_w, w_down, g_final):
    depth = w_ada.shape[0]
    bsz = x.shape[0]
    assert bsz <= SUBLANES
    o_gt = 3 * CONV_DIM + 2 * HEADS * DQK + 2 * MLSTM_DIM + 2 * HEADS

    for l in range(depth):
        mod = _ada_call(c, w_ada[l], b_ada[l][None, :])

        bi = b_in[l]
        x = _mixer_call(
            x, mod, g_norm_mix[l][None, :],
            w_in[l].T, bi[None, :], bi[None, o_gt:],
            conv_mix_w[l][:, None, :], mlstm_head_g[l],
            w_proj_conv[l], w_proj_mlstm[l], w_out[l])

        x = _ffn_call(
            x, mod, g_norm_ffn[l][None, :], w_up[l],
            conv_ffn_w[l][:, None, :], w_down[l], g_final[None, :],
            final_norm=(l == depth - 1))
    return x
```

```python
import functools

import jax
import jax.numpy as jnp
from jax import lax
from jax.experimental import pallas as pl
from jax.experimental.pallas import tpu as pltpu

F32 = jnp.float32
BF16 = jnp.bfloat16

D_MODEL = 1024
CONV_DIM = 1024
CONV_K = 3
HEADS = 4
DQK = 128
DV = 256
MLSTM_DIM = HEADS * DV
CHUNK = 128
D_FF = 2816
N_MOD = 6
EPS = 1e-6

SUBLANES = 8
TM_MIX = 512
TM_FFN = 512
MIX_BLOCK = 256
STAGE_SLOTS = 8
STAGE_ROWS = 256
FFN_STAGE_COLS = D_FF // 2
VMEM_LIMIT_BYTES = 56 * 1024 * 1024


def _dot(a, b):
    return jnp.dot(a, b, preferred_element_type=F32)


def _dot_nt(a, b):
    return lax.dot_general(a, b, (((1,), (1,)), ((), ())), preferred_element_type=F32)


def _rms_scale(x):
    return x * lax.rsqrt(jnp.mean(x * x, axis=-1, keepdims=True) + EPS)


def _log_sigmoid(x):
    return -(jnp.maximum(-x, 0.0) + jnp.log1p(jnp.exp(-jnp.abs(x))))


def _stage_to_bf16(jobs, stage, sem):
    def copy(i):
        src = jobs[i][0]
        rows, cols = src.shape
        slot = i % STAGE_SLOTS
        return pltpu.make_async_copy(src, stage.at[slot, pl.ds(0, rows), pl.ds(0, cols)], sem.at[slot])

    for i in range(min(STAGE_SLOTS, len(jobs))):
        copy(i).start()
    for i, (src, dst) in enumerate(jobs):
        rows, cols = src.shape
        copy(i).wait()
        dst[...] = stage[i % STAGE_SLOTS, 0:rows, 0:cols].astype(BF16)
        if i + STAGE_SLOTS < len(jobs):
            copy(i + STAGE_SLOTS).start()


def _row_jobs(src, src_row0, dst, n_rows):
    step = min(STAGE_ROWS, n_rows)
    return [(src.at[pl.ds(src_row0 + r, step), :], dst.at[pl.ds(r, step), :])
            for r in range(0, n_rows, step)]


def _ada_kernel(c_ref, w_ref, b_ref, o_ref):
    c = c_ref[...]
    act = c * jax.nn.sigmoid(c)
    pad = jnp.zeros((SUBLANES - act.shape[0], act.shape[1]), F32)
    act = jnp.concatenate([act, pad], axis=0)
    o_ref[...] = _dot(act.astype(BF16), w_ref[...].astype(BF16)) + b_ref[...]


def _ada_call(c, w_ada, b_ada):
    n = w_ada.shape[1]
    tn = 1536
    return pl.pallas_call(
        _ada_kernel,
        grid=(n // tn,),
        in_specs=[
            pl.BlockSpec(c.shape, lambda i: (0, 0)),
            pl.BlockSpec((D_MODEL, tn), lambda i: (0, i)),
            pl.BlockSpec((1, tn), lambda i: (0, i)),
        ],
        out_specs=pl.BlockSpec((SUBLANES, tn), lambda i: (0, i)),
        out_shape=jax.ShapeDtypeStruct((SUBLANES, n), F32),
        compiler_params=pltpu.CompilerParams(dimension_semantics=("arbitrary",)),
        name="adaln_mod",
    )(c, w_ada, b_ada)


def _mixer_kernel(x_ref, mod_ref, g_ref, w_in_hbm, b_xbc, b_qk, b_vo, b_gt, b_if,
                  cw_ref, hg_ref, w_pc_hbm, w_pm_hbm, w_o_hbm,
                  o_ref,
                  w_xbc, w_qk, w_vo, w_gt, w_if, w_pc, w_pm, w_o, stage, stage_sem,
                  hb_s, ubuf, y_s, sg_s, qs_s, k_s, qb_s, kb_s, vb_s, og_s, hm_s, c_s, n_s, m_s):
    tm = TM_MIX
    n_chunks = tm // CHUNK
    j = pl.program_id(1)
    n_xbc, n_qk, n_vo, n_gt = 3 * CONV_DIM, 2 * HEADS * DQK, 2 * MLSTM_DIM, 2 * D_MODEL

    @pl.when(jnp.logical_and(pl.program_id(0) == 0, j == 0))
    def _():
        o_if = n_xbc + n_qk + n_vo
        jobs = (_row_jobs(w_in_hbm, o_if, w_if, CHUNK)
                + _row_jobs(w_in_hbm, n_xbc, w_qk, n_qk)
                + _row_jobs(w_in_hbm, n_xbc + n_qk, w_vo, n_vo)
                + _row_jobs(w_in_hbm, 0, w_xbc, n_xbc)
                + _row_jobs(w_in_hbm, o_if + 2 * HEADS, w_gt, n_gt)
                + _row_jobs(w_pc_hbm, 0, w_pc, CONV_DIM)
                + _row_jobs(w_pm_hbm, 0, w_pm, MLSTM_DIM)
                + _row_jobs(w_o_hbm, 0, w_o, D_MODEL))
        _stage_to_bf16(jobs, stage, stage_sem)

    @pl.when(j == 0)
    def _():
        ubuf[0:SUBLANES, :] = jnp.zeros((SUBLANES, CONV_DIM), F32)
        c_s[...] = jnp.zeros_like(c_s)
        n_s[...] = jnp.zeros_like(n_s)
        m_s[...] = jnp.zeros_like(m_s)

    x = x_ref[0]
    mod_row = pl.ds(pl.program_id(0), 1)
    sh1 = mod_ref[mod_row, 0:D_MODEL]
    sc1 = mod_ref[mod_row, D_MODEL:2 * D_MODEL]
    gt1 = mod_ref[mod_row, 2 * D_MODEL:3 * D_MODEL]
    h = (_rms_scale(x) * g_ref[...]) * (1.0 + sc1) + sh1
    hb_s[...] = h.astype(BF16)

    gif = _dot_nt(hb_s[...], w_if[...]) + b_if[...]
    lane8 = lax.broadcasted_iota(jnp.int32, (SUBLANES, CHUNK), 1)
    head_rows = lax.broadcasted_iota(jnp.int32, (SUBLANES, CHUNK), 0) < HEADS

    def lane_scan(v, op, fill):
        d = 1
        while d < CHUNK:
            v = op(v, jnp.where(lane8 >= d, pltpu.roll(v, d, axis=1), fill))
            d *= 2
        return v

    m_run = m_s[...]
    gate_rows = []
    for ci in range(n_chunks):
        gi = gif[ci * CHUNK:(ci + 1) * CHUNK, :].T[0:SUBLANES, :]
        li = jnp.where(head_rows, gi, 0.0)
        lf = jnp.where(head_rows, _log_sigmoid(pltpu.roll(gi, HEADS, axis=0)), 0.0)
        b = lane_scan(lf, jnp.add, 0.0)
        g = jnp.sum(lf, axis=1, keepdims=True)
        r = li - b
        cm = lane_scan(r, jnp.maximum, -jnp.inf)
        r_max = jnp.max(r, axis=1, keepdims=True)
        m_prev = m_run
        big_m = jnp.maximum(m_prev, cm)
        m_loc = g + r_max
        m_new = jnp.maximum(g + m_prev, m_loc)
        gate_rows.append(dict(r=r, big_m=big_m, w=jnp.exp(r - r_max), em=jnp.exp(-(b + big_m)), m_prev=m_prev,
                              s_old=jnp.exp(g + m_prev - m_new), s_new=jnp.exp(m_loc - m_new)))
        m_run = m_new
    m_s[...] = m_run

    nq = HEADS * DQK
    qs = (_dot_nt(hb_s[...], w_qk[0:nq, :]) + b_qk[:, 0:nq]) * (DQK ** -0.5)
    qs_s[...] = qs
    qb_s[...] = qs.astype(BF16)
    k = _dot_nt(hb_s[...], w_qk[nq:2 * nq, :]) + b_qk[:, nq:2 * nq]
    k_s[...] = k
    kb_s[...] = k.astype(BF16)
    vb_s[...] = (_dot_nt(hb_s[...], w_vo[0:MLSTM_DIM, :]) + b_vo[:, 0:MLSTM_DIM]).astype(BF16)
    og_s[...] = jax.nn.sigmoid(_dot_nt(hb_s[...], w_vo[MLSTM_DIM:2 * MLSTM_DIM, :])
                               + b_vo[:, MLSTM_DIM:2 * MLSTM_DIM])

    row = lax.broadcasted_iota(jnp.int32, (CHUNK, CHUNK), 0)
    col = lax.broadcasted_iota(jnp.int32, (CHUNK, CHUNK), 1)
    tri = col <= row

    def token_major(row_vec):
        return jnp.broadcast_to(row_vec, (CHUNK, CHUNK)).T

    def pair_weights(ci, hd):
        rs = slice(ci * CHUNK, (ci + 1) * CHUNK)
        hq = slice(hd * DQK, (hd + 1) * DQK)
        gr = gate_rows[ci]
        m_bc = token_major(gr["big_m"][hd:hd + 1, :])
        w_bc = token_major(gr["w"][hd:hd + 1, :])
        r_row = gr["r"][hd:hd + 1, :]
        wts = jnp.where(tri, jnp.exp(r_row - m_bc), 0.0) * _dot_nt(qb_s[rs, hq], kb_s[rs, hq])
        kw = w_bc * k_s[rs, hq]
        return wts, kw.T.astype(BF16), jnp.sum(kw, axis=0, keepdims=True), m_bc

    def mlstm_pair(ci, hd, wts, kw_t, n_loc, m_bc):
        rs = slice(ci * CHUNK, (ci + 1) * CHUNK)
        hq = slice(hd * DQK, (hd + 1) * DQK)
        hv = slice(hd * DV, (hd + 1) * DV)
        gr = gate_rows[ci]
        em_bc = token_major(gr["em"][hd:hd + 1, :])
        si_bc = jnp.exp(gr["m_prev"][hd:hd + 1, :] - m_bc)
        s_old = gr["s_old"][hd:hd + 1, :]
        s_new = gr["s_new"][hd:hd + 1, :]
        n_prev = n_s[hd:hd + 1, :]
        c_prev = c_s[hd]
        qb = qb_s[rs, hq]
        vb = vb_s[rs, hv]

        num = (_dot(wts.astype(BF16), vb)
               + jnp.concatenate([si_bc, si_bc], axis=1) * _dot(qb, c_prev.astype(BF16)))
        den = (jnp.sum(wts, axis=1, keepdims=True)
               + si_bc * jnp.sum(qs_s[rs, hq] * n_prev, axis=1, keepdims=True))
        inv = 1.0 / jnp.maximum(jnp.abs(den), em_bc)
        hm_s[rs, hv] = num * jnp.concatenate([inv, inv], axis=1)

        c_loc = _dot(kw_t, vb)
        c_s[hd] = (jnp.concatenate([s_old, s_old], axis=1) * c_prev
                   + jnp.concatenate([s_new, s_new], axis=1) * c_loc)
        n_s[hd:hd + 1, :] = s_old * n_prev + s_new * n_loc

    def conv_block(jb):
        lo = jb * MIX_BLOCK
        cs = slice(lo, lo + MIX_BLOCK)

        def proj(off):
            return (_dot_nt(hb_s[...], w_xbc[off + lo:off + lo + MIX_BLOCK, :])
                    + b_xbc[:, off + lo:off + lo + MIX_BLOCK])

        xin = proj(0)
        bg = proj(CONV_DIM)
        cg = proj(2 * CONV_DIM)
        ubuf[SUBLANES:SUBLANES + tm, cs] = cg * xin
        u0 = ubuf[SUBLANES:SUBLANES + tm, cs]
        u1 = ubuf[SUBLANES - 1:SUBLANES - 1 + tm, cs]
        u2 = ubuf[SUBLANES - 2:SUBLANES - 2 + tm, cs]
        y = bg * (u2 * cw_ref[0, :, cs] + u1 * cw_ref[1, :, cs] + u0 * cw_ref[2, :, cs])
        y_s[:, cs] = y.astype(BF16)
        ubuf[0:SUBLANES, cs] = ubuf[tm:tm + SUBLANES, cs]

    def gate_block(jb):
        cs = slice(jb * MIX_BLOCK, (jb + 1) * MIX_BLOCK)
        sg_s[:, cs] = jax.nn.sigmoid(_dot_nt(hb_s[...], w_gt[cs, :]) + b_gt[:, cs])

    def pconv_block(jb):
        cs = slice(jb * MIX_BLOCK, (jb + 1) * MIX_BLOCK)
        sg_s[:, cs] = sg_s[:, cs] * _dot(y_s[...], w_pc[:, cs])

    nb = CONV_DIM // MIX_BLOCK
    mxu_items = ([functools.partial(conv_block, jb) for jb in range(nb)]
                 + [functools.partial(gate_block, jb) for jb in range(2 * nb)]
                 + [functools.partial(pconv_block, jb) for jb in range(nb)])

    pairs = [(ci, hd) for ci in range(n_chunks) for hd in range(HEADS)]
    ahead = pair_weights(*pairs[0])
    for idx, (ci, hd) in enumerate(pairs):
        ready = ahead
        if idx + 1 < len(pairs):
            ahead = pair_weights(*pairs[idx + 1])
        mlstm_pair(ci, hd, *ready)
        lo = idx * len(mxu_items) // len(pairs)
        hi = (idx + 1) * len(mxu_items) // len(pairs)
        for item in mxu_items[lo:hi]:
            item()

    parts = []
    for hd in range(HEADS):
        hm_h = hm_s[:, hd * DV:(hd + 1) * DV]
        parts.append(_rms_scale(hm_h) * hg_ref[hd:hd + 1, :])
    hm = og_s[...] * jnp.concatenate(parts, axis=1)
    p_m = _dot(hm.astype(BF16), w_pm[...])
    z = sg_s[:, 0:D_MODEL] + sg_s[:, D_MODEL:2 * D_MODEL] * p_m
    o_ref[0] = x + gt1 * _dot(z.astype(BF16), w_o[...])


def _const_spec(shape):
    nd = len(shape)
    return pl.BlockSpec(shape, lambda b, j: (0,) * nd, pipeline_mode=pl.Buffered(1))


def _lane_window_spec(width, block):
    return pl.BlockSpec((1, width), lambda b, j: (0, block), pipeline_mode=pl.Buffered(1))


def _mixer_call(x, mod, g_norm, w_t, b_in, b_gt, conv_w, head_g, w_pc, w_pm, w_o):
    bsz, seq, d = x.shape
    tm = TM_MIX
    n_xbc, n_qk, n_vo, n_gt = 3 * CONV_DIM, 2 * HEADS * DQK, 2 * MLSTM_DIM, 2 * D_MODEL
    bias_windows = [_lane_window_spec(n_xbc, 0), _lane_window_spec(n_qk, n_xbc // n_qk),
                    _lane_window_spec(n_vo, (n_xbc + n_qk) // n_vo),
                    _const_spec(b_gt.shape),
                    _lane_window_spec(CHUNK, (n_xbc + n_qk + n_vo) // CHUNK)]
    hbm = pl.BlockSpec(memory_space=pl.ANY)
    return pl.pallas_call(
        _mixer_kernel,
        grid=(bsz, seq // tm),
        in_specs=[
            pl.BlockSpec((1, tm, d), lambda b, j: (b, j, 0)),
            _const_spec(mod.shape), _const_spec(g_norm.shape), hbm,
        ] + bias_windows + [_const_spec(conv_w.shape), _const_spec(head_g.shape), hbm, hbm, hbm],
        out_specs=pl.BlockSpec((1, tm, d), lambda b, j: (b, j, 0)),
        out_shape=jax.ShapeDtypeStruct(x.shape, F32),
        scratch_shapes=[
            pltpu.VMEM((n_xbc, d), BF16),
            pltpu.VMEM((n_qk, d), BF16),
            pltpu.VMEM((n_vo, d), BF16),
            pltpu.VMEM((n_gt, d), BF16),
            pltpu.VMEM((CHUNK, d), BF16),
            pltpu.VMEM((CONV_DIM, d), BF16),
            pltpu.VMEM((MLSTM_DIM, d), BF16),
            pltpu.VMEM((d, d), BF16),
            pltpu.VMEM((STAGE_SLOTS, STAGE_ROWS, d), F32),
            pltpu.SemaphoreType.DMA((STAGE_SLOTS,)),
            pltpu.VMEM((tm, D_MODEL), BF16),
            pltpu.VMEM((tm + SUBLANES, CONV_DIM), F32),
            pltpu.VMEM((tm, CONV_DIM), BF16),
            pltpu.VMEM((tm, 2 * D_MODEL), F32),
            pltpu.VMEM((tm, HEADS * DQK), F32),
            pltpu.VMEM((tm, HEADS * DQK), F32),
            pltpu.VMEM((tm, HEADS * DQK), BF16),
            pltpu.VMEM((tm, HEADS * DQK), BF16),
            pltpu.VMEM((tm, MLSTM_DIM), BF16),
            pltpu.VMEM((tm, MLSTM_DIM), F32),
            pltpu.VMEM((tm, MLSTM_DIM), F32),
            pltpu.VMEM((HEADS, DQK, DV), F32),
            pltpu.VMEM((SUBLANES, DQK), F32),
            pltpu.VMEM((SUBLANES, CHUNK), F32),
        ],
        compiler_params=pltpu.CompilerParams(
            dimension_semantics=("arbitrary", "arbitrary"),
            vmem_limit_bytes=VMEM_LIMIT_BYTES),
        name="mixer_sublayer",
    )(x, mod, g_norm, w_t, b_in, b_in, b_in, b_gt, b_in, conv_w, head_g, w_pc, w_pm, w_o)


def _ffn_kernel(x_ref, mod_ref, g_ref, w_up_hbm, cw_ref, w_d_hbm, gf_ref, o_ref,
                w_a, w_g, w_d, stage, stage_sem, abuf, *, final_norm):
    tm = TM_FFN
    j = pl.program_id(1)

    @pl.when(jnp.logical_and(pl.program_id(0) == 0, j == 0))
    def _():
        jobs = []
        for half, dst in enumerate((w_a, w_g)):
            for c in range(0, D_FF, FFN_STAGE_COLS):
                for r in range(0, D_MODEL, STAGE_ROWS):
                    jobs.append((w_up_hbm.at[pl.ds(r, STAGE_ROWS), pl.ds(half * D_FF + c, FFN_STAGE_COLS)],
                                 dst.at[pl.ds(r, STAGE_ROWS), pl.ds(c, FFN_STAGE_COLS)]))
        jobs += _row_jobs(w_d_hbm, 0, w_d, D_FF)
        _stage_to_bf16(jobs, stage, stage_sem)

    @pl.when(j == 0)
    def _():
        abuf[0:SUBLANES, :] = jnp.zeros((SUBLANES, D_FF), F32)

    x = x_ref[0]
    mod_row = pl.ds(pl.program_id(0), 1)
    sh2 = mod_ref[mod_row, 3 * D_MODEL:4 * D_MODEL]
    sc2 = mod_ref[mod_row, 4 * D_MODEL:5 * D_MODEL]
    gt2 = mod_ref[mod_row, 5 * D_MODEL:6 * D_MODEL]
    h = (_rms_scale(x) * g_ref[...]) * (1.0 + sc2) + sh2
    hb = h.astype(BF16)
    abuf[SUBLANES:SUBLANES + tm, :] = _dot(hb, w_a[...])
    gate = _dot(hb, w_g[...])
    a0 = abuf[SUBLANES:SUBLANES + tm, :]
    a1 = abuf[SUBLANES - 1:SUBLANES - 1 + tm, :]
    a2 = abuf[SUBLANES - 2:SUBLANES - 2 + tm, :]
    ac = a2 * cw_ref[0] + a1 * cw_ref[1] + a0 * cw_ref[2]
    abuf[0:SUBLANES, :] = abuf[tm:tm + SUBLANES, :]
    act = (ac * jax.nn.sigmoid(ac)) * gate
    x2 = x + gt2 * _dot(act.astype(BF16), w_d[...])
    o_ref[0] = _rms_scale(x2) * gf_ref[...] if final_norm else x2


def _ffn_call(x, mod, g_norm, w_up, conv_w, w_d, g_final, final_norm):
    bsz, seq, d = x.shape
    tm = TM_FFN
    hbm = pl.BlockSpec(memory_space=pl.ANY)
    return pl.pallas_call(
        functools.partial(_ffn_kernel, final_norm=final_norm),
        grid=(bsz, seq // tm),
        in_specs=[
            pl.BlockSpec((1, tm, d), lambda b, j: (b, j, 0)),
            _const_spec(mod.shape), _const_spec(g_norm.shape), hbm, _const_spec(conv_w.shape), hbm,
            _const_spec(g_final.shape),
        ],
        out_specs=pl.BlockSpec((1, tm, d), lambda b, j: (b, j, 0)),
        out_shape=jax.ShapeDtypeStruct(x.shape, F32),
        scratch_shapes=[
            pltpu.VMEM((d, D_FF), BF16),
            pltpu.VMEM((d, D_FF), BF16),
            pltpu.VMEM((D_FF, d), BF16),
            pltpu.VMEM((STAGE_SLOTS, STAGE_ROWS, FFN_STAGE_COLS), F32),
            pltpu.SemaphoreType.DMA((STAGE_SLOTS,)),
            pltpu.VMEM((tm + SUBLANES, D_FF), F32),
        ],
        compiler_params=pltpu.CompilerParams(
            dimension_semantics=("arbitrary", "arbitrary"),
            vmem_limit_bytes=VMEM_LIMIT_BYTES),
        name="ffn_sublayer",
    )(x, mod, g_norm, w_up, conv_w, w_d, g_final)


def kernel(x, c, w_ada, b_ada, g_norm_mix, w_in, b_in, conv_mix_w, mlstm_head_g,
           w_proj_conv, w_proj_mlstm, w_out, g_norm_ffn, w_up, conv_ffn_w, w_down, g_final):
    depth = w_ada.shape[0]
    bsz = x.shape[0]
    assert bsz <= SUBLANES
    o_gt = 3 * CONV_DIM + 2 * HEADS * DQK + 2 * MLSTM_DIM + 2 * HEADS

    for l in range(depth):
        mod = _ada_call(c, w_ada[l], b_ada[l][None, :])

        bi = b_in[l]
        x = _mixer_call(
            x, mod, g_norm_mix[l][None, :],
            w_in[l].T, bi[None, :], bi[None, o_gt:],
            conv_mix_w[l][:, None, :], mlstm_head_g[l],
            w_proj_conv[l], w_proj_mlstm[l], w_out[l])

        x = _ffn_call(
            x, mod, g_norm_ffn[l][None, :], w_up[l],
            conv_ffn_w[l][:, None, :], w_down[l], g_final[None, :],
            final_norm=(l == depth - 1))
    return x
```

```python
import functools

import jax
import jax.numpy as jnp
from jax import lax
from jax.experimental import pallas as pl
from jax.experimental.pallas import tpu as pltpu

F32 = jnp.float32
BF16 = jnp.bfloat16

D_MODEL = 1024
CONV_DIM = 1024
CONV_K = 3
HEADS = 4
DQK = 128
DV = 256
MLSTM_DIM = HEADS * DV
CHUNK = 128
D_FF = 2816
N_MOD = 6
EPS = 1e-6

SUBLANES = 8
TM_MIX = 512
TM_FFN = 512
MIX_BLOCK = 256
STAGE_SLOTS = 8
STAGE_ROWS = 256
FFN_STAGE_COLS = D_FF // 2
V7X_VMEM_BYTES = 64 * 1024 * 1024
VMEM_LIMIT_BYTES = V7X_VMEM_BYTES - 8 * 1024 * 1024
ADA_BLOCK = 3072
FFN_DOWN_ROWS = 256


def _dot(a, b):
    return jnp.dot(a, b, preferred_element_type=F32)


def _dot_nt(a, b):
    return lax.dot_general(a, b, (((1,), (1,)), ((), ())), preferred_element_type=F32)


def _rms_scale(x):
    return x * lax.rsqrt(jnp.mean(x * x, axis=-1, keepdims=True) + EPS)


def _log_sigmoid(x):
    return -(jnp.maximum(-x, 0.0) + jnp.log1p(jnp.exp(-jnp.abs(x))))


def _stage_to_bf16(jobs, stage, sem):
    def copy(i):
        src = jobs[i][0]
        rows, cols = src.shape
        slot = i % STAGE_SLOTS
        return pltpu.make_async_copy(src, stage.at[slot, pl.ds(0, rows), pl.ds(0, cols)], sem.at[slot])

    for i in range(min(STAGE_SLOTS, len(jobs))):
        copy(i).start()
    for i, (src, dst) in enumerate(jobs):
        rows, cols = src.shape
        copy(i).wait()
        dst[...] = stage[i % STAGE_SLOTS, 0:rows, 0:cols].astype(BF16)
        if i + STAGE_SLOTS < len(jobs):
            copy(i + STAGE_SLOTS).start()


def _row_jobs(src, src_row0, dst, n_rows):
    step = min(STAGE_ROWS, n_rows)
    return [(src.at[pl.ds(src_row0 + r, step), :], dst.at[pl.ds(r, step), :])
            for r in range(0, n_rows, step)]


def _ada_kernel(c_ref, w_ref, b_ref, o_ref):
    c = c_ref[...]
    act = c * jax.nn.sigmoid(c)
    pad = jnp.zeros((SUBLANES - act.shape[0], act.shape[1]), F32)
    act = jnp.concatenate([act, pad], axis=0)
    o_ref[...] = _dot(act.astype(BF16), w_ref[...].astype(BF16)) + b_ref[...]


def _ada_call(c, w_ada, b_ada):
    n = w_ada.shape[1]
    tn = ADA_BLOCK
    return pl.pallas_call(
        _ada_kernel,
        grid=(n // tn,),
        in_specs=[
            pl.BlockSpec(c.shape, lambda i: (0, 0)),
            pl.BlockSpec((D_MODEL, tn), lambda i: (0, i)),
            pl.BlockSpec((1, tn), lambda i: (0, i)),
        ],
        out_specs=pl.BlockSpec((SUBLANES, tn), lambda i: (0, i)),
        out_shape=jax.ShapeDtypeStruct((SUBLANES, n), F32),
        compiler_params=pltpu.CompilerParams(dimension_semantics=("arbitrary",)),
        name="adaln_mod",
    )(c, w_ada, b_ada)


def _mixer_kernel(x_ref, mod_ref, g_ref, w_in_hbm, b_xbc, b_qk, b_vo, b_gt, b_if,
                  cw_ref, hg_ref, w_pc_hbm, w_pm_hbm, w_o_hbm,
                  o_ref,
                  w_xbc, w_qk, w_vo, w_gt, w_if, w_pc, w_pm, w_o, stage, stage_sem,
                  hb_s, ubuf, y_s, sg_s, qs_s, k_s, qb_s, kb_s, vb_s, og_s, hm_s, c_s, n_s, m_s):
    tm = TM_MIX
    n_chunks = tm // CHUNK
    j = pl.program_id(1)
    n_xbc, n_qk, n_vo, n_gt = 3 * CONV_DIM, 2 * HEADS * DQK, 2 * MLSTM_DIM, 2 * D_MODEL

    @pl.when(jnp.logical_and(pl.program_id(0) == 0, j == 0))
    def _():
        o_if = n_xbc + n_qk + n_vo
        jobs = (_row_jobs(w_in_hbm, o_if, w_if, CHUNK)
                + _row_jobs(w_in_hbm, n_xbc, w_qk, n_qk)
                + _row_jobs(w_in_hbm, n_xbc + n_qk, w_vo, n_vo)
                + _row_jobs(w_in_hbm, 0, w_xbc, n_xbc)
                + _row_jobs(w_in_hbm, o_if + 2 * HEADS, w_gt, n_gt)
                + _row_jobs(w_pc_hbm, 0, w_pc, CONV_DIM)
                + _row_jobs(w_pm_hbm, 0, w_pm, MLSTM_DIM)
                + _row_jobs(w_o_hbm, 0, w_o, D_MODEL))
        _stage_to_bf16(jobs, stage, stage_sem)

    @pl.when(j == 0)
    def _():
        ubuf[0:SUBLANES, :] = jnp.zeros((SUBLANES, CONV_DIM), F32)
        c_s[...] = jnp.zeros_like(c_s)
        n_s[...] = jnp.zeros_like(n_s)
        m_s[...] = jnp.zeros_like(m_s)

    x = x_ref[0]
    mod_row = pl.ds(pl.program_id(0), 1)
    sh1 = mod_ref[mod_row, 0:D_MODEL]
    sc1 = mod_ref[mod_row, D_MODEL:2 * D_MODEL]
    gt1 = mod_ref[mod_row, 2 * D_MODEL:3 * D_MODEL]
    h = (_rms_scale(x) * g_ref[...]) * (1.0 + sc1) + sh1
    hb_s[...] = h.astype(BF16)

    gif = _dot_nt(hb_s[...], w_if[...]) + b_if[...]
    lane8 = lax.broadcasted_iota(jnp.int32, (SUBLANES, CHUNK), 1)
    head_rows = lax.broadcasted_iota(jnp.int32, (SUBLANES, CHUNK), 0) < HEADS

    def lane_scan(v, op, fill):
        d = 1
        while d < CHUNK:
            v = op(v, jnp.where(lane8 >= d, pltpu.roll(v, d, axis=1), fill))
            d *= 2
        return v

    m_run = m_s[...]
    gate_rows = []
    for ci in range(n_chunks):
        gi = gif[ci * CHUNK:(ci + 1) * CHUNK, :].T[0:SUBLANES, :]
        li = jnp.where(head_rows, gi, 0.0)
        lf = jnp.where(head_rows, _log_sigmoid(pltpu.roll(gi, HEADS, axis=0)), 0.0)
        b = lane_scan(lf, jnp.add, 0.0)
        g = b[:, CHUNK - 1:CHUNK]
        r = li - b
        cm = lane_scan(r, jnp.maximum, -jnp.inf)
        r_max = cm[:, CHUNK - 1:CHUNK]
        m_prev = m_run[:, 0:1]
        big_m = jnp.maximum(m_prev, cm)
        m_loc = g + r_max
        m_new = jnp.maximum(g + m_prev, m_loc)
        gate_rows.append(dict(r=r, big_m=big_m, w=jnp.exp(r - r_max), em=jnp.exp(-(b + big_m)), m_prev=m_prev,
                              s_old=jnp.exp(g + m_prev - m_new), s_new=jnp.exp(m_loc - m_new)))
        m_run = jnp.broadcast_to(m_new, (SUBLANES, CHUNK))
    m_s[...] = m_run

    nq = HEADS * DQK
    qs = (_dot_nt(hb_s[...], w_qk[0:nq, :]) + b_qk[:, 0:nq]) * (DQK ** -0.5)
    qs_s[...] = qs
    qb_s[...] = qs.astype(BF16)
    k = _dot_nt(hb_s[...], w_qk[nq:2 * nq, :]) + b_qk[:, nq:2 * nq]
    k_s[...] = k
    kb_s[...] = k.astype(BF16)
    vb_s[...] = (_dot_nt(hb_s[...], w_vo[0:MLSTM_DIM, :]) + b_vo[:, 0:MLSTM_DIM]).astype(BF16)
    og_s[...] = jax.nn.sigmoid(_dot_nt(hb_s[...], w_vo[MLSTM_DIM:2 * MLSTM_DIM, :])
                               + b_vo[:, MLSTM_DIM:2 * MLSTM_DIM])

    row = lax.broadcasted_iota(jnp.int32, (CHUNK, CHUNK), 0)
    col = lax.broadcasted_iota(jnp.int32, (CHUNK, CHUNK), 1)
    tri = col <= row

    def token_major(row_vec):
        return jnp.broadcast_to(row_vec, (CHUNK, CHUNK)).T

    def pair_weights(ci, hd):
        rs = slice(ci * CHUNK, (ci + 1) * CHUNK)
        hq = slice(hd * DQK, (hd + 1) * DQK)
        gr = gate_rows[ci]
        m_bc = token_major(gr["big_m"][hd:hd + 1, :])
        w_bc = token_major(gr["w"][hd:hd + 1, :])
        r_row = gr["r"][hd:hd + 1, :]
        wts = jnp.where(tri, jnp.exp(r_row - m_bc), 0.0) * _dot_nt(qb_s[rs, hq], kb_s[rs, hq])
        kw = w_bc * k_s[rs, hq]
        return wts, kw.T.astype(BF16), jnp.sum(kw, axis=0, keepdims=True), m_bc

    def mlstm_pair(ci, hd, wts, kw_t, n_loc, m_bc):
        rs = slice(ci * CHUNK, (ci + 1) * CHUNK)
        hq = slice(hd * DQK, (hd + 1) * DQK)
        hv = slice(hd * DV, (hd + 1) * DV)
        gr = gate_rows[ci]
        em_bc = token_major(gr["em"][hd:hd + 1, :])
        si_bc = jnp.exp(gr["m_prev"][hd:hd + 1, :] - m_bc)
        s_old = gr["s_old"][hd:hd + 1, :]
        s_new = gr["s_new"][hd:hd + 1, :]
        n_prev = n_s[hd:hd + 1, :]
        c_prev = c_s[hd]
        qb = qb_s[rs, hq]
        vb = vb_s[rs, hv]

        num = (_dot(wts.astype(BF16), vb)
               + jnp.concatenate([si_bc, si_bc], axis=1) * _dot(qb, c_prev.astype(BF16)))
        den = (jnp.sum(wts, axis=1, keepdims=True)
               + si_bc * jnp.sum(qs_s[rs, hq] * n_prev, axis=1, keepdims=True))
        inv = 1.0 / jnp.maximum(jnp.abs(den), em_bc)
        hm_s[rs, hv] = num * jnp.concatenate([inv, inv], axis=1)

        c_loc = _dot(kw_t, vb)
        c_s[hd] = s_old * c_prev + s_new * c_loc
        n_s[hd:hd + 1, :] = s_old * n_prev + s_new * n_loc

    def conv_block(jb):
        lo = jb * MIX_BLOCK
        cs = slice(lo, lo + MIX_BLOCK)

        def proj(off):
            return (_dot_nt(hb_s[...], w_xbc[off + lo:off + lo + MIX_BLOCK, :])
                    + b_xbc[:, off + lo:off + lo + MIX_BLOCK])

        xin = proj(0)
        bg = proj(CONV_DIM)
        cg = proj(2 * CONV_DIM)
        ubuf[SUBLANES:SUBLANES + tm, cs] = cg * xin
        u0 = ubuf[SUBLANES:SUBLANES + tm, cs]
        u1 = ubuf[SUBLANES - 1:SUBLANES - 1 + tm, cs]
        u2 = ubuf[SUBLANES - 2:SUBLANES - 2 + tm, cs]
        y = bg * (u2 * cw_ref[0, :, cs] + u1 * cw_ref[1, :, cs] + u0 * cw_ref[2, :, cs])
        y_s[:, cs] = y.astype(BF16)
        ubuf[0:SUBLANES, cs] = ubuf[tm:tm + SUBLANES, cs]

    def gate_block(jb):
        cs = slice(jb * MIX_BLOCK, (jb + 1) * MIX_BLOCK)
        sg_s[:, cs] = jax.nn.sigmoid(_dot_nt(hb_s[...], w_gt[cs, :]) + b_gt[:, cs])

    def pconv_block(jb):
        cs = slice(jb * MIX_BLOCK, (jb + 1) * MIX_BLOCK)
        sg_s[:, cs] = sg_s[:, cs] * _dot(y_s[...], w_pc[:, cs])

    nb = CONV_DIM // MIX_BLOCK
    mxu_items = ([functools.partial(conv_block, jb) for jb in range(nb)]
                 + [functools.partial(gate_block, jb) for jb in range(2 * nb)]
                 + [functools.partial(pconv_block, jb) for jb in range(nb)])

    pairs = [(ci, hd) for ci in range(n_chunks) for hd in range(HEADS)]
    ahead = pair_weights(*pairs[0])
    for idx, (ci, hd) in enumerate(pairs):
        ready = ahead
        if idx + 1 < len(pairs):
            ahead = pair_weights(*pairs[idx + 1])
        mlstm_pair(ci, hd, *ready)
        lo = idx * len(mxu_items) // len(pairs)
        hi = (idx + 1) * len(mxu_items) // len(pairs)
        for item in mxu_items[lo:hi]:
            item()

    parts = []
    for hd in range(HEADS):
        hm_h = hm_s[:, hd * DV:(hd + 1) * DV]
        parts.append(_rms_scale(hm_h) * hg_ref[hd:hd + 1, :])
    hm = og_s[...] * jnp.concatenate(parts, axis=1)
    p_m = _dot(hm.astype(BF16), w_pm[...])
    z = sg_s[:, 0:D_MODEL] + sg_s[:, D_MODEL:2 * D_MODEL] * p_m
    o_ref[0] = x + gt1 * _dot(z.astype(BF16), w_o[...])


def _const_spec(shape):
    nd = len(shape)
    return pl.BlockSpec(shape, lambda b, j: (0,) * nd, pipeline_mode=pl.Buffered(1))


def _lane_window_spec(width, block):
    return pl.BlockSpec((1, width), lambda b, j: (0, block), pipeline_mode=pl.Buffered(1))


def _mixer_call(x, mod, g_norm, w_t, b_in, b_gt, conv_w, head_g, w_pc, w_pm, w_o):
    bsz, seq, d = x.shape
    tm = TM_MIX
    n_xbc, n_qk, n_vo, n_gt = 3 * CONV_DIM, 2 * HEADS * DQK, 2 * MLSTM_DIM, 2 * D_MODEL
    bias_windows = [_lane_window_spec(n_xbc, 0), _lane_window_spec(n_qk, n_xbc // n_qk),
                    _lane_window_spec(n_vo, (n_xbc + n_qk) // n_vo),
                    _const_spec(b_gt.shape),
                    _lane_window_spec(CHUNK, (n_xbc + n_qk + n_vo) // CHUNK)]
    hbm = pl.BlockSpec(memory_space=pl.ANY)
    return pl.pallas_call(
        _mixer_kernel,
        grid=(bsz, seq // tm),
        in_specs=[
            pl.BlockSpec((1, tm, d), lambda b, j: (b, j, 0)),
            _const_spec(mod.shape), _const_spec(g_norm.shape), hbm,
        ] + bias_windows + [_const_spec(conv_w.shape), _const_spec(head_g.shape), hbm, hbm, hbm],
        out_specs=pl.BlockSpec((1, tm, d), lambda b, j: (b, j, 0)),
        out_shape=jax.ShapeDtypeStruct(x.shape, F32),
        scratch_shapes=[
            pltpu.VMEM((n_xbc, d), BF16),
            pltpu.VMEM((n_qk, d), BF16),
            pltpu.VMEM((n_vo, d), BF16),
            pltpu.VMEM((n_gt, d), BF16),
            pltpu.VMEM((CHUNK, d), BF16),
            pltpu.VMEM((CONV_DIM, d), BF16),
            pltpu.VMEM((MLSTM_DIM, d), BF16),
            pltpu.VMEM((d, d), BF16),
            pltpu.VMEM((STAGE_SLOTS, STAGE_ROWS, d), F32),
            pltpu.SemaphoreType.DMA((STAGE_SLOTS,)),
            pltpu.VMEM((tm, D_MODEL), BF16),
            pltpu.VMEM((tm + SUBLANES, CONV_DIM), F32),
            pltpu.VMEM((tm, CONV_DIM), BF16),
            pltpu.VMEM((tm, 2 * D_MODEL), F32),
            pltpu.VMEM((tm, HEADS * DQK), F32),
            pltpu.VMEM((tm, HEADS * DQK), F32),
            pltpu.VMEM((tm, HEADS * DQK), BF16),
            pltpu.VMEM((tm, HEADS * DQK), BF16),
            pltpu.VMEM((tm, MLSTM_DIM), BF16),
            pltpu.VMEM((tm, MLSTM_DIM), F32),
            pltpu.VMEM((tm, MLSTM_DIM), F32),
            pltpu.VMEM((HEADS, DQK, DV), F32),
            pltpu.VMEM((SUBLANES, DQK), F32),
            pltpu.VMEM((SUBLANES, CHUNK), F32),
        ],
        compiler_params=pltpu.CompilerParams(
            dimension_semantics=("arbitrary", "arbitrary"),
            vmem_limit_bytes=VMEM_LIMIT_BYTES),
        name="mixer_sublayer",
    )(x, mod, g_norm, w_t, b_in, b_in, b_in, b_gt, b_in, conv_w, head_g, w_pc, w_pm, w_o)


def _ffn_kernel(x_ref, mod_ref, g_ref, w_up_hbm, cw_ref, w_d_hbm, gf_ref, o_ref,
                w_a, w_g, w_d, stage, stage_sem, abuf, *, final_norm):
    tm = TM_FFN
    j = pl.program_id(1)

    @pl.when(jnp.logical_and(pl.program_id(0) == 0, j == 0))
    def _():
        jobs = []
        for half, dst in enumerate((w_a, w_g)):
            for c in range(0, D_FF, FFN_STAGE_COLS):
                for r in range(0, D_MODEL, STAGE_ROWS):
                    jobs.append((w_up_hbm.at[pl.ds(r, STAGE_ROWS), pl.ds(half * D_FF + c, FFN_STAGE_COLS)],
                                 dst.at[pl.ds(r, STAGE_ROWS), pl.ds(c, FFN_STAGE_COLS)]))
        jobs += _row_jobs(w_d_hbm, 0, w_d, D_FF)
        _stage_to_bf16(jobs, stage, stage_sem)

    @pl.when(j == 0)
    def _():
        abuf[0:SUBLANES, :] = jnp.zeros((SUBLANES, D_FF), F32)

    x = x_ref[0]
    mod_row = pl.ds(pl.program_id(0), 1)
    sh2 = mod_ref[mod_row, 3 * D_MODEL:4 * D_MODEL]
    sc2 = mod_ref[mod_row, 4 * D_MODEL:5 * D_MODEL]
    gt2 = mod_ref[mod_row, 5 * D_MODEL:6 * D_MODEL]
    h = (_rms_scale(x) * g_ref[...]) * (1.0 + sc2) + sh2
    hb = h.astype(BF16)
    abuf[SUBLANES:SUBLANES + tm, :] = _dot(hb, w_a[...])
    gate = _dot(hb, w_g[...])
    a0 = abuf[SUBLANES:SUBLANES + tm, :]
    a1 = abuf[SUBLANES - 1:SUBLANES - 1 + tm, :]
    a2 = abuf[SUBLANES - 2:SUBLANES - 2 + tm, :]
    ac = a2 * cw_ref[0] + a1 * cw_ref[1] + a0 * cw_ref[2]
    abuf[0:SUBLANES, :] = abuf[tm:tm + SUBLANES, :]
    act = ((ac * jax.nn.sigmoid(ac)) * gate).astype(BF16)
    for r0 in range(0, tm, FFN_DOWN_ROWS):
        rows = slice(r0, r0 + FFN_DOWN_ROWS)
        x2 = x[rows] + gt2 * _dot(act[rows], w_d[...])
        o_ref[0, rows, :] = _rms_scale(x2) * gf_ref[...] if final_norm else x2


def _ffn_call(x, mod, g_norm, w_up, conv_w, w_d, g_final, final_norm):
    bsz, seq, d = x.shape
    tm = TM_FFN
    hbm = pl.BlockSpec(memory_space=pl.ANY)
    return pl.pallas_call(
        functools.partial(_ffn_kernel, final_norm=final_norm),
        grid=(bsz, seq // tm),
        in_specs=[
            pl.BlockSpec((1, tm, d), lambda b, j: (b, j, 0)),
            _const_spec(mod.shape), _const_spec(g_norm.shape), hbm, _const_spec(conv_w.shape), hbm,
            _const_spec(g_final.shape),
        ],
        out_specs=pl.BlockSpec((1, tm, d), lambda b, j: (b, j, 0)),
        out_shape=jax.ShapeDtypeStruct(x.shape, F32),
        scratch_shapes=[
            pltpu.VMEM((d, D_FF), BF16),
            pltpu.VMEM((d, D_FF), BF16),
            pltpu.VMEM((D_FF, d), BF16),
            pltpu.VMEM((STAGE_SLOTS, STAGE_ROWS, FFN_STAGE_COLS), F32),
            pltpu.SemaphoreType.DMA((STAGE_SLOTS,)),
            pltpu.VMEM((tm + SUBLANES, D_FF), F32),
        ],
        compiler_params=pltpu.CompilerParams(
            dimension_semantics=("arbitrary", "arbitrary"),
            vmem_limit_bytes=VMEM_LIMIT_BYTES),
        name="ffn_sublayer",
    )(x, mod, g_norm, w_up, conv_w, w_d, g_final)


def kernel(x, c, w_ada, b_ada, g_norm_mix, w_in, b_in, conv_mix_w, mlstm_head_g,
           w_proj_conv, w_proj_mlstm, w_out, g_norm_ffn, w_up, conv_ffn_w, w_down, g_final):
    depth = w_ada.shape[0]
    bsz = x.shape[0]
    assert bsz <= SUBLANES
    o_gt = 3 * CONV_DIM + 2 * HEADS * DQK + 2 * MLSTM_DIM + 2 * HEADS

    for l in range(depth):
        mod = _ada_call(c, w_ada[l], b_ada[l][None, :])

        bi = b_in[l]
        x = _mixer_call(
            x, mod, g_norm_mix[l][None, :],
            w_in[l].T, bi[None, :], bi[None, o_gt:],
            conv_mix_w[l][:, None, :], mlstm_head_g[l],
            w_proj_conv[l], w_proj_mlstm[l], w_out[l])

        x = _ffn_call(
            x, mod, g_norm_ffn[l][None, :], w_up[l],
            conv_ffn_w[l][:, None, :], w_down[l], g_final[None, :],
            final_norm=(l == depth - 1))
    return x
```

```python
import functools

import jax
import jax.numpy as jnp
from jax import lax
from jax.experimental import pallas as pl
from jax.experimental.pallas import tpu as pltpu

F32 = jnp.float32
BF16 = jnp.bfloat16

D_MODEL = 1024
CONV_DIM = 1024
HEADS = 4
DQK = 128
DV = 256
MLSTM_DIM = HEADS * DV
CHUNK = 128
D_FF = 2816
N_MOD = 6
EPS = 1e-6

SUBLANES = 8
TM_MIX = 512
TM_FFN = 512
MIX_BLOCK = 256
STAGE_SLOTS = 8
STAGE_ROWS = 256
FFN_STAGE_COLS = D_FF // 2
ADA_BLOCK = 3072
V7X_VMEM_BYTES = 64 * 1024 * 1024
VMEM_LIMIT_BYTES = V7X_VMEM_BYTES * 7 // 8


def _dot(a, b):
    return jnp.dot(a, b, preferred_element_type=F32)


def _dot_nt(a, b):
    return lax.dot_general(a, b, (((1,), (1,)), ((), ())), preferred_element_type=F32)


def _rms_scale(x):
    return x * lax.rsqrt(jnp.mean(x * x, axis=-1, keepdims=True) + EPS)


def _log_sigmoid(x):
    return -(jnp.maximum(-x, 0.0) + jnp.log1p(jnp.exp(-jnp.abs(x))))


def _stage_to_bf16(jobs, stage, sem):
    def copy(i):
        src = jobs[i][0]
        rows, cols = src.shape
        slot = i % STAGE_SLOTS
        return pltpu.make_async_copy(src, stage.at[slot, pl.ds(0, rows), pl.ds(0, cols)], sem.at[slot])

    for i in range(min(STAGE_SLOTS, len(jobs))):
        copy(i).start()
    for i, (src, dst) in enumerate(jobs):
        rows, cols = src.shape
        copy(i).wait()
        dst[...] = stage[i % STAGE_SLOTS, 0:rows, 0:cols].astype(BF16)
        if i + STAGE_SLOTS < len(jobs):
            copy(i + STAGE_SLOTS).start()


def _row_jobs(src, src_row0, dst, n_rows):
    step = min(STAGE_ROWS, n_rows)
    return [(src.at[pl.ds(src_row0 + r, step), :], dst.at[pl.ds(r, step), :])
            for r in range(0, n_rows, step)]


def _ada_kernel(c_ref, w_ref, b_ref, o_ref):
    c = c_ref[...]
    act = c * jax.nn.sigmoid(c)
    pad = jnp.zeros((SUBLANES - act.shape[0], act.shape[1]), F32)
    act = jnp.concatenate([act, pad], axis=0)
    o_ref[...] = _dot(act.astype(BF16), w_ref[...].astype(BF16)) + b_ref[...]


def _ada_call(c, w_ada, b_ada):
    n = w_ada.shape[1]
    tn = ADA_BLOCK
    return pl.pallas_call(
        _ada_kernel,
        grid=(n // tn,),
        in_specs=[
            pl.BlockSpec(c.shape, lambda i: (0, 0)),
            pl.BlockSpec((D_MODEL, tn), lambda i: (0, i)),
            pl.BlockSpec((1, tn), lambda i: (0, i)),
        ],
        out_specs=pl.BlockSpec((SUBLANES, tn), lambda i: (0, i)),
        out_shape=jax.ShapeDtypeStruct((SUBLANES, n), F32),
        compiler_params=pltpu.CompilerParams(dimension_semantics=("arbitrary",)),
        name="adaln_mod",
    )(c, w_ada, b_ada)


def _mixer_kernel(x_ref, mod_ref, g_ref, w_in_hbm, b_xbc, b_qk, b_vo, b_gt, b_if,
                  cw_ref, hg_ref, w_pc_hbm, w_pm_hbm, w_o_hbm,
                  o_ref,
                  w_xbc, w_qk, w_vo, w_gt, w_if, w_pc, w_pm, w_o, stage, stage_sem,
                  hb_s, ubuf, y_s, sg_s, qs_s, k_s, qb_s, kb_s, vb_s, og_s, hm_s, c_s, n_s, m_s):
    tm = TM_MIX
    n_chunks = tm // CHUNK
    j = pl.program_id(1)
    n_xbc, n_qk, n_vo, n_gt = 3 * CONV_DIM, 2 * HEADS * DQK, 2 * MLSTM_DIM, 2 * D_MODEL

    @pl.when(jnp.logical_and(pl.program_id(0) == 0, j == 0))
    def _():
        o_if = n_xbc + n_qk + n_vo
        jobs = (_row_jobs(w_in_hbm, o_if, w_if, CHUNK)
                + _row_jobs(w_in_hbm, n_xbc, w_qk, n_qk)
                + _row_jobs(w_in_hbm, n_xbc + n_qk, w_vo, n_vo)
                + _row_jobs(w_in_hbm, 0, w_xbc, n_xbc)
                + _row_jobs(w_in_hbm, o_if + 2 * HEADS, w_gt, n_gt)
                + _row_jobs(w_pc_hbm, 0, w_pc, CONV_DIM)
                + _row_jobs(w_pm_hbm, 0, w_pm, MLSTM_DIM)
                + _row_jobs(w_o_hbm, 0, w_o, D_MODEL))
        _stage_to_bf16(jobs, stage, stage_sem)

    @pl.when(j == 0)
    def _():
        ubuf[0:SUBLANES, :] = jnp.zeros((SUBLANES, CONV_DIM), F32)
        c_s[...] = jnp.zeros_like(c_s)
        n_s[...] = jnp.zeros_like(n_s)
        m_s[...] = jnp.zeros_like(m_s)

    x = x_ref[0]
    mod_row = pl.ds(pl.program_id(0), 1)
    sh1 = mod_ref[mod_row, 0:D_MODEL]
    sc1 = mod_ref[mod_row, D_MODEL:2 * D_MODEL]
    gt1 = mod_ref[mod_row, 2 * D_MODEL:3 * D_MODEL]
    h = (_rms_scale(x) * g_ref[...]) * (1.0 + sc1) + sh1
    hb_s[...] = h.astype(BF16)


    gif = _dot_nt(hb_s[...], w_if[...]) + b_if[...]
    lane8 = lax.broadcasted_iota(jnp.int32, (SUBLANES, CHUNK), 1)
    head_rows = lax.broadcasted_iota(jnp.int32, (SUBLANES, CHUNK), 0) < HEADS

    def lane_scan(v, op, fill):
        d = 1
        while d < CHUNK:
            v = op(v, jnp.where(lane8 >= d, pltpu.roll(v, d, axis=1), fill))
            d *= 2
        return v

    m_run = m_s[...]
    gate_rows = []
    for ci in range(n_chunks):
        gi = gif[ci * CHUNK:(ci + 1) * CHUNK, :].T[0:SUBLANES, :]
        li = jnp.where(head_rows, gi, 0.0)
        lf = jnp.where(head_rows, _log_sigmoid(pltpu.roll(gi, HEADS, axis=0)), 0.0)
        b = lane_scan(lf, jnp.add, 0.0)
        g = b[:, CHUNK - 1:CHUNK]
        r = li - b
        cm = lane_scan(r, jnp.maximum, -jnp.inf)
        r_max = cm[:, CHUNK - 1:CHUNK]
        m_prev = m_run[:, 0:1]
        big_m = jnp.maximum(m_prev, cm)
        m_loc = g + r_max
        m_new = jnp.maximum(g + m_prev, m_loc)
        gate_rows.append(dict(r=r, big_m=big_m, w=jnp.exp(r - r_max), em=jnp.exp(-(b + big_m)), m_prev=m_prev,
                              s_old=jnp.exp(g + m_prev - m_new), s_new=jnp.exp(m_loc - m_new)))
        m_run = jnp.broadcast_to(m_new, (SUBLANES, CHUNK))
    m_s[...] = m_run

    nq = HEADS * DQK
    qs = (_dot_nt(hb_s[...], w_qk[0:nq, :]) + b_qk[:, 0:nq]) * (DQK ** -0.5)
    qs_s[...] = qs
    qb_s[...] = qs.astype(BF16)
    k = _dot_nt(hb_s[...], w_qk[nq:2 * nq, :]) + b_qk[:, nq:2 * nq]
    k_s[...] = k
    kb_s[...] = k.astype(BF16)
    vb_s[...] = (_dot_nt(hb_s[...], w_vo[0:MLSTM_DIM, :]) + b_vo[:, 0:MLSTM_DIM]).astype(BF16)
    og_s[...] = jax.nn.sigmoid(_dot_nt(hb_s[...], w_vo[MLSTM_DIM:2 * MLSTM_DIM, :])
                               + b_vo[:, MLSTM_DIM:2 * MLSTM_DIM])

    row = lax.broadcasted_iota(jnp.int32, (CHUNK, CHUNK), 0)
    col = lax.broadcasted_iota(jnp.int32, (CHUNK, CHUNK), 1)
    tri = col <= row

    def token_major(row_vec):
        return jnp.broadcast_to(row_vec, (CHUNK, CHUNK)).T

    def pair_weights(ci, hd):
        rs = slice(ci * CHUNK, (ci + 1) * CHUNK)
        hq = slice(hd * DQK, (hd + 1) * DQK)
        gr = gate_rows[ci]
        m_bc = token_major(gr["big_m"][hd:hd + 1, :])
        w_bc = token_major(gr["w"][hd:hd + 1, :])
        r_row = gr["r"][hd:hd + 1, :]
        wts = jnp.where(tri, jnp.exp(r_row - m_bc), 0.0) * _dot_nt(qb_s[rs, hq], kb_s[rs, hq])
        kw = w_bc * k_s[rs, hq]
        return wts, kw.T.astype(BF16), jnp.sum(kw, axis=0, keepdims=True), m_bc

    def mlstm_pair(ci, hd, wts, kw_t, n_loc, m_bc):
        rs = slice(ci * CHUNK, (ci + 1) * CHUNK)
        hq = slice(hd * DQK, (hd + 1) * DQK)
        hv = slice(hd * DV, (hd + 1) * DV)
        gr = gate_rows[ci]
        em_bc = token_major(gr["em"][hd:hd + 1, :])
        si_bc = jnp.exp(gr["m_prev"][hd:hd + 1, :] - m_bc)
        s_old = gr["s_old"][hd:hd + 1, :]
        s_new = gr["s_new"][hd:hd + 1, :]
        n_prev = n_s[hd:hd + 1, :]
        c_prev = c_s[hd]
        qb = qb_s[rs, hq]
        vb = vb_s[rs, hv]

        num = (_dot(wts.astype(BF16), vb)
               + jnp.concatenate([si_bc, si_bc], axis=1) * _dot(qb, c_prev.astype(BF16)))
        den = (jnp.sum(wts, axis=1, keepdims=True)
               + si_bc * jnp.sum(qs_s[rs, hq] * n_prev, axis=1, keepdims=True))
        inv = 1.0 / jnp.maximum(jnp.abs(den), em_bc)
        hm_s[rs, hv] = num * jnp.concatenate([inv, inv], axis=1)

        c_loc = _dot(kw_t, vb)
        c_s[hd] = s_old * c_prev + s_new * c_loc
        n_s[hd:hd + 1, :] = s_old * n_prev + s_new * n_loc

    def conv_block(jb):
        lo = jb * MIX_BLOCK
        cs = slice(lo, lo + MIX_BLOCK)

        def proj(off):
            return (_dot_nt(hb_s[...], w_xbc[off + lo:off + lo + MIX_BLOCK, :])
                    + b_xbc[:, off + lo:off + lo + MIX_BLOCK])

        xin = proj(0)
        bg = proj(CONV_DIM)
        cg = proj(2 * CONV_DIM)
        ubuf[SUBLANES:SUBLANES + tm, cs] = cg * xin
        u0 = ubuf[SUBLANES:SUBLANES + tm, cs]
        u1 = ubuf[SUBLANES - 1:SUBLANES - 1 + tm, cs]
        u2 = ubuf[SUBLANES - 2:SUBLANES - 2 + tm, cs]
        y = bg * (u2 * cw_ref[0, :, cs] + u1 * cw_ref[1, :, cs] + u0 * cw_ref[2, :, cs])
        y_s[:, cs] = y.astype(BF16)
        ubuf[0:SUBLANES, cs] = ubuf[tm:tm + SUBLANES, cs]

    def gate_block(jb):
        cs = slice(jb * MIX_BLOCK, (jb + 1) * MIX_BLOCK)
        sg_s[:, cs] = jax.nn.sigmoid(_dot_nt(hb_s[...], w_gt[cs, :]) + b_gt[:, cs])

    def pconv_block(jb):
        cs = slice(jb * MIX_BLOCK, (jb + 1) * MIX_BLOCK)
        sg_s[:, cs] = sg_s[:, cs] * _dot(y_s[...], w_pc[:, cs])

    nb = CONV_DIM // MIX_BLOCK
    mxu_items = ([functools.partial(conv_block, jb) for jb in range(nb)]
                 + [functools.partial(gate_block, jb) for jb in range(2 * nb)]
                 + [functools.partial(pconv_block, jb) for jb in range(nb)])

    pairs = [(ci, hd) for ci in range(n_chunks) for hd in range(HEADS)]
    ahead = pair_weights(*pairs[0])
    for idx, (ci, hd) in enumerate(pairs):
        ready = ahead
        if idx + 1 < len(pairs):
            ahead = pair_weights(*pairs[idx + 1])
        mlstm_pair(ci, hd, *ready)
        lo = idx * len(mxu_items) // len(pairs)
        hi = (idx + 1) * len(mxu_items) // len(pairs)
        for item in mxu_items[lo:hi]:
            item()

    parts = []
    for hd in range(HEADS):
        hm_h = hm_s[:, hd * DV:(hd + 1) * DV]
        parts.append(_rms_scale(hm_h) * hg_ref[hd:hd + 1, :])
    hm = og_s[...] * jnp.concatenate(parts, axis=1)
    p_m = _dot(hm.astype(BF16), w_pm[...])
    z = sg_s[:, 0:D_MODEL] + sg_s[:, D_MODEL:2 * D_MODEL] * p_m
    o_ref[0] = x + gt1 * _dot(z.astype(BF16), w_o[...])


def _const_spec(shape):
    nd = len(shape)
    return pl.BlockSpec(shape, lambda b, j: (0,) * nd, pipeline_mode=pl.Buffered(1))


def _lane_window_spec(width, block):
    return pl.BlockSpec((1, width), lambda b, j: (0, block), pipeline_mode=pl.Buffered(1))


def _mixer_call(x, mod, g_norm, w_t, b_in, b_gt, conv_w, head_g, w_pc, w_pm, w_o):
    bsz, seq, d = x.shape
    tm = TM_MIX
    n_xbc, n_qk, n_vo, n_gt = 3 * CONV_DIM, 2 * HEADS * DQK, 2 * MLSTM_DIM, 2 * D_MODEL
    bias_windows = [_lane_window_spec(n_xbc, 0), _lane_window_spec(n_qk, n_xbc // n_qk),
                    _lane_window_spec(n_vo, (n_xbc + n_qk) // n_vo),
                    _const_spec(b_gt.shape),
                    _lane_window_spec(CHUNK, (n_xbc + n_qk + n_vo) // CHUNK)]
    hbm = pl.BlockSpec(memory_space=pl.ANY)
    return pl.pallas_call(
        _mixer_kernel,
        grid=(bsz, seq // tm),
        in_specs=[
            pl.BlockSpec((1, tm, d), lambda b, j: (b, j, 0)),
            _const_spec(mod.shape), _const_spec(g_norm.shape), hbm,
        ] + bias_windows + [_const_spec(conv_w.shape), _const_spec(head_g.shape), hbm, hbm, hbm],
        out_specs=pl.BlockSpec((1, tm, d), lambda b, j: (b, j, 0)),
        out_shape=jax.ShapeDtypeStruct(x.shape, F32),
        scratch_shapes=[
            pltpu.VMEM((n_xbc, d), BF16),
            pltpu.VMEM((n_qk, d), BF16),
            pltpu.VMEM((n_vo, d), BF16),
            pltpu.VMEM((n_gt, d), BF16),
            pltpu.VMEM((CHUNK, d), BF16),
            pltpu.VMEM((CONV_DIM, d), BF16),
            pltpu.VMEM((MLSTM_DIM, d), BF16),
            pltpu.VMEM((d, d), BF16),
            pltpu.VMEM((STAGE_SLOTS, STAGE_ROWS, d), F32),
            pltpu.SemaphoreType.DMA((STAGE_SLOTS,)),
            pltpu.VMEM((tm, D_MODEL), BF16),
            pltpu.VMEM((tm + SUBLANES, CONV_DIM), F32),
            pltpu.VMEM((tm, CONV_DIM), BF16),
            pltpu.VMEM((tm, 2 * D_MODEL), F32),
            pltpu.VMEM((tm, HEADS * DQK), F32),
            pltpu.VMEM((tm, HEADS * DQK), F32),
            pltpu.VMEM((tm, HEADS * DQK), BF16),
            pltpu.VMEM((tm, HEADS * DQK), BF16),
            pltpu.VMEM((tm, MLSTM_DIM), BF16),
            pltpu.VMEM((tm, MLSTM_DIM), F32),
            pltpu.VMEM((tm, MLSTM_DIM), F32),
            pltpu.VMEM((HEADS, DQK, DV), F32),
            pltpu.VMEM((SUBLANES, DQK), F32),
            pltpu.VMEM((SUBLANES, CHUNK), F32),
        ],
        compiler_params=pltpu.CompilerParams(
            dimension_semantics=("arbitrary", "arbitrary"),
            vmem_limit_bytes=VMEM_LIMIT_BYTES),
        name="mixer_sublayer",
    )(x, mod, g_norm, w_t, b_in, b_in, b_in, b_gt, b_in, conv_w, head_g, w_pc, w_pm, w_o)


def _ffn_kernel(x_ref, mod_ref, g_ref, w_up_hbm, cw_ref, w_d_hbm, gf_ref, o_ref,
                w_a, w_g, w_d, stage, stage_sem, abuf, *, final_norm):
    tm = TM_FFN
    j = pl.program_id(1)

    @pl.when(jnp.logical_and(pl.program_id(0) == 0, j == 0))
    def _():
        jobs = []
        for half, dst in enumerate((w_a, w_g)):
            for c in range(0, D_FF, FFN_STAGE_COLS):
                for r in range(0, D_MODEL, STAGE_ROWS):
                    jobs.append((w_up_hbm.at[pl.ds(r, STAGE_ROWS), pl.ds(half * D_FF + c, FFN_STAGE_COLS)],
                                 dst.at[pl.ds(r, STAGE_ROWS), pl.ds(c, FFN_STAGE_COLS)]))
        jobs += _row_jobs(w_d_hbm, 0, w_d, D_FF)
        _stage_to_bf16(jobs, stage, stage_sem)

    @pl.when(j == 0)
    def _():
        abuf[0:SUBLANES, :] = jnp.zeros((SUBLANES, D_FF), F32)

    x = x_ref[0]
    mod_row = pl.ds(pl.program_id(0), 1)
    sh2 = mod_ref[mod_row, 3 * D_MODEL:4 * D_MODEL]
    sc2 = mod_ref[mod_row, 4 * D_MODEL:5 * D_MODEL]
    gt2 = mod_ref[mod_row, 5 * D_MODEL:6 * D_MODEL]
    h = (_rms_scale(x) * g_ref[...]) * (1.0 + sc2) + sh2
    hb = h.astype(BF16)
    abuf[SUBLANES:SUBLANES + tm, :] = _dot(hb, w_a[...])
    gate = _dot(hb, w_g[...])
    a0 = abuf[SUBLANES:SUBLANES + tm, :]
    a1 = abuf[SUBLANES - 1:SUBLANES - 1 + tm, :]
    a2 = abuf[SUBLANES - 2:SUBLANES - 2 + tm, :]
    ac = a2 * cw_ref[0] + a1 * cw_ref[1] + a0 * cw_ref[2]
    abuf[0:SUBLANES, :] = abuf[tm:tm + SUBLANES, :]
    act = (ac * jax.nn.sigmoid(ac)) * gate
    x2 = x + gt2 * _dot(act.astype(BF16), w_d[...])
    o_ref[0] = _rms_scale(x2) * gf_ref[...] if final_norm else x2


def _ffn_call(x, mod, g_norm, w_up, conv_w, w_d, g_final, final_norm):
    bsz, seq, d = x.shape
    tm = TM_FFN
    hbm = pl.BlockSpec(memory_space=pl.ANY)
    return pl.pallas_call(
        functools.partial(_ffn_kernel, final_norm=final_norm),
        grid=(bsz, seq // tm),
        in_specs=[
            pl.BlockSpec((1, tm, d), lambda b, j: (b, j, 0)),
            _const_spec(mod.shape), _const_spec(g_norm.shape), hbm, _const_spec(conv_w.shape), hbm,
            _const_spec(g_final.shape),
        ],
        out_specs=pl.BlockSpec((1, tm, d), lambda b, j: (b, j, 0)),
        out_shape=jax.ShapeDtypeStruct(x.shape, F32),
        scratch_shapes=[
            pltpu.VMEM((d, D_FF), BF16),
            pltpu.VMEM((d, D_FF), BF16),
            pltpu.VMEM((D_FF, d), BF16),
            pltpu.VMEM((STAGE_SLOTS, STAGE_ROWS, FFN_STAGE_COLS), F32),
            pltpu.SemaphoreType.DMA((STAGE_SLOTS,)),
            pltpu.VMEM((tm + SUBLANES, D_FF), F32),
        ],
        compiler_params=pltpu.CompilerParams(
            dimension_semantics=("arbitrary", "arbitrary"),
            vmem_limit_bytes=VMEM_LIMIT_BYTES),
        name="ffn_sublayer",
    )(x, mod, g_norm, w_up, conv_w, w_d, g_final)


def kernel(x, c, w_ada, b_ada, g_norm_mix, w_in, b_in, conv_mix_w, mlstm_head_g,
           w_proj_conv, w_proj_mlstm, w_out, g_norm_ffn, w_up, conv_ffn_w, w_down, g_final):
    depth = w_ada.shape[0]
    bsz = x.shape[0]
    assert bsz <= SUBLANES
    o_gt = 3 * CONV_DIM + 2 * HEADS * DQK + 2 * MLSTM_DIM + 2 * HEADS

    for l in range(depth):
        mod = _ada_call(c, w_ada[l], b_ada[l][None, :])

        bi = b_in[l]
        x = _mixer_call(
            x, mod, g_norm_mix[l][None, :],
            w_in[l].T, bi[None, :], bi[None, o_gt:],
            conv_mix_w[l][:, None, :], mlstm_head_g[l],
            w_proj_conv[l], w_proj_mlstm[l], w_out[l])

        x = _ffn_call(
            x, mod, g_norm_ffn[l][None, :], w_up[l],
            conv_ffn_w[l][:, None, :], w_down[l], g_final[None, :],
            final_norm=(l == depth - 1))
    return x
```

```python
import functools

import jax
import jax.numpy as jnp
from jax import lax
from jax.experimental import pallas as pl
from jax.experimental.pallas import tpu as pltpu

F32 = jnp.float32
BF16 = jnp.bfloat16

D_MODEL = 1024
CONV_DIM = 1024
HEADS = 4
DQK = 128
DV = 256
MLSTM_DIM = HEADS * DV
CHUNK = 128
D_FF = 2816
N_MOD = 6
EPS = 1e-6

SUBLANES = 8
TM_MIX = 512
TM_FFN = 512
MIX_BLOCK = 256
STAGE_SLOTS = 8
STAGE_ROWS = 256
FFN_STAGE_COLS = D_FF // 2
ADA_BLOCK = 3072
V7X_VMEM_BYTES = 64 * 1024 * 1024
VMEM_LIMIT_BYTES = V7X_VMEM_BYTES * 7 // 8


def _dot(a, b):
    return jnp.dot(a, b, preferred_element_type=F32)


def _dot_nt(a, b):
    return lax.dot_general(a, b, (((1,), (1,)), ((), ())), preferred_element_type=F32)


def _rms_scale(x):
    return x * lax.rsqrt(jnp.mean(x * x, axis=-1, keepdims=True) + EPS)


def _log_sigmoid(x):
    return -(jnp.maximum(-x, 0.0) + jnp.log1p(jnp.exp(-jnp.abs(x))))


def _stage_to_bf16(jobs, stage, sem):
    def copy(i):
        src = jobs[i][0]
        rows, cols = src.shape
        slot = i % STAGE_SLOTS
        return pltpu.make_async_copy(src, stage.at[slot, pl.ds(0, rows), pl.ds(0, cols)], sem.at[slot])

    for i in range(min(STAGE_SLOTS, len(jobs))):
        copy(i).start()
    for i, (src, dst) in enumerate(jobs):
        rows, cols = src.shape
        copy(i).wait()
        chunk = stage[i % STAGE_SLOTS, 0:rows, 0:cols]
        if dst.shape != src.shape:
            assert dst.shape == (cols, rows)
            chunk = chunk.T
        dst[...] = chunk.astype(BF16)
        if i + STAGE_SLOTS < len(jobs):
            copy(i + STAGE_SLOTS).start()


def _row_jobs(src, src_row0, dst, n_rows):
    step = min(STAGE_ROWS, n_rows)
    return [(src.at[pl.ds(src_row0 + r, step), :], dst.at[pl.ds(r, step), :])
            for r in range(0, n_rows, step)]


def _transposed_jobs(src, src_row0, dst, n_rows):
    step = min(STAGE_ROWS, n_rows)
    return [(src.at[pl.ds(src_row0 + r, step), :], dst.at[:, pl.ds(r, step)])
            for r in range(0, n_rows, step)]


def _ada_kernel(c_ref, w_ref, b_ref, o_ref):
    c = c_ref[...]
    act = c * jax.nn.sigmoid(c)
    pad = jnp.zeros((SUBLANES - act.shape[0], act.shape[1]), F32)
    act = jnp.concatenate([act, pad], axis=0)
    o_ref[...] = _dot(act.astype(BF16), w_ref[...].astype(BF16)) + b_ref[...]


def _ada_call(c, w_ada, b_ada):
    n = w_ada.shape[1]
    tn = ADA_BLOCK
    return pl.pallas_call(
        _ada_kernel,
        grid=(n // tn,),
        in_specs=[
            pl.BlockSpec(c.shape, lambda i: (0, 0)),
            pl.BlockSpec((D_MODEL, tn), lambda i: (0, i)),
            pl.BlockSpec((1, tn), lambda i: (0, i)),
        ],
        out_specs=pl.BlockSpec((SUBLANES, tn), lambda i: (0, i)),
        out_shape=jax.ShapeDtypeStruct((SUBLANES, n), F32),
        compiler_params=pltpu.CompilerParams(dimension_semantics=("arbitrary",)),
        name="adaln_mod",
    )(c, w_ada, b_ada)


def _mixer_kernel(x_ref, mod_ref, g_ref, w_in_hbm, b_xbc, b_qk, b_vo, b_gt, b_if,
                  cw_ref, hg_ref, w_pc_hbm, w_pm_hbm, w_o_hbm,
                  o_ref,
                  w_xbc, w_qk, w_vo, w_gt, w_if, w_pc, w_pm, w_o, stage, stage_sem,
                  hb_s, ubuf, y_s, sg_s, qs_s, k_s, qb_s, kb_s, vb_s, og_s, hm_s, c_s, n_s, m_s):
    tm = TM_MIX
    n_chunks = tm // CHUNK
    j = pl.program_id(1)
    n_xbc, n_qk, n_vo, n_gt = 3 * CONV_DIM, 2 * HEADS * DQK, 2 * MLSTM_DIM, 2 * D_MODEL

    @pl.when(jnp.logical_and(pl.program_id(0) == 0, j == 0))
    def _():
        o_if = n_xbc + n_qk + n_vo
        jobs = (_transposed_jobs(w_in_hbm, o_if, w_if, CHUNK)
                + _transposed_jobs(w_in_hbm, n_xbc, w_qk, n_qk)
                + _transposed_jobs(w_in_hbm, n_xbc + n_qk, w_vo, n_vo)
                + _transposed_jobs(w_in_hbm, 0, w_xbc, n_xbc)
                + _transposed_jobs(w_in_hbm, o_if + 2 * HEADS, w_gt, n_gt)
                + _row_jobs(w_pc_hbm, 0, w_pc, CONV_DIM)
                + _row_jobs(w_pm_hbm, 0, w_pm, MLSTM_DIM)
                + _row_jobs(w_o_hbm, 0, w_o, D_MODEL))
        _stage_to_bf16(jobs, stage, stage_sem)

    @pl.when(j == 0)
    def _():
        ubuf[0:SUBLANES, :] = jnp.zeros((SUBLANES, CONV_DIM), F32)
        c_s[...] = jnp.zeros_like(c_s)
        n_s[...] = jnp.zeros_like(n_s)
        m_s[...] = jnp.zeros_like(m_s)

    x = x_ref[0]
    mod_row = pl.ds(pl.program_id(0), 1)
    sh1 = mod_ref[mod_row, 0:D_MODEL]
    sc1 = mod_ref[mod_row, D_MODEL:2 * D_MODEL]
    gt1 = mod_ref[mod_row, 2 * D_MODEL:3 * D_MODEL]
    h = (_rms_scale(x) * g_ref[...]) * (1.0 + sc1) + sh1
    hb_s[...] = h.astype(BF16)


    gif = _dot(hb_s[...], w_if[...]) + b_if[...]
    lane8 = lax.broadcasted_iota(jnp.int32, (SUBLANES, CHUNK), 1)
    head_rows = lax.broadcasted_iota(jnp.int32, (SUBLANES, CHUNK), 0) < HEADS

    def lane_scan(v, op, fill):
        d = 1
        while d < CHUNK:
            v = op(v, jnp.where(lane8 >= d, pltpu.roll(v, d, axis=1), fill))
            d *= 2
        return v

    m_run = m_s[...]
    gate_rows = []
    for ci in range(n_chunks):
        gi = gif[ci * CHUNK:(ci + 1) * CHUNK, :].T[0:SUBLANES, :]
        li = jnp.where(head_rows, gi, 0.0)
        lf = jnp.where(head_rows, _log_sigmoid(pltpu.roll(gi, HEADS, axis=0)), 0.0)
        b = lane_scan(lf, jnp.add, 0.0)
        g = b[:, CHUNK - 1:CHUNK]
        r = li - b
        cm = lane_scan(r, jnp.maximum, -jnp.inf)
        r_max = cm[:, CHUNK - 1:CHUNK]
        m_prev = m_run[:, 0:1]
        big_m = jnp.maximum(m_prev, cm)
        m_loc = g + r_max
        m_new = jnp.maximum(g + m_prev, m_loc)
        gate_rows.append(dict(r=r, big_m=big_m, w=jnp.exp(r - r_max), em=jnp.exp(-(b + big_m)), m_prev=m_prev,
                              s_old=jnp.exp(g + m_prev - m_new), s_new=jnp.exp(m_loc - m_new)))
        m_run = jnp.broadcast_to(m_new, (SUBLANES, CHUNK))
    m_s[...] = m_run

    nq = HEADS * DQK
    qs = (_dot(hb_s[...], w_qk[:, 0:nq]) + b_qk[:, 0:nq]) * (DQK ** -0.5)
    qs_s[...] = qs
    qb_s[...] = qs.astype(BF16)
    k = _dot(hb_s[...], w_qk[:, nq:2 * nq]) + b_qk[:, nq:2 * nq]
    k_s[...] = k
    kb_s[...] = k.astype(BF16)
    vb_s[...] = (_dot(hb_s[...], w_vo[:, 0:MLSTM_DIM]) + b_vo[:, 0:MLSTM_DIM]).astype(BF16)
    og_s[...] = jax.nn.sigmoid(_dot(hb_s[...], w_vo[:, MLSTM_DIM:2 * MLSTM_DIM])
                               + b_vo[:, MLSTM_DIM:2 * MLSTM_DIM])

    row = lax.broadcasted_iota(jnp.int32, (CHUNK, CHUNK), 0)
    col = lax.broadcasted_iota(jnp.int32, (CHUNK, CHUNK), 1)
    tri = col <= row

    def token_major(row_vec):
        return jnp.broadcast_to(row_vec, (CHUNK, CHUNK)).T

    def pair_weights(ci, hd):
        rs = slice(ci * CHUNK, (ci + 1) * CHUNK)
        hq = slice(hd * DQK, (hd + 1) * DQK)
        gr = gate_rows[ci]
        m_bc = token_major(gr["big_m"][hd:hd + 1, :])
        w_bc = token_major(gr["w"][hd:hd + 1, :])
        r_row = gr["r"][hd:hd + 1, :]
        wts = jnp.where(tri, jnp.exp(r_row - m_bc), 0.0) * _dot_nt(qb_s[rs, hq], kb_s[rs, hq])
        kw = w_bc * k_s[rs, hq]
        return wts, kw.T.astype(BF16), jnp.sum(kw, axis=0, keepdims=True), m_bc

    def mlstm_pair(ci, hd, wts, kw_t, n_loc, m_bc):
        rs = slice(ci * CHUNK, (ci + 1) * CHUNK)
        hq = slice(hd * DQK, (hd + 1) * DQK)
        hv = slice(hd * DV, (hd + 1) * DV)
        gr = gate_rows[ci]
        em_bc = token_major(gr["em"][hd:hd + 1, :])
        si_bc = jnp.exp(gr["m_prev"][hd:hd + 1, :] - m_bc)
        s_old = gr["s_old"][hd:hd + 1, :]
        s_new = gr["s_new"][hd:hd + 1, :]
        n_prev = n_s[hd:hd + 1, :]
        c_prev = c_s[hd]
        qb = qb_s[rs, hq]
        vb = vb_s[rs, hv]

        num = (_dot(wts.astype(BF16), vb)
               + jnp.concatenate([si_bc, si_bc], axis=1) * _dot(qb, c_prev.astype(BF16)))
        den = (jnp.sum(wts, axis=1, keepdims=True)
               + si_bc * jnp.sum(qs_s[rs, hq] * n_prev, axis=1, keepdims=True))
        inv = 1.0 / jnp.maximum(jnp.abs(den), em_bc)
        hm_s[rs, hv] = num * jnp.concatenate([inv, inv], axis=1)

        c_loc = _dot(kw_t, vb)
        c_s[hd] = s_old * c_prev + s_new * c_loc
        n_s[hd:hd + 1, :] = s_old * n_prev + s_new * n_loc

    def conv_block(jb):
        lo = jb * MIX_BLOCK
        cs = slice(lo, lo + MIX_BLOCK)

        def proj(off):
            return (_dot(hb_s[...], w_xbc[:, off + lo:off + lo + MIX_BLOCK])
                    + b_xbc[:, off + lo:off + lo + MIX_BLOCK])

        xin = proj(0)
        bg = proj(CONV_DIM)
        cg = proj(2 * CONV_DIM)
        ubuf[SUBLANES:SUBLANES + tm, cs] = cg * xin
        u0 = ubuf[SUBLANES:SUBLANES + tm, cs]
        u1 = ubuf[SUBLANES - 1:SUBLANES - 1 + tm, cs]
        u2 = ubuf[SUBLANES - 2:SUBLANES - 2 + tm, cs]
        y = bg * (u2 * cw_ref[0, :, cs] + u1 * cw_ref[1, :, cs] + u0 * cw_ref[2, :, cs])
        y_s[:, cs] = y.astype(BF16)
        ubuf[0:SUBLANES, cs] = ubuf[tm:tm + SUBLANES, cs]

    def gate_block(jb):
        cs = slice(jb * MIX_BLOCK, (jb + 1) * MIX_BLOCK)
        sg_s[:, cs] = jax.nn.sigmoid(_dot(hb_s[...], w_gt[:, cs]) + b_gt[:, cs])

    def pconv_block(jb):
        cs = slice(jb * MIX_BLOCK, (jb + 1) * MIX_BLOCK)
        sg_s[:, cs] = sg_s[:, cs] * _dot(y_s[...], w_pc[:, cs])

    nb = CONV_DIM // MIX_BLOCK
    mxu_items = ([functools.partial(conv_block, jb) for jb in range(nb)]
                 + [functools.partial(gate_block, jb) for jb in range(2 * nb)]
                 + [functools.partial(pconv_block, jb) for jb in range(nb)])

    pairs = [(ci, hd) for ci in range(n_chunks) for hd in range(HEADS)]
    ahead = pair_weights(*pairs[0])
    for idx, (ci, hd) in enumerate(pairs):
        ready = ahead
        if idx + 1 < len(pairs):
            ahead = pair_weights(*pairs[idx + 1])
        mlstm_pair(ci, hd, *ready)
        lo = idx * len(mxu_items) // len(pairs)
        hi = (idx + 1) * len(mxu_items) // len(pairs)
        for item in mxu_items[lo:hi]:
            item()

    parts = []
    for hd in range(HEADS):
        hm_h = hm_s[:, hd * DV:(hd + 1) * DV]
        parts.append(_rms_scale(hm_h) * hg_ref[hd:hd + 1, :])
    hm = og_s[...] * jnp.concatenate(parts, axis=1)
    p_m = _dot(hm.astype(BF16), w_pm[...])
    z = sg_s[:, 0:D_MODEL] + sg_s[:, D_MODEL:2 * D_MODEL] * p_m
    o_ref[0] = x + gt1 * _dot(z.astype(BF16), w_o[...])


def _const_spec(shape):
    nd = len(shape)
    return pl.BlockSpec(shape, lambda b, j: (0,) * nd, pipeline_mode=pl.Buffered(1))


def _lane_window_spec(width, block):
    return pl.BlockSpec((1, width), lambda b, j: (0, block), pipeline_mode=pl.Buffered(1))


def _mixer_call(x, mod, g_norm, w_t, b_in, b_gt, conv_w, head_g, w_pc, w_pm, w_o):
    bsz, seq, d = x.shape
    tm = TM_MIX
    n_xbc, n_qk, n_vo, n_gt = 3 * CONV_DIM, 2 * HEADS * DQK, 2 * MLSTM_DIM, 2 * D_MODEL
    bias_windows = [_lane_window_spec(n_xbc, 0), _lane_window_spec(n_qk, n_xbc // n_qk),
                    _lane_window_spec(n_vo, (n_xbc + n_qk) // n_vo),
                    _const_spec(b_gt.shape),
                    _lane_window_spec(CHUNK, (n_xbc + n_qk + n_vo) // CHUNK)]
    hbm = pl.BlockSpec(memory_space=pl.ANY)
    return pl.pallas_call(
        _mixer_kernel,
        grid=(bsz, seq // tm),
        in_specs=[
            pl.BlockSpec((1, tm, d), lambda b, j: (b, j, 0)),
            _const_spec(mod.shape), _const_spec(g_norm.shape), hbm,
        ] + bias_windows + [_const_spec(conv_w.shape), _const_spec(head_g.shape), hbm, hbm, hbm],
        out_specs=pl.BlockSpec((1, tm, d), lambda b, j: (b, j, 0)),
        out_shape=jax.ShapeDtypeStruct(x.shape, F32),
        scratch_shapes=[
            pltpu.VMEM((d, n_xbc), BF16),
            pltpu.VMEM((d, n_qk), BF16),
            pltpu.VMEM((d, n_vo), BF16),
            pltpu.VMEM((d, n_gt), BF16),
            pltpu.VMEM((d, CHUNK), BF16),
            pltpu.VMEM((CONV_DIM, d), BF16),
            pltpu.VMEM((MLSTM_DIM, d), BF16),
            pltpu.VMEM((d, d), BF16),
            pltpu.VMEM((STAGE_SLOTS, STAGE_ROWS, d), F32),
            pltpu.SemaphoreType.DMA((STAGE_SLOTS,)),
            pltpu.VMEM((tm, D_MODEL), BF16),
            pltpu.VMEM((tm + SUBLANES, CONV_DIM), F32),
            pltpu.VMEM((tm, CONV_DIM), BF16),
            pltpu.VMEM((tm, 2 * D_MODEL), F32),
            pltpu.VMEM((tm, HEADS * DQK), F32),
            pltpu.VMEM((tm, HEADS * DQK), F32),
            pltpu.VMEM((tm, HEADS * DQK), BF16),
            pltpu.VMEM((tm, HEADS * DQK), BF16),
            pltpu.VMEM((tm, MLSTM_DIM), BF16),
            pltpu.VMEM((tm, MLSTM_DIM), F32),
            pltpu.VMEM((tm, MLSTM_DIM), F32),
            pltpu.VMEM((HEADS, DQK, DV), F32),
            pltpu.VMEM((SUBLANES, DQK), F32),
            pltpu.VMEM((SUBLANES, CHUNK), F32),
        ],
        compiler_params=pltpu.CompilerParams(
            dimension_semantics=("arbitrary", "arbitrary"),
            vmem_limit_bytes=VMEM_LIMIT_BYTES),
        name="mixer_sublayer",
    )(x, mod, g_norm, w_t, b_in, b_in, b_in, b_gt, b_in, conv_w, head_g, w_pc, w_pm, w_o)


def _ffn_kernel(x_ref, mod_ref, g_ref, w_up_hbm, cw_ref, w_d_hbm, gf_ref, o_ref,
                w_a, w_g, w_d, stage, stage_sem, abuf, *, final_norm):
    tm = TM_FFN
    j = pl.program_id(1)

    @pl.when(jnp.logical_and(pl.program_id(0) == 0, j == 0))
    def _():
        jobs = []
        for half, dst in enumerate((w_a, w_g)):
            for c in range(0, D_FF, FFN_STAGE_COLS):
                for r in range(0, D_MODEL, STAGE_ROWS):
                    jobs.append((w_up_hbm.at[pl.ds(r, STAGE_ROWS), pl.ds(half * D_FF + c, FFN_STAGE_COLS)],
                                 dst.at[pl.ds(r, STAGE_ROWS), pl.ds(c, FFN_STAGE_COLS)]))
        jobs += _row_jobs(w_d_hbm, 0, w_d, D_FF)
        _stage_to_bf16(jobs, stage, stage_sem)

    @pl.when(j == 0)
    def _():
        abuf[0:SUBLANES, :] = jnp.zeros((SUBLANES, D_FF), F32)

    x = x_ref[0]
    mod_row = pl.ds(pl.program_id(0), 1)
    sh2 = mod_ref[mod_row, 3 * D_MODEL:4 * D_MODEL]
    sc2 = mod_ref[mod_row, 4 * D_MODEL:5 * D_MODEL]
    gt2 = mod_ref[mod_row, 5 * D_MODEL:6 * D_MODEL]
    h = (_rms_scale(x) * g_ref[...]) * (1.0 + sc2) + sh2
    hb = h.astype(BF16)
    abuf[SUBLANES:SUBLANES + tm, :] = _dot(hb, w_a[...])
    gate = _dot(hb, w_g[...])
    a0 = abuf[SUBLANES:SUBLANES + tm, :]
    a1 = abuf[SUBLANES - 1:SUBLANES - 1 + tm, :]
    a2 = abuf[SUBLANES - 2:SUBLANES - 2 + tm, :]
    ac = a2 * cw_ref[0] + a1 * cw_ref[1] + a0 * cw_ref[2]
    abuf[0:SUBLANES, :] = abuf[tm:tm + SUBLANES, :]
    act = (ac * jax.nn.sigmoid(ac)) * gate
    x2 = x + gt2 * _dot(act.astype(BF16), w_d[...])
    o_ref[0] = _rms_scale(x2) * gf_ref[...] if final_norm else x2


def _ffn_call(x, mod, g_norm, w_up, conv_w, w_d, g_final, final_norm):
    bsz, seq, d = x.shape
    tm = TM_FFN
    hbm = pl.BlockSpec(memory_space=pl.ANY)
    return pl.pallas_call(
        functools.partial(_ffn_kernel, final_norm=final_norm),
        grid=(bsz, seq // tm),
        in_specs=[
            pl.BlockSpec((1, tm, d), lambda b, j: (b, j, 0)),
            _const_spec(mod.shape), _const_spec(g_norm.shape), hbm, _const_spec(conv_w.shape), hbm,
            _const_spec(g_final.shape),
        ],
        out_specs=pl.BlockSpec((1, tm, d), lambda b, j: (b, j, 0)),
        out_shape=jax.ShapeDtypeStruct(x.shape, F32),
        scratch_shapes=[
            pltpu.VMEM((d, D_FF), BF16),
            pltpu.VMEM((d, D_FF), BF16),
            pltpu.VMEM((D_FF, d), BF16),
            pltpu.VMEM((STAGE_SLOTS, STAGE_ROWS, FFN_STAGE_COLS), F32),
            pltpu.SemaphoreType.DMA((STAGE_SLOTS,)),
            pltpu.VMEM((tm + SUBLANES, D_FF), F32),
        ],
        compiler_params=pltpu.CompilerParams(
            dimension_semantics=("arbitrary", "arbitrary"),
            vmem_limit_bytes=VMEM_LIMIT_BYTES),
        name="ffn_sublayer",
    )(x, mod, g_norm, w_up, conv_w, w_d, g_final)


def kernel(x, c, w_ada, b_ada, g_norm_mix, w_in, b_in, conv_mix_w, mlstm_head_g,
           w_proj_conv, w_proj_mlstm, w_out, g_norm_ffn, w_up, conv_ffn_w, w_down, g_final):
    depth = w_ada.shape[0]
    bsz = x.shape[0]
    assert bsz <= SUBLANES
    o_gt = 3 * CONV_DIM + 2 * HEADS * DQK + 2 * MLSTM_DIM + 2 * HEADS

    for l in range(depth):
        mod = _ada_call(c, w_ada[l], b_ada[l][None, :])

        bi = b_in[l]
        x = _mixer_call(
            x, mod, g_norm_mix[l][None, :],
            w_in[l].T, bi[None, :], bi[None, o_gt:],
            conv_mix_w[l][:, None, :], mlstm_head_g[l],
            w_proj_conv[l], w_proj_mlstm[l], w_out[l])

        x = _ffn_call(
            x, mod, g_norm_ffn[l][None, :], w_up[l],
            conv_ffn_w[l][:, None, :], w_down[l], g_final[None, :],
            final_norm=(l == depth - 1))
    return x
```

```python
import functools

import jax
import jax.numpy as jnp
from jax import lax
from jax.experimental import pallas as pl
from jax.experimental.pallas import tpu as pltpu

F32 = jnp.float32
BF16 = jnp.bfloat16

D_MODEL = 1024
CONV_DIM = 1024
HEADS = 4
DQK = 128
DV = 256
MLSTM_DIM = HEADS * DV
CHUNK = 128
D_FF = 2816
N_MOD = 6
EPS = 1e-6

SUBLANES = 8
TM_MIX = 512
TM_FFN = 512
FFN_TILES_PER_STEP = 2
MIX_BLOCK = 256
STAGE_SLOTS = 8
STAGE_ROWS = 256
FFN_STAGE_COLS = D_FF // 2
ADA_BLOCK = 3072
V7X_VMEM_BYTES = 64 * 1024 * 1024
VMEM_LIMIT_BYTES = V7X_VMEM_BYTES * 7 // 8


def _dot(a, b):
    return jnp.dot(a, b, preferred_element_type=F32)


def _dot_nt(a, b):
    return lax.dot_general(a, b, (((1,), (1,)), ((), ())), preferred_element_type=F32)


def _rms_scale(x):
    return x * lax.rsqrt(jnp.mean(x * x, axis=-1, keepdims=True) + EPS)


def _log_sigmoid(x):
    return -(jnp.maximum(-x, 0.0) + jnp.log1p(jnp.exp(-jnp.abs(x))))


def _stage_to_bf16(jobs, stage, sem):
    def copy(i):
        src = jobs[i][0]
        rows, cols = src.shape
        slot = i % STAGE_SLOTS
        return pltpu.make_async_copy(src, stage.at[slot, pl.ds(0, rows), pl.ds(0, cols)], sem.at[slot])

    for i in range(min(STAGE_SLOTS, len(jobs))):
        copy(i).start()
    for i, (src, dst) in enumerate(jobs):
        rows, cols = src.shape
        copy(i).wait()
        chunk = stage[i % STAGE_SLOTS, 0:rows, 0:cols]
        if dst.shape != src.shape:
            assert dst.shape == (cols, rows)
            chunk = chunk.T
        dst[...] = chunk.astype(BF16)
        if i + STAGE_SLOTS < len(jobs):
            copy(i + STAGE_SLOTS).start()


def _row_jobs(src, src_row0, dst, n_rows):
    step = min(STAGE_ROWS, n_rows)
    return [(src.at[pl.ds(src_row0 + r, step), :], dst.at[pl.ds(r, step), :])
            for r in range(0, n_rows, step)]


def _transposed_jobs(src, src_row0, dst, n_rows):
    step = min(STAGE_ROWS, n_rows)
    return [(src.at[pl.ds(src_row0 + r, step), :], dst.at[:, pl.ds(r, step)])
            for r in range(0, n_rows, step)]


def _ada_kernel(c_ref, w_ref, b_ref, o_ref):
    c = c_ref[...]
    act = c * jax.nn.sigmoid(c)
    pad = jnp.zeros((SUBLANES - act.shape[0], act.shape[1]), F32)
    act = jnp.concatenate([act, pad], axis=0)
    o_ref[...] = _dot(act.astype(BF16), w_ref[...].astype(BF16)) + b_ref[...]


def _ada_call(c, w_ada, b_ada):
    n = w_ada.shape[1]
    tn = ADA_BLOCK
    return pl.pallas_call(
        _ada_kernel,
        grid=(n // tn,),
        in_specs=[
            pl.BlockSpec(c.shape, lambda i: (0, 0)),
            pl.BlockSpec((D_MODEL, tn), lambda i: (0, i)),
            pl.BlockSpec((1, tn), lambda i: (0, i)),
        ],
        out_specs=pl.BlockSpec((SUBLANES, tn), lambda i: (0, i)),
        out_shape=jax.ShapeDtypeStruct((SUBLANES, n), F32),
        compiler_params=pltpu.CompilerParams(dimension_semantics=("arbitrary",)),
        name="adaln_mod",
    )(c, w_ada, b_ada)


def _mixer_kernel(x_ref, mod_ref, g_ref, w_in_hbm, b_xbc, b_qk, b_vo, b_gt, b_if,
                  cw_ref, hg_ref, w_pc_hbm, w_pm_hbm, w_o_hbm,
                  o_ref,
                  w_xbc, w_qk, w_vo, w_gt, w_if, w_pc, w_pm, w_o, stage, stage_sem,
                  hb_s, ubuf, y_s, sg_s, qs_s, k_s, qb_s, kb_s, vb_s, og_s, hm_s, c_s, n_s, m_s):
    tm = TM_MIX
    n_chunks = tm // CHUNK
    j = pl.program_id(1)
    n_xbc, n_qk, n_vo, n_gt = 3 * CONV_DIM, 2 * HEADS * DQK, 2 * MLSTM_DIM, 2 * D_MODEL

    @pl.when(jnp.logical_and(pl.program_id(0) == 0, j == 0))
    def _():
        o_if = n_xbc + n_qk + n_vo
        jobs = (_transposed_jobs(w_in_hbm, o_if, w_if, CHUNK)
                + _transposed_jobs(w_in_hbm, n_xbc, w_qk, n_qk)
                + _transposed_jobs(w_in_hbm, n_xbc + n_qk, w_vo, n_vo)
                + _transposed_jobs(w_in_hbm, 0, w_xbc, n_xbc)
                + _transposed_jobs(w_in_hbm, o_if + 2 * HEADS, w_gt, n_gt)
                + _row_jobs(w_pc_hbm, 0, w_pc, CONV_DIM)
                + _row_jobs(w_pm_hbm, 0, w_pm, MLSTM_DIM)
                + _row_jobs(w_o_hbm, 0, w_o, D_MODEL))
        _stage_to_bf16(jobs, stage, stage_sem)

    @pl.when(j == 0)
    def _():
        ubuf[0:SUBLANES, :] = jnp.zeros((SUBLANES, CONV_DIM), F32)
        c_s[...] = jnp.zeros_like(c_s)
        n_s[...] = jnp.zeros_like(n_s)
        m_s[...] = jnp.zeros_like(m_s)

    x = x_ref[0]
    mod_row = pl.ds(pl.program_id(0), 1)
    sh1 = mod_ref[mod_row, 0:D_MODEL]
    sc1 = mod_ref[mod_row, D_MODEL:2 * D_MODEL]
    gt1 = mod_ref[mod_row, 2 * D_MODEL:3 * D_MODEL]
    h = (_rms_scale(x) * g_ref[...]) * (1.0 + sc1) + sh1
    hb_s[...] = h.astype(BF16)


    gif = _dot(hb_s[...], w_if[...]) + b_if[...]
    lane8 = lax.broadcasted_iota(jnp.int32, (SUBLANES, CHUNK), 1)
    head_rows = lax.broadcasted_iota(jnp.int32, (SUBLANES, CHUNK), 0) < HEADS

    def lane_scan(v, op, fill):
        d = 1
        while d < CHUNK:
            v = op(v, jnp.where(lane8 >= d, pltpu.roll(v, d, axis=1), fill))
            d *= 2
        return v

    m_run = m_s[...]
    gate_rows = []
    for ci in range(n_chunks):
        gi = gif[ci * CHUNK:(ci + 1) * CHUNK, :].T[0:SUBLANES, :]
        li = jnp.where(head_rows, gi, 0.0)
        lf = jnp.where(head_rows, _log_sigmoid(pltpu.roll(gi, HEADS, axis=0)), 0.0)
        b = lane_scan(lf, jnp.add, 0.0)
        g = b[:, CHUNK - 1:CHUNK]
        r = li - b
        cm = lane_scan(r, jnp.maximum, -jnp.inf)
        r_max = cm[:, CHUNK - 1:CHUNK]
        m_prev = m_run[:, 0:1]
        big_m = jnp.maximum(m_prev, cm)
        m_loc = g + r_max
        m_new = jnp.maximum(g + m_prev, m_loc)
        gate_rows.append(dict(r=r, big_m=big_m, w=jnp.exp(r - r_max), em=jnp.exp(-(b + big_m)), m_prev=m_prev,
                              s_old=jnp.exp(g + m_prev - m_new), s_new=jnp.exp(m_loc - m_new)))
        m_run = jnp.broadcast_to(m_new, (SUBLANES, CHUNK))
    m_s[...] = m_run

    nq = HEADS * DQK
    qs = (_dot(hb_s[...], w_qk[:, 0:nq]) + b_qk[:, 0:nq]) * (DQK ** -0.5)
    qs_s[...] = qs
    qb_s[...] = qs.astype(BF16)
    k = _dot(hb_s[...], w_qk[:, nq:2 * nq]) + b_qk[:, nq:2 * nq]
    k_s[...] = k
    kb_s[...] = k.astype(BF16)
    vb_s[...] = (_dot(hb_s[...], w_vo[:, 0:MLSTM_DIM]) + b_vo[:, 0:MLSTM_DIM]).astype(BF16)
    og_s[...] = jax.nn.sigmoid(_dot(hb_s[...], w_vo[:, MLSTM_DIM:2 * MLSTM_DIM])
                               + b_vo[:, MLSTM_DIM:2 * MLSTM_DIM])

    row = lax.broadcasted_iota(jnp.int32, (CHUNK, CHUNK), 0)
    col = lax.broadcasted_iota(jnp.int32, (CHUNK, CHUNK), 1)
    tri = col <= row

    def token_major(row_vec):
        return jnp.broadcast_to(row_vec, (CHUNK, CHUNK)).T

    def pair_weights(ci, hd):
        rs = slice(ci * CHUNK, (ci + 1) * CHUNK)
        hq = slice(hd * DQK, (hd + 1) * DQK)
        gr = gate_rows[ci]
        m_bc = token_major(gr["big_m"][hd:hd + 1, :])
        w_bc = token_major(gr["w"][hd:hd + 1, :])
        r_row = gr["r"][hd:hd + 1, :]
        wts = jnp.where(tri, jnp.exp(r_row - m_bc), 0.0) * _dot_nt(qb_s[rs, hq], kb_s[rs, hq])
        kw = w_bc * k_s[rs, hq]
        return wts, kw.T.astype(BF16), jnp.sum(kw, axis=0, keepdims=True), m_bc

    def mlstm_pair(ci, hd, wts, kw_t, n_loc, m_bc):
        rs = slice(ci * CHUNK, (ci + 1) * CHUNK)
        hq = slice(hd * DQK, (hd + 1) * DQK)
        hv = slice(hd * DV, (hd + 1) * DV)
        gr = gate_rows[ci]
        em_bc = token_major(gr["em"][hd:hd + 1, :])
        si_bc = jnp.exp(gr["m_prev"][hd:hd + 1, :] - m_bc)
        s_old = gr["s_old"][hd:hd + 1, :]
        s_new = gr["s_new"][hd:hd + 1, :]
        n_prev = n_s[hd:hd + 1, :]
        c_prev = c_s[hd]
        qb = qb_s[rs, hq]
        vb = vb_s[rs, hv]

        num = (_dot(wts.astype(BF16), vb)
               + jnp.concatenate([si_bc, si_bc], axis=1) * _dot(qb, c_prev.astype(BF16)))
        den = (jnp.sum(wts, axis=1, keepdims=True)
               + si_bc * jnp.sum(qs_s[rs, hq] * n_prev, axis=1, keepdims=True))
        inv = 1.0 / jnp.maximum(jnp.abs(den), em_bc)
        hm_s[rs, hv] = num * jnp.concatenate([inv, inv], axis=1)

        c_loc = _dot(kw_t, vb)
        c_s[hd] = s_old * c_prev + s_new * c_loc
        n_s[hd:hd + 1, :] = s_old * n_prev + s_new * n_loc

    def conv_block(jb):
        lo = jb * MIX_BLOCK
        cs = slice(lo, lo + MIX_BLOCK)

        def proj(off):
            return (_dot(hb_s[...], w_xbc[:, off + lo:off + lo + MIX_BLOCK])
                    + b_xbc[:, off + lo:off + lo + MIX_BLOCK])

        xin = proj(0)
        bg = proj(CONV_DIM)
        cg = proj(2 * CONV_DIM)
        ubuf[SUBLANES:SUBLANES + tm, cs] = cg * xin
        u0 = ubuf[SUBLANES:SUBLANES + tm, cs]
        u1 = ubuf[SUBLANES - 1:SUBLANES - 1 + tm, cs]
        u2 = ubuf[SUBLANES - 2:SUBLANES - 2 + tm, cs]
        y = bg * (u2 * cw_ref[0, :, cs] + u1 * cw_ref[1, :, cs] + u0 * cw_ref[2, :, cs])
        y_s[:, cs] = y.astype(BF16)
        ubuf[0:SUBLANES, cs] = ubuf[tm:tm + SUBLANES, cs]

    def gate_block(jb):
        cs = slice(jb * MIX_BLOCK, (jb + 1) * MIX_BLOCK)
        sg_s[:, cs] = jax.nn.sigmoid(_dot(hb_s[...], w_gt[:, cs]) + b_gt[:, cs])

    def pconv_block(jb):
        cs = slice(jb * MIX_BLOCK, (jb + 1) * MIX_BLOCK)
        sg_s[:, cs] = sg_s[:, cs] * _dot(y_s[...], w_pc[:, cs])

    nb = CONV_DIM // MIX_BLOCK
    mxu_items = ([functools.partial(conv_block, jb) for jb in range(nb)]
                 + [functools.partial(gate_block, jb) for jb in range(2 * nb)]
                 + [functools.partial(pconv_block, jb) for jb in range(nb)])

    pairs = [(ci, hd) for ci in range(n_chunks) for hd in range(HEADS)]
    ahead = pair_weights(*pairs[0])
    for idx, (ci, hd) in enumerate(pairs):
        ready = ahead
        if idx + 1 < len(pairs):
            ahead = pair_weights(*pairs[idx + 1])
        mlstm_pair(ci, hd, *ready)
        lo = idx * len(mxu_items) // len(pairs)
        hi = (idx + 1) * len(mxu_items) // len(pairs)
        for item in mxu_items[lo:hi]:
            item()

    parts = []
    for hd in range(HEADS):
        hm_h = hm_s[:, hd * DV:(hd + 1) * DV]
        parts.append(_rms_scale(hm_h) * hg_ref[hd:hd + 1, :])
    hm = og_s[...] * jnp.concatenate(parts, axis=1)
    p_m = _dot(hm.astype(BF16), w_pm[...])
    z = sg_s[:, 0:D_MODEL] + sg_s[:, D_MODEL:2 * D_MODEL] * p_m
    o_ref[0] = x + gt1 * _dot(z.astype(BF16), w_o[...])


def _const_spec(shape):
    nd = len(shape)
    return pl.BlockSpec(shape, lambda b, j: (0,) * nd, pipeline_mode=pl.Buffered(1))


def _lane_window_spec(width, block):
    return pl.BlockSpec((1, width), lambda b, j: (0, block), pipeline_mode=pl.Buffered(1))


def _mixer_call(x, mod, g_norm, w_t, b_in, b_gt, conv_w, head_g, w_pc, w_pm, w_o):
    bsz, seq, d = x.shape
    tm = TM_MIX
    n_xbc, n_qk, n_vo, n_gt = 3 * CONV_DIM, 2 * HEADS * DQK, 2 * MLSTM_DIM, 2 * D_MODEL
    bias_windows = [_lane_window_spec(n_xbc, 0), _lane_window_spec(n_qk, n_xbc // n_qk),
                    _lane_window_spec(n_vo, (n_xbc + n_qk) // n_vo),
                    _const_spec(b_gt.shape),
                    _lane_window_spec(CHUNK, (n_xbc + n_qk + n_vo) // CHUNK)]
    hbm = pl.BlockSpec(memory_space=pl.ANY)
    return pl.pallas_call(
        _mixer_kernel,
        grid=(bsz, seq // tm),
        in_specs=[
            pl.BlockSpec((1, tm, d), lambda b, j: (b, j, 0)),
            _const_spec(mod.shape), _const_spec(g_norm.shape), hbm,
        ] + bias_windows + [_const_spec(conv_w.shape), _const_spec(head_g.shape), hbm, hbm, hbm],
        out_specs=pl.BlockSpec((1, tm, d), lambda b, j: (b, j, 0)),
        out_shape=jax.ShapeDtypeStruct(x.shape, F32),
        scratch_shapes=[
            pltpu.VMEM((d, n_xbc), BF16),
            pltpu.VMEM((d, n_qk), BF16),
            pltpu.VMEM((d, n_vo), BF16),
            pltpu.VMEM((d, n_gt), BF16),
            pltpu.VMEM((d, CHUNK), BF16),
            pltpu.VMEM((CONV_DIM, d), BF16),
            pltpu.VMEM((MLSTM_DIM, d), BF16),
            pltpu.VMEM((d, d), BF16),
            pltpu.VMEM((STAGE_SLOTS, STAGE_ROWS, d), F32),
            pltpu.SemaphoreType.DMA((STAGE_SLOTS,)),
            pltpu.VMEM((tm, D_MODEL), BF16),
            pltpu.VMEM((tm + SUBLANES, CONV_DIM), F32),
            pltpu.VMEM((tm, CONV_DIM), BF16),
            pltpu.VMEM((tm, 2 * D_MODEL), F32),
            pltpu.VMEM((tm, HEADS * DQK), F32),
            pltpu.VMEM((tm, HEADS * DQK), F32),
            pltpu.VMEM((tm, HEADS * DQK), BF16),
            pltpu.VMEM((tm, HEADS * DQK), BF16),
            pltpu.VMEM((tm, MLSTM_DIM), BF16),
            pltpu.VMEM((tm, MLSTM_DIM), F32),
            pltpu.VMEM((tm, MLSTM_DIM), F32),
            pltpu.VMEM((HEADS, DQK, DV), F32),
            pltpu.VMEM((SUBLANES, DQK), F32),
            pltpu.VMEM((SUBLANES, CHUNK), F32),
        ],
        compiler_params=pltpu.CompilerParams(
            dimension_semantics=("arbitrary", "arbitrary"),
            vmem_limit_bytes=VMEM_LIMIT_BYTES),
        name="mixer_sublayer",
    )(x, mod, g_norm, w_t, b_in, b_in, b_in, b_gt, b_in, conv_w, head_g, w_pc, w_pm, w_o)


def _ffn_kernel(x_ref, mod_ref, g_ref, w_up_hbm, cw_ref, w_d_hbm, gf_ref, o_ref,
                w_a, w_g, w_d, stage, stage_sem, abuf, *, final_norm):
    tm = TM_FFN
    j = pl.program_id(1)

    @pl.when(jnp.logical_and(pl.program_id(0) == 0, j == 0))
    def _():
        jobs = []
        for half, dst in enumerate((w_a, w_g)):
            for c in range(0, D_FF, FFN_STAGE_COLS):
                for r in range(0, D_MODEL, STAGE_ROWS):
                    jobs.append((w_up_hbm.at[pl.ds(r, STAGE_ROWS), pl.ds(half * D_FF + c, FFN_STAGE_COLS)],
                                 dst.at[pl.ds(r, STAGE_ROWS), pl.ds(c, FFN_STAGE_COLS)]))
        jobs += _row_jobs(w_d_hbm, 0, w_d, D_FF)
        _stage_to_bf16(jobs, stage, stage_sem)

    @pl.when(j == 0)
    def _():
        abuf[0:SUBLANES, :] = jnp.zeros((SUBLANES, D_FF), F32)

    mod_row = pl.ds(pl.program_id(0), 1)
    sh2 = mod_ref[mod_row, 3 * D_MODEL:4 * D_MODEL]
    sc2 = mod_ref[mod_row, 4 * D_MODEL:5 * D_MODEL]
    gt2 = mod_ref[mod_row, 5 * D_MODEL:6 * D_MODEL]

    def tile(i, carry):
        rows = pl.ds(pl.multiple_of(i * tm, tm), tm)
        x = x_ref[0, rows, :]
        h = (_rms_scale(x) * g_ref[...]) * (1.0 + sc2) + sh2
        hb = h.astype(BF16)
        abuf[SUBLANES:SUBLANES + tm, :] = _dot(hb, w_a[...])
        gate = _dot(hb, w_g[...])
        a0 = abuf[SUBLANES:SUBLANES + tm, :]
        a1 = abuf[SUBLANES - 1:SUBLANES - 1 + tm, :]
        a2 = abuf[SUBLANES - 2:SUBLANES - 2 + tm, :]
        ac = a2 * cw_ref[0] + a1 * cw_ref[1] + a0 * cw_ref[2]
        abuf[0:SUBLANES, :] = abuf[tm:tm + SUBLANES, :]
        act = (ac * jax.nn.sigmoid(ac)) * gate
        x2 = x + gt2 * _dot(act.astype(BF16), w_d[...])
        o_ref[0, rows, :] = _rms_scale(x2) * gf_ref[...] if final_norm else x2
        return carry

    lax.fori_loop(0, FFN_TILES_PER_STEP, tile, 0)


def _ffn_call(x, mod, g_norm, w_up, conv_w, w_d, g_final, final_norm):
    bsz, seq, d = x.shape
    tm = TM_FFN
    hbm = pl.BlockSpec(memory_space=pl.ANY)
    return pl.pallas_call(
        functools.partial(_ffn_kernel, final_norm=final_norm),
        grid=(bsz, seq // (tm * FFN_TILES_PER_STEP)),
        in_specs=[
            pl.BlockSpec((1, tm * FFN_TILES_PER_STEP, d), lambda b, j: (b, j, 0)),
            _const_spec(mod.shape), _const_spec(g_norm.shape), hbm, _const_spec(conv_w.shape), hbm,
            _const_spec(g_final.shape),
        ],
        out_specs=pl.BlockSpec((1, tm * FFN_TILES_PER_STEP, d), lambda b, j: (b, j, 0)),
        out_shape=jax.ShapeDtypeStruct(x.shape, F32),
        scratch_shapes=[
            pltpu.VMEM((d, D_FF), BF16),
            pltpu.VMEM((d, D_FF), BF16),
            pltpu.VMEM((D_FF, d), BF16),
            pltpu.VMEM((STAGE_SLOTS, STAGE_ROWS, FFN_STAGE_COLS), F32),
            pltpu.SemaphoreType.DMA((STAGE_SLOTS,)),
            pltpu.VMEM((tm + SUBLANES, D_FF), F32),
        ],
        compiler_params=pltpu.CompilerParams(
            dimension_semantics=("arbitrary", "arbitrary"),
            vmem_limit_bytes=VMEM_LIMIT_BYTES),
        name="ffn_sublayer",
    )(x, mod, g_norm, w_up, conv_w, w_d, g_final)


def kernel(x, c, w_ada, b_ada, g_norm_mix, w_in, b_in, conv_mix_w, mlstm_head_g,
           w_proj_conv, w_proj_mlstm, w_out, g_norm_ffn, w_up, conv_ffn_w, w_down, g_final):
    depth = w_ada.shape[0]
    bsz = x.shape[0]
    assert bsz <= SUBLANES
    o_gt = 3 * CONV_DIM + 2 * HEADS * DQK + 2 * MLSTM_DIM + 2 * HEADS

    for l in range(depth):
        mod = _ada_call(c, w_ada[l], b_ada[l][None, :])

        bi = b_in[l]
        x = _mixer_call(
            x, mod, g_norm_mix[l][None, :],
            w_in[l].T, bi[None, :], bi[None, o_gt:],
            conv_mix_w[l][:, None, :], mlstm_head_g[l],
            w_proj_conv[l], w_proj_mlstm[l], w_out[l])

        x = _ffn_call(
            x, mod, g_norm_ffn[l][None, :], w_up[l],
            conv_ffn_w[l][:, None, :], w_down[l], g_final[None, :],
            final_norm=(l == depth - 1))
    return x
```

```python
import functools

import jax
import jax.numpy as jnp
from jax import lax
from jax.experimental import pallas as pl
from jax.experimental.pallas import tpu as pltpu

F32 = jnp.float32
BF16 = jnp.bfloat16

D_MODEL = 1024
CONV_DIM = 1024
HEADS = 4
DQK = 128
DV = 256
MLSTM_DIM = HEADS * DV
CHUNK = 128
D_FF = 2816
N_MOD = 6
EPS = 1e-6

SUBLANES = 8
TM_MIX = 512
MIX_SEQS = 2
TM_FFN = 512
MIX_BLOCK = 256
STAGE_SLOTS = 8
STAGE_ROWS = 256
FFN_STAGE_COLS = D_FF // 2
ADA_BLOCK = 3072
V7X_VMEM_BYTES = 64 * 1024 * 1024
VMEM_LIMIT_BYTES = V7X_VMEM_BYTES * 7 // 8


def _dot(a, b):
    return jnp.dot(a, b, preferred_element_type=F32)


def _dot_nt(a, b):
    return lax.dot_general(a, b, (((1,), (1,)), ((), ())), preferred_element_type=F32)


def _rms_scale(x):
    return x * lax.rsqrt(jnp.mean(x * x, axis=-1, keepdims=True) + EPS)


def _log_sigmoid(x):
    return -(jnp.maximum(-x, 0.0) + jnp.log1p(jnp.exp(-jnp.abs(x))))


def _stage_to_bf16(jobs, stage, sem):
    def copy(i):
        src = jobs[i][0]
        rows, cols = src.shape
        slot = i % STAGE_SLOTS
        return pltpu.make_async_copy(src, stage.at[slot, pl.ds(0, rows), pl.ds(0, cols)], sem.at[slot])

    for i in range(min(STAGE_SLOTS, len(jobs))):
        copy(i).start()
    for i, (src, dst) in enumerate(jobs):
        rows, cols = src.shape
        copy(i).wait()
        chunk = stage[i % STAGE_SLOTS, 0:rows, 0:cols]
        if dst.shape != src.shape:
            assert dst.shape == (cols, rows)
            chunk = chunk.T
        dst[...] = chunk.astype(BF16)
        if i + STAGE_SLOTS < len(jobs):
            copy(i + STAGE_SLOTS).start()


def _row_jobs(src, src_row0, dst, n_rows):
    step = min(STAGE_ROWS, n_rows)
    return [(src.at[pl.ds(src_row0 + r, step), :], dst.at[pl.ds(r, step), :])
            for r in range(0, n_rows, step)]


def _transposed_jobs(src, src_row0, dst, n_rows):
    step = min(STAGE_ROWS, n_rows)
    return [(src.at[pl.ds(src_row0 + r, step), :], dst.at[:, pl.ds(r, step)])
            for r in range(0, n_rows, step)]


def _ada_kernel(c_ref, w_ref, b_ref, o_ref):
    c = c_ref[...]
    act = c * jax.nn.sigmoid(c)
    pad = jnp.zeros((SUBLANES - act.shape[0], act.shape[1]), F32)
    act = jnp.concatenate([act, pad], axis=0)
    o_ref[...] = _dot(act.astype(BF16), w_ref[...].astype(BF16)) + b_ref[...]


def _ada_call(c, w_ada, b_ada):
    n = w_ada.shape[1]
    tn = ADA_BLOCK
    return pl.pallas_call(
        _ada_kernel,
        grid=(n // tn,),
        in_specs=[
            pl.BlockSpec(c.shape, lambda i: (0, 0)),
            pl.BlockSpec((D_MODEL, tn), lambda i: (0, i)),
            pl.BlockSpec((1, tn), lambda i: (0, i)),
        ],
        out_specs=pl.BlockSpec((SUBLANES, tn), lambda i: (0, i)),
        out_shape=jax.ShapeDtypeStruct((SUBLANES, n), F32),
        compiler_params=pltpu.CompilerParams(dimension_semantics=("arbitrary",)),
        name="adaln_mod",
    )(c, w_ada, b_ada)


def _mixer_kernel(x_ref, mod_ref, g_ref, w_in_hbm, b_xbc, b_qk, b_vo, b_gt, b_if,
                  cw_ref, hg_ref, w_pc_hbm, w_pm_hbm, w_o_hbm,
                  o_ref,
                  w_xbc, w_qk, w_vo, w_gt, w_if, w_pc, w_pm, w_o, stage, stage_sem,
                  hb_s, ubuf, y_s, sg_s, qs_s, k_s, qb_s, kb_s, vb_s, og_s, hm_s, c_s, n_s, m_s):
    tm = TM_MIX
    ts = tm // MIX_SEQS
    n_chunks = tm // CHUNK
    chunks_per_seq = ts // CHUNK
    seq_rows = [slice(s * ts, (s + 1) * ts) for s in range(MIX_SEQS)]
    j = pl.program_id(1)
    n_xbc, n_qk, n_vo, n_gt = 3 * CONV_DIM, 2 * HEADS * DQK, 2 * MLSTM_DIM, 2 * D_MODEL

    @pl.when(jnp.logical_and(pl.program_id(0) == 0, j == 0))
    def _():
        o_if = n_xbc + n_qk + n_vo
        jobs = (_transposed_jobs(w_in_hbm, o_if, w_if, CHUNK)
                + _transposed_jobs(w_in_hbm, n_xbc, w_qk, n_qk)
                + _transposed_jobs(w_in_hbm, n_xbc + n_qk, w_vo, n_vo)
                + _transposed_jobs(w_in_hbm, 0, w_xbc, n_xbc)
                + _transposed_jobs(w_in_hbm, o_if + 2 * HEADS, w_gt, n_gt)
                + _row_jobs(w_pc_hbm, 0, w_pc, CONV_DIM)
                + _row_jobs(w_pm_hbm, 0, w_pm, MLSTM_DIM)
                + _row_jobs(w_o_hbm, 0, w_o, D_MODEL))
        _stage_to_bf16(jobs, stage, stage_sem)

    @pl.when(j == 0)
    def _():
        ubuf[:, 0:SUBLANES, :] = jnp.zeros((MIX_SEQS, SUBLANES, CONV_DIM), F32)
        c_s[...] = jnp.zeros_like(c_s)
        n_s[...] = jnp.zeros_like(n_s)
        m_s[...] = jnp.zeros_like(m_s)

    def mod_vec(s, k):
        return mod_ref[pl.ds(pl.program_id(0) * MIX_SEQS + s, 1), k * D_MODEL:(k + 1) * D_MODEL]

    for s in range(MIX_SEQS):
        h = (_rms_scale(x_ref[s]) * g_ref[...]) * (1.0 + mod_vec(s, 1)) + mod_vec(s, 0)
        hb_s[seq_rows[s], :] = h.astype(BF16)


    gif = _dot(hb_s[...], w_if[...]) + b_if[...]
    lane8 = lax.broadcasted_iota(jnp.int32, (SUBLANES, CHUNK), 1)
    head_rows = lax.broadcasted_iota(jnp.int32, (SUBLANES, CHUNK), 0) < HEADS

    def lane_scan(v, op, fill):
        d = 1
        while d < CHUNK:
            v = op(v, jnp.where(lane8 >= d, pltpu.roll(v, d, axis=1), fill))
            d *= 2
        return v

    m_runs = [m_s[s] for s in range(MIX_SEQS)]
    gate_rows = []
    for ci in range(n_chunks):
        m_run = m_runs[ci // chunks_per_seq]
        gi = gif[ci * CHUNK:(ci + 1) * CHUNK, :].T[0:SUBLANES, :]
        li = jnp.where(head_rows, gi, 0.0)
        lf = jnp.where(head_rows, _log_sigmoid(pltpu.roll(gi, HEADS, axis=0)), 0.0)
        b = lane_scan(lf, jnp.add, 0.0)
        g = b[:, CHUNK - 1:CHUNK]
        r = li - b
        cm = lane_scan(r, jnp.maximum, -jnp.inf)
        r_max = cm[:, CHUNK - 1:CHUNK]
        m_prev = m_run[:, 0:1]
        big_m = jnp.maximum(m_prev, cm)
        m_loc = g + r_max
        m_new = jnp.maximum(g + m_prev, m_loc)
        gate_rows.append(dict(r=r, big_m=big_m, w=jnp.exp(r - r_max), em=jnp.exp(-(b + big_m)), m_prev=m_prev,
                              s_old=jnp.exp(g + m_prev - m_new), s_new=jnp.exp(m_loc - m_new)))
        m_runs[ci // chunks_per_seq] = jnp.broadcast_to(m_new, (SUBLANES, CHUNK))
    for s in range(MIX_SEQS):
        m_s[s] = m_runs[s]

    nq = HEADS * DQK
    qs = (_dot(hb_s[...], w_qk[:, 0:nq]) + b_qk[:, 0:nq]) * (DQK ** -0.5)
    qs_s[...] = qs
    qb_s[...] = qs.astype(BF16)
    k = _dot(hb_s[...], w_qk[:, nq:2 * nq]) + b_qk[:, nq:2 * nq]
    k_s[...] = k
    kb_s[...] = k.astype(BF16)
    vb_s[...] = (_dot(hb_s[...], w_vo[:, 0:MLSTM_DIM]) + b_vo[:, 0:MLSTM_DIM]).astype(BF16)
    og_s[...] = jax.nn.sigmoid(_dot(hb_s[...], w_vo[:, MLSTM_DIM:2 * MLSTM_DIM])
                               + b_vo[:, MLSTM_DIM:2 * MLSTM_DIM])

    row = lax.broadcasted_iota(jnp.int32, (CHUNK, CHUNK), 0)
    col = lax.broadcasted_iota(jnp.int32, (CHUNK, CHUNK), 1)
    tri = col <= row

    def token_major(row_vec):
        return jnp.broadcast_to(row_vec, (CHUNK, CHUNK)).T

    def pair_weights(ci, hd):
        rs = slice(ci * CHUNK, (ci + 1) * CHUNK)
        hq = slice(hd * DQK, (hd + 1) * DQK)
        gr = gate_rows[ci]
        m_bc = token_major(gr["big_m"][hd:hd + 1, :])
        w_bc = token_major(gr["w"][hd:hd + 1, :])
        r_row = gr["r"][hd:hd + 1, :]
        wts = jnp.where(tri, jnp.exp(r_row - m_bc), 0.0) * _dot_nt(qb_s[rs, hq], kb_s[rs, hq])
        kw = w_bc * k_s[rs, hq]
        return wts, kw.T.astype(BF16), jnp.sum(kw, axis=0, keepdims=True), m_bc

    def mlstm_pair(ci, hd, wts, kw_t, n_loc, m_bc):
        rs = slice(ci * CHUNK, (ci + 1) * CHUNK)
        hq = slice(hd * DQK, (hd + 1) * DQK)
        hv = slice(hd * DV, (hd + 1) * DV)
        gr = gate_rows[ci]
        em_bc = token_major(gr["em"][hd:hd + 1, :])
        si_bc = jnp.exp(gr["m_prev"][hd:hd + 1, :] - m_bc)
        s_old = gr["s_old"][hd:hd + 1, :]
        s_new = gr["s_new"][hd:hd + 1, :]
        sq = ci // chunks_per_seq
        n_prev = n_s[sq, hd:hd + 1, :]
        c_prev = c_s[sq * HEADS + hd]
        qb = qb_s[rs, hq]
        vb = vb_s[rs, hv]

        num = (_dot(wts.astype(BF16), vb)
               + jnp.concatenate([si_bc, si_bc], axis=1) * _dot(qb, c_prev.astype(BF16)))
        den = (jnp.sum(wts, axis=1, keepdims=True)
               + si_bc * jnp.sum(qs_s[rs, hq] * n_prev, axis=1, keepdims=True))
        inv = 1.0 / jnp.maximum(jnp.abs(den), em_bc)
        hm_s[rs, hv] = num * jnp.concatenate([inv, inv], axis=1)

        c_loc = _dot(kw_t, vb)
        c_s[sq * HEADS + hd] = s_old * c_prev + s_new * c_loc
        n_s[sq, hd:hd + 1, :] = s_old * n_prev + s_new * n_loc

    def conv_block(jb):
        lo = jb * MIX_BLOCK
        cs = slice(lo, lo + MIX_BLOCK)

        def proj(off):
            return (_dot(hb_s[...], w_xbc[:, off + lo:off + lo + MIX_BLOCK])
                    + b_xbc[:, off + lo:off + lo + MIX_BLOCK])

        xin = proj(0)
        bg = proj(CONV_DIM)
        cg = proj(2 * CONV_DIM)
        u = cg * xin
        for s in range(MIX_SEQS):
            ubuf[s, SUBLANES:SUBLANES + ts, cs] = u[seq_rows[s]]
            u0 = ubuf[s, SUBLANES:SUBLANES + ts, cs]
            u1 = ubuf[s, SUBLANES - 1:SUBLANES - 1 + ts, cs]
            u2 = ubuf[s, SUBLANES - 2:SUBLANES - 2 + ts, cs]
            y = bg[seq_rows[s]] * (u2 * cw_ref[0, :, cs] + u1 * cw_ref[1, :, cs] + u0 * cw_ref[2, :, cs])
            y_s[seq_rows[s], cs] = y.astype(BF16)
            ubuf[s, 0:SUBLANES, cs] = ubuf[s, ts:ts + SUBLANES, cs]

    def gate_block(jb):
        cs = slice(jb * MIX_BLOCK, (jb + 1) * MIX_BLOCK)
        sg_s[:, cs] = jax.nn.sigmoid(_dot(hb_s[...], w_gt[:, cs]) + b_gt[:, cs])

    def pconv_block(jb):
        cs = slice(jb * MIX_BLOCK, (jb + 1) * MIX_BLOCK)
        sg_s[:, cs] = sg_s[:, cs] * _dot(y_s[...], w_pc[:, cs])

    nb = CONV_DIM // MIX_BLOCK
    mxu_items = ([functools.partial(conv_block, jb) for jb in range(nb)]
                 + [functools.partial(gate_block, jb) for jb in range(2 * nb)]
                 + [functools.partial(pconv_block, jb) for jb in range(nb)])

    chunk_order = [s * chunks_per_seq + c for c in range(chunks_per_seq) for s in range(MIX_SEQS)]
    pairs = [(ci, hd) for ci in chunk_order for hd in range(HEADS)]
    ahead = pair_weights(*pairs[0])
    for idx, (ci, hd) in enumerate(pairs):
        ready = ahead
        if idx + 1 < len(pairs):
            ahead = pair_weights(*pairs[idx + 1])
        mlstm_pair(ci, hd, *ready)
        lo = idx * len(mxu_items) // len(pairs)
        hi = (idx + 1) * len(mxu_items) // len(pairs)
        for item in mxu_items[lo:hi]:
            item()

    parts = []
    for hd in range(HEADS):
        hm_h = hm_s[:, hd * DV:(hd + 1) * DV]
        parts.append(_rms_scale(hm_h) * hg_ref[hd:hd + 1, :])
    hm = og_s[...] * jnp.concatenate(parts, axis=1)
    p_m = _dot(hm.astype(BF16), w_pm[...])
    z = sg_s[:, 0:D_MODEL] + sg_s[:, D_MODEL:2 * D_MODEL] * p_m
    mixed = _dot(z.astype(BF16), w_o[...])
    for s in range(MIX_SEQS):
        o_ref[s] = x_ref[s] + mod_vec(s, 2) * mixed[seq_rows[s]]


def _const_spec(shape):
    nd = len(shape)
    return pl.BlockSpec(shape, lambda b, j: (0,) * nd, pipeline_mode=pl.Buffered(1))


def _lane_window_spec(width, block):
    return pl.BlockSpec((1, width), lambda b, j: (0, block), pipeline_mode=pl.Buffered(1))


def _mixer_call(x, mod, g_norm, w_t, b_in, b_gt, conv_w, head_g, w_pc, w_pm, w_o):
    bsz, seq, d = x.shape
    tm = TM_MIX
    ts = tm // MIX_SEQS
    n_xbc, n_qk, n_vo, n_gt = 3 * CONV_DIM, 2 * HEADS * DQK, 2 * MLSTM_DIM, 2 * D_MODEL
    bias_windows = [_lane_window_spec(n_xbc, 0), _lane_window_spec(n_qk, n_xbc // n_qk),
                    _lane_window_spec(n_vo, (n_xbc + n_qk) // n_vo),
                    _const_spec(b_gt.shape),
                    _lane_window_spec(CHUNK, (n_xbc + n_qk + n_vo) // CHUNK)]
    hbm = pl.BlockSpec(memory_space=pl.ANY)
    return pl.pallas_call(
        _mixer_kernel,
        grid=(bsz // MIX_SEQS, seq // ts),
        in_specs=[
            pl.BlockSpec((MIX_SEQS, ts, d), lambda b, j: (b, j, 0)),
            _const_spec(mod.shape), _const_spec(g_norm.shape), hbm,
        ] + bias_windows + [_const_spec(conv_w.shape), _const_spec(head_g.shape), hbm, hbm, hbm],
        out_specs=pl.BlockSpec((MIX_SEQS, ts, d), lambda b, j: (b, j, 0)),
        out_shape=jax.ShapeDtypeStruct(x.shape, F32),
        scratch_shapes=[
            pltpu.VMEM((d, n_xbc), BF16),
            pltpu.VMEM((d, n_qk), BF16),
            pltpu.VMEM((d, n_vo), BF16),
            pltpu.VMEM((d, n_gt), BF16),
            pltpu.VMEM((d, CHUNK), BF16),
            pltpu.VMEM((CONV_DIM, d), BF16),
            pltpu.VMEM((MLSTM_DIM, d), BF16),
            pltpu.VMEM((d, d), BF16),
            pltpu.VMEM((STAGE_SLOTS, STAGE_ROWS, d), F32),
            pltpu.SemaphoreType.DMA((STAGE_SLOTS,)),
            pltpu.VMEM((tm, D_MODEL), BF16),
            pltpu.VMEM((MIX_SEQS, ts + SUBLANES, CONV_DIM), F32),
            pltpu.VMEM((tm, CONV_DIM), BF16),
            pltpu.VMEM((tm, 2 * D_MODEL), F32),
            pltpu.VMEM((tm, HEADS * DQK), F32),
            pltpu.VMEM((tm, HEADS * DQK), F32),
            pltpu.VMEM((tm, HEADS * DQK), BF16),
            pltpu.VMEM((tm, HEADS * DQK), BF16),
            pltpu.VMEM((tm, MLSTM_DIM), BF16),
            pltpu.VMEM((tm, MLSTM_DIM), F32),
            pltpu.VMEM((tm, MLSTM_DIM), F32),
            pltpu.VMEM((MIX_SEQS * HEADS, DQK, DV), F32),
            pltpu.VMEM((MIX_SEQS, SUBLANES, DQK), F32),
            pltpu.VMEM((MIX_SEQS, SUBLANES, CHUNK), F32),
        ],
        compiler_params=pltpu.CompilerParams(
            dimension_semantics=("arbitrary", "arbitrary"),
            vmem_limit_bytes=VMEM_LIMIT_BYTES),
        name="mixer_sublayer",
    )(x, mod, g_norm, w_t, b_in, b_in, b_in, b_gt, b_in, conv_w, head_g, w_pc, w_pm, w_o)


def _ffn_kernel(x_ref, mod_ref, g_ref, w_up_hbm, cw_ref, w_d_hbm, gf_ref, o_ref,
                w_a, w_g, w_d, stage, stage_sem, abuf, *, final_norm):
    tm = TM_FFN
    j = pl.program_id(1)

    @pl.when(jnp.logical_and(pl.program_id(0) == 0, j == 0))
    def _():
        jobs = []
        for half, dst in enumerate((w_a, w_g)):
            for c in range(0, D_FF, FFN_STAGE_COLS):
                for r in range(0, D_MODEL, STAGE_ROWS):
                    jobs.append((w_up_hbm.at[pl.ds(r, STAGE_ROWS), pl.ds(half * D_FF + c, FFN_STAGE_COLS)],
                                 dst.at[pl.ds(r, STAGE_ROWS), pl.ds(c, FFN_STAGE_COLS)]))
        jobs += _row_jobs(w_d_hbm, 0, w_d, D_FF)
        _stage_to_bf16(jobs, stage, stage_sem)

    @pl.when(j == 0)
    def _():
        abuf[0:SUBLANES, :] = jnp.zeros((SUBLANES, D_FF), F32)

    x = x_ref[0]
    mod_row = pl.ds(pl.program_id(0), 1)
    sh2 = mod_ref[mod_row, 3 * D_MODEL:4 * D_MODEL]
    sc2 = mod_ref[mod_row, 4 * D_MODEL:5 * D_MODEL]
    gt2 = mod_ref[mod_row, 5 * D_MODEL:6 * D_MODEL]
    h = (_rms_scale(x) * g_ref[...]) * (1.0 + sc2) + sh2
    hb = h.astype(BF16)
    abuf[SUBLANES:SUBLANES + tm, :] = _dot(hb, w_a[...])
    gate = _dot(hb, w_g[...])
    a0 = abuf[SUBLANES:SUBLANES + tm, :]
    a1 = abuf[SUBLANES - 1:SUBLANES - 1 + tm, :]
    a2 = abuf[SUBLANES - 2:SUBLANES - 2 + tm, :]
    ac = a2 * cw_ref[0] + a1 * cw_ref[1] + a0 * cw_ref[2]
    abuf[0:SUBLANES, :] = abuf[tm:tm + SUBLANES, :]
    act = (ac * jax.nn.sigmoid(ac)) * gate
    x2 = x + gt2 * _dot(act.astype(BF16), w_d[...])
    o_ref[0] = _rms_scale(x2) * gf_ref[...] if final_norm else x2


def _ffn_call(x, mod, g_norm, w_up, conv_w, w_d, g_final, final_norm):
    bsz, seq, d = x.shape
    tm = TM_FFN
    hbm = pl.BlockSpec(memory_space=pl.ANY)
    return pl.pallas_call(
        functools.partial(_ffn_kernel, final_norm=final_norm),
        grid=(bsz, seq // tm),
        in_specs=[
            pl.BlockSpec((1, tm, d), lambda b, j: (b, j, 0)),
            _const_spec(mod.shape), _const_spec(g_norm.shape), hbm, _const_spec(conv_w.shape), hbm,
            _const_spec(g_final.shape),
        ],
        out_specs=pl.BlockSpec((1, tm, d), lambda b, j: (b, j, 0)),
        out_shape=jax.ShapeDtypeStruct(x.shape, F32),
        scratch_shapes=[
            pltpu.VMEM((d, D_FF), BF16),
            pltpu.VMEM((d, D_FF), BF16),
            pltpu.VMEM((D_FF, d), BF16),
            pltpu.VMEM((STAGE_SLOTS, STAGE_ROWS, FFN_STAGE_COLS), F32),
            pltpu.SemaphoreType.DMA((STAGE_SLOTS,)),
            pltpu.VMEM((tm + SUBLANES, D_FF), F32),
        ],
        compiler_params=pltpu.CompilerParams(
            dimension_semantics=("arbitrary", "arbitrary"),
            vmem_limit_bytes=VMEM_LIMIT_BYTES),
        name="ffn_sublayer",
    )(x, mod, g_norm, w_up, conv_w, w_d, g_final)


def kernel(x, c, w_ada, b_ada, g_norm_mix, w_in, b_in, conv_mix_w, mlstm_head_g,
           w_proj_conv, w_proj_mlstm, w_out, g_norm_ffn, w_up, conv_ffn_w, w_down, g_final):
    depth = w_ada.shape[0]
    bsz = x.shape[0]
    assert bsz <= SUBLANES
    o_gt = 3 * CONV_DIM + 2 * HEADS * DQK + 2 * MLSTM_DIM + 2 * HEADS

    for l in range(depth):
        mod = _ada_call(c, w_ada[l], b_ada[l][None, :])

        bi = b_in[l]
        x = _mixer_call(
            x, mod, g_norm_mix[l][None, :],
            w_in[l].T, bi[None, :], bi[None, o_gt:],
            conv_mix_w[l][:, None, :], mlstm_head_g[l],
            w_proj_conv[l], w_proj_mlstm[l], w_out[l])

        x = _ffn_call(
            x, mod, g_norm_ffn[l][None, :], w_up[l],
            conv_ffn_w[l][:, None, :], w_down[l], g_final[None, :],
            final_norm=(l == depth - 1))
    return x
```

```python
import functools

import jax
import jax.numpy as jnp
from jax import lax
from jax.experimental import pallas as pl
from jax.experimental.pallas import tpu as pltpu

F32 = jnp.float32
BF16 = jnp.bfloat16

D_MODEL = 1024
CONV_DIM = 1024
HEADS = 4
DQK = 128
DV = 256
MLSTM_DIM = HEADS * DV
CHUNK = 128
D_FF = 2816
N_MOD = 6
EPS = 1e-6

SUBLANES = 8
TM_MIX = 512
TM_FFN = 512
MIX_BLOCK = 256
STAGE_SLOTS = 8
STAGE_ROWS = 256
FFN_STAGE_COLS = D_FF // 2
ADA_BLOCK = 3072
V7X_VMEM_BYTES = 64 * 1024 * 1024
VMEM_LIMIT_BYTES = V7X_VMEM_BYTES * 7 // 8


def _dot(a, b):
    return jnp.dot(a, b, preferred_element_type=F32)


def _dot_nt(a, b):
    return lax.dot_general(a, b, (((1,), (1,)), ((), ())), preferred_element_type=F32)


def _rms_scale(x):
    return x * lax.rsqrt(jnp.mean(x * x, axis=-1, keepdims=True) + EPS)


def _log_sigmoid(x):
    return -(jnp.maximum(-x, 0.0) + jnp.log1p(jnp.exp(-jnp.abs(x))))


class _LazyStage:
    def __init__(self, groups, stage, sem):
        self.jobs = [job for _, jobs in groups for job in jobs]
        self.end, n = {}, 0
        for key, jobs in groups:
            n += len(jobs)
            self.end[key] = n
        self.stage, self.sem, self.done = stage, sem, 0
        for i in range(min(STAGE_SLOTS, len(self.jobs))):
            self._copy(i).start()

    def _copy(self, i):
        src = self.jobs[i][0]
        rows, cols = src.shape
        slot = i % STAGE_SLOTS
        return pltpu.make_async_copy(src, self.stage.at[slot, pl.ds(0, rows), pl.ds(0, cols)],
                                     self.sem.at[slot])

    def need(self, key):
        for i in range(self.done, self.end[key]):
            src, dst = self.jobs[i]
            rows, cols = src.shape
            self._copy(i).wait()
            chunk = self.stage[i % STAGE_SLOTS, 0:rows, 0:cols]
            if dst.shape != src.shape:
                assert dst.shape == (cols, rows)
                chunk = chunk.T
            dst[...] = chunk.astype(BF16)
            if i + STAGE_SLOTS < len(self.jobs):
                self._copy(i + STAGE_SLOTS).start()
        self.done = max(self.done, self.end[key])


def _row_jobs(src, src_row0, dst, n_rows):
    step = min(STAGE_ROWS, n_rows)
    return [(src.at[pl.ds(src_row0 + r, step), :], dst.at[pl.ds(r, step), :])
            for r in range(0, n_rows, step)]


def _transposed_jobs(src, src_row0, dst, dst_col0, n_rows):
    step = min(STAGE_ROWS, n_rows)
    return [(src.at[pl.ds(src_row0 + r, step), :], dst.at[:, pl.ds(dst_col0 + r, step)])
            for r in range(0, n_rows, step)]


def _ada_kernel(c_ref, w_ref, b_ref, o_ref):
    c = c_ref[...]
    act = c * jax.nn.sigmoid(c)
    pad = jnp.zeros((SUBLANES - act.shape[0], act.shape[1]), F32)
    act = jnp.concatenate([act, pad], axis=0)
    o_ref[...] = _dot(act.astype(BF16), w_ref[...].astype(BF16)) + b_ref[...]


def _ada_call(c, w_ada, b_ada):
    n = w_ada.shape[1]
    tn = ADA_BLOCK
    return pl.pallas_call(
        _ada_kernel,
        grid=(n // tn,),
        in_specs=[
            pl.BlockSpec(c.shape, lambda i: (0, 0)),
            pl.BlockSpec((D_MODEL, tn), lambda i: (0, i)),
            pl.BlockSpec((1, tn), lambda i: (0, i)),
        ],
        out_specs=pl.BlockSpec((SUBLANES, tn), lambda i: (0, i)),
        out_shape=jax.ShapeDtypeStruct((SUBLANES, n), F32),
        compiler_params=pltpu.CompilerParams(dimension_semantics=("arbitrary",)),
        name="adaln_mod",
    )(c, w_ada, b_ada)


def _mixer_kernel(x_ref, mod_ref, g_ref, w_in_hbm, b_xbc, b_qk, b_vo, b_gt, b_if,
                  cw_ref, hg_ref, w_pc_hbm, w_pm_hbm, w_o_hbm,
                  o_ref,
                  w_xbc, w_qk, w_vo, w_gt, w_if, w_pc, w_pm, w_o, stage, stage_sem,
                  hb_s, ubuf, y_s, sg_s, qs_s, k_s, qb_s, kb_s, vb_s, og_s, hm_s, c_s, n_s, m_s):
    tm = TM_MIX
    n_chunks = tm // CHUNK
    j = pl.program_id(1)
    n_xbc, n_qk, n_vo, n_gt = 3 * CONV_DIM, 2 * HEADS * DQK, 2 * MLSTM_DIM, 2 * D_MODEL
    nb = CONV_DIM // MIX_BLOCK
    nq = HEADS * DQK

    def weight_groups():
        o_if = n_xbc + n_qk + n_vo
        groups = [("if", _transposed_jobs(w_in_hbm, o_if, w_if, 0, CHUNK)),
                  ("q", _transposed_jobs(w_in_hbm, n_xbc, w_qk, 0, nq)),
                  ("k", _transposed_jobs(w_in_hbm, n_xbc + nq, w_qk, nq, nq)),
                  ("v", _transposed_jobs(w_in_hbm, n_xbc + n_qk, w_vo, 0, MLSTM_DIM)),
                  ("o", _transposed_jobs(w_in_hbm, n_xbc + n_qk + MLSTM_DIM, w_vo, MLSTM_DIM, MLSTM_DIM))]
        for jb in range(nb):
            lo = jb * MIX_BLOCK
            groups.append((("conv", jb),
                           [job for off in (0, CONV_DIM, 2 * CONV_DIM)
                            for job in _transposed_jobs(w_in_hbm, off + lo, w_xbc, off + lo, MIX_BLOCK)]))
        for jb in range(2 * nb):
            lo = jb * MIX_BLOCK
            groups.append((("gate", jb),
                           _transposed_jobs(w_in_hbm, o_if + 2 * HEADS + lo, w_gt, lo, MIX_BLOCK)))
        groups += [("pc", _row_jobs(w_pc_hbm, 0, w_pc, CONV_DIM)),
                   ("pm", _row_jobs(w_pm_hbm, 0, w_pm, MLSTM_DIM)),
                   ("wo", _row_jobs(w_o_hbm, 0, w_o, D_MODEL))]
        return groups

    @pl.when(j == 0)
    def _():
        ubuf[0:SUBLANES, :] = jnp.zeros((SUBLANES, CONV_DIM), F32)
        c_s[...] = jnp.zeros_like(c_s)
        n_s[...] = jnp.zeros_like(n_s)
        m_s[...] = jnp.zeros_like(m_s)

    def step(need):
        x = x_ref[0]
        mod_row = pl.ds(pl.program_id(0), 1)
        sh1 = mod_ref[mod_row, 0:D_MODEL]
        sc1 = mod_ref[mod_row, D_MODEL:2 * D_MODEL]
        gt1 = mod_ref[mod_row, 2 * D_MODEL:3 * D_MODEL]
        h = (_rms_scale(x) * g_ref[...]) * (1.0 + sc1) + sh1
        hb_s[...] = h.astype(BF16)


        need("if")
        gif = _dot(hb_s[...], w_if[...]) + b_if[...]
        lane8 = lax.broadcasted_iota(jnp.int32, (SUBLANES, CHUNK), 1)
        head_rows = lax.broadcasted_iota(jnp.int32, (SUBLANES, CHUNK), 0) < HEADS

        def lane_scan(v, op, fill):
            d = 1
            while d < CHUNK:
                v = op(v, jnp.where(lane8 >= d, pltpu.roll(v, d, axis=1), fill))
                d *= 2
            return v

        m_run = m_s[...]
        gate_rows = []
        for ci in range(n_chunks):
            gi = gif[ci * CHUNK:(ci + 1) * CHUNK, :].T[0:SUBLANES, :]
            li = jnp.where(head_rows, gi, 0.0)
            lf = jnp.where(head_rows, _log_sigmoid(pltpu.roll(gi, HEADS, axis=0)), 0.0)
            b = lane_scan(lf, jnp.add, 0.0)
            g = b[:, CHUNK - 1:CHUNK]
            r = li - b
            cm = lane_scan(r, jnp.maximum, -jnp.inf)
            r_max = cm[:, CHUNK - 1:CHUNK]
            m_prev = m_run[:, 0:1]
            big_m = jnp.maximum(m_prev, cm)
            m_loc = g + r_max
            m_new = jnp.maximum(g + m_prev, m_loc)
            gate_rows.append(dict(r=r, big_m=big_m, w=jnp.exp(r - r_max), em=jnp.exp(-(b + big_m)), m_prev=m_prev,
                                  s_old=jnp.exp(g + m_prev - m_new), s_new=jnp.exp(m_loc - m_new)))
            m_run = jnp.broadcast_to(m_new, (SUBLANES, CHUNK))
        m_s[...] = m_run

        need("q")
        qs = (_dot(hb_s[...], w_qk[:, 0:nq]) + b_qk[:, 0:nq]) * (DQK ** -0.5)
        qs_s[...] = qs
        qb_s[...] = qs.astype(BF16)
        need("k")
        k = _dot(hb_s[...], w_qk[:, nq:2 * nq]) + b_qk[:, nq:2 * nq]
        k_s[...] = k
        kb_s[...] = k.astype(BF16)
        need("v")
        vb_s[...] = (_dot(hb_s[...], w_vo[:, 0:MLSTM_DIM]) + b_vo[:, 0:MLSTM_DIM]).astype(BF16)
        need("o")
        og_s[...] = jax.nn.sigmoid(_dot(hb_s[...], w_vo[:, MLSTM_DIM:2 * MLSTM_DIM])
                                   + b_vo[:, MLSTM_DIM:2 * MLSTM_DIM])

        row = lax.broadcasted_iota(jnp.int32, (CHUNK, CHUNK), 0)
        col = lax.broadcasted_iota(jnp.int32, (CHUNK, CHUNK), 1)
        tri = col <= row

        def token_major(row_vec):
            return jnp.broadcast_to(row_vec, (CHUNK, CHUNK)).T

        def pair_weights(ci, hd):
            rs = slice(ci * CHUNK, (ci + 1) * CHUNK)
            hq = slice(hd * DQK, (hd + 1) * DQK)
            gr = gate_rows[ci]
            m_bc = token_major(gr["big_m"][hd:hd + 1, :])
            w_bc = token_major(gr["w"][hd:hd + 1, :])
            r_row = gr["r"][hd:hd + 1, :]
            wts = jnp.where(tri, jnp.exp(r_row - m_bc), 0.0) * _dot_nt(qb_s[rs, hq], kb_s[rs, hq])
            kw = w_bc * k_s[rs, hq]
            return wts, kw.T.astype(BF16), jnp.sum(kw, axis=0, keepdims=True), m_bc

        def mlstm_pair(ci, hd, wts, kw_t, n_loc, m_bc):
            rs = slice(ci * CHUNK, (ci + 1) * CHUNK)
            hq = slice(hd * DQK, (hd + 1) * DQK)
            hv = slice(hd * DV, (hd + 1) * DV)
            gr = gate_rows[ci]
            em_bc = token_major(gr["em"][hd:hd + 1, :])
            si_bc = jnp.exp(gr["m_prev"][hd:hd + 1, :] - m_bc)
            s_old = gr["s_old"][hd:hd + 1, :]
            s_new = gr["s_new"][hd:hd + 1, :]
            n_prev = n_s[hd:hd + 1, :]
            c_prev = c_s[hd]
            qb = qb_s[rs, hq]
            vb = vb_s[rs, hv]

            num = (_dot(wts.astype(BF16), vb)
                   + jnp.concatenate([si_bc, si_bc], axis=1) * _dot(qb, c_prev.astype(BF16)))
            den = (jnp.sum(wts, axis=1, keepdims=True)
                   + si_bc * jnp.sum(qs_s[rs, hq] * n_prev, axis=1, keepdims=True))
            inv = 1.0 / jnp.maximum(jnp.abs(den), em_bc)
            hm_s[rs, hv] = num * jnp.concatenate([inv, inv], axis=1)

            c_loc = _dot(kw_t, vb)
            c_s[hd] = s_old * c_prev + s_new * c_loc
            n_s[hd:hd + 1, :] = s_old * n_prev + s_new * n_loc

        def conv_block(jb):
            lo = jb * MIX_BLOCK
            cs = slice(lo, lo + MIX_BLOCK)
            need(("conv", jb))

            def proj(off):
                return (_dot(hb_s[...], w_xbc[:, off + lo:off + lo + MIX_BLOCK])
                        + b_xbc[:, off + lo:off + lo + MIX_BLOCK])

            xin = proj(0)
            bg = proj(CONV_DIM)
            cg = proj(2 * CONV_DIM)
            ubuf[SUBLANES:SUBLANES + tm, cs] = cg * xin
            u0 = ubuf[SUBLANES:SUBLANES + tm, cs]
            u1 = ubuf[SUBLANES - 1:SUBLANES - 1 + tm, cs]
            u2 = ubuf[SUBLANES - 2:SUBLANES - 2 + tm, cs]
            y = bg * (u2 * cw_ref[0, :, cs] + u1 * cw_ref[1, :, cs] + u0 * cw_ref[2, :, cs])
            y_s[:, cs] = y.astype(BF16)
            ubuf[0:SUBLANES, cs] = ubuf[tm:tm + SUBLANES, cs]

        def gate_block(jb):
            cs = slice(jb * MIX_BLOCK, (jb + 1) * MIX_BLOCK)
            need(("gate", jb))
            sg_s[:, cs] = jax.nn.sigmoid(_dot(hb_s[...], w_gt[:, cs]) + b_gt[:, cs])

        def pconv_block(jb):
            cs = slice(jb * MIX_BLOCK, (jb + 1) * MIX_BLOCK)
            need("pc")
            sg_s[:, cs] = sg_s[:, cs] * _dot(y_s[...], w_pc[:, cs])

        mxu_items = ([functools.partial(conv_block, jb) for jb in range(nb)]
                     + [functools.partial(gate_block, jb) for jb in range(2 * nb)]
                     + [functools.partial(pconv_block, jb) for jb in range(nb)])

        pairs = [(ci, hd) for ci in range(n_chunks) for hd in range(HEADS)]
        ahead = pair_weights(*pairs[0])
        for idx, (ci, hd) in enumerate(pairs):
            ready = ahead
            if idx + 1 < len(pairs):
                ahead = pair_weights(*pairs[idx + 1])
            mlstm_pair(ci, hd, *ready)
            lo = idx * len(mxu_items) // len(pairs)
            hi = (idx + 1) * len(mxu_items) // len(pairs)
            for item in mxu_items[lo:hi]:
                item()

        parts = []
        for hd in range(HEADS):
            hm_h = hm_s[:, hd * DV:(hd + 1) * DV]
            parts.append(_rms_scale(hm_h) * hg_ref[hd:hd + 1, :])
        hm = og_s[...] * jnp.concatenate(parts, axis=1)
        need("pm")
        p_m = _dot(hm.astype(BF16), w_pm[...])
        z = sg_s[:, 0:D_MODEL] + sg_s[:, D_MODEL:2 * D_MODEL] * p_m
        need("wo")
        o_ref[0] = x + gt1 * _dot(z.astype(BF16), w_o[...])

    @pl.when(jnp.logical_and(pl.program_id(0) == 0, j == 0))
    def _():
        _LazyStage(weight_groups(), stage, stage_sem).need("wo")

    step(lambda key: None)


def _const_spec(shape):
    nd = len(shape)
    return pl.BlockSpec(shape, lambda b, j: (0,) * nd, pipeline_mode=pl.Buffered(1))


def _lane_window_spec(width, block):
    return pl.BlockSpec((1, width), lambda b, j: (0, block), pipeline_mode=pl.Buffered(1))


def _mixer_call(x, mod, g_norm, w_t, b_in, b_gt, conv_w, head_g, w_pc, w_pm, w_o):
    bsz, seq, d = x.shape
    tm = TM_MIX
    n_xbc, n_qk, n_vo, n_gt = 3 * CONV_DIM, 2 * HEADS * DQK, 2 * MLSTM_DIM, 2 * D_MODEL
    bias_windows = [_lane_window_spec(n_xbc, 0), _lane_window_spec(n_qk, n_xbc // n_qk),
                    _lane_window_spec(n_vo, (n_xbc + n_qk) // n_vo),
                    _const_spec(b_gt.shape),
                    _lane_window_spec(CHUNK, (n_xbc + n_qk + n_vo) // CHUNK)]
    hbm = pl.BlockSpec(memory_space=pl.ANY)
    return pl.pallas_call(
        _mixer_kernel,
        grid=(bsz, seq // tm),
        in_specs=[
            pl.BlockSpec((1, tm, d), lambda b, j: (b, j, 0)),
            _const_spec(mod.shape), _const_spec(g_norm.shape), hbm,
        ] + bias_windows + [_const_spec(conv_w.shape), _const_spec(head_g.shape), hbm, hbm, hbm],
        out_specs=pl.BlockSpec((1, tm, d), lambda b, j: (b, j, 0)),
        out_shape=jax.ShapeDtypeStruct(x.shape, F32),
        scratch_shapes=[
            pltpu.VMEM((d, n_xbc), BF16),
            pltpu.VMEM((d, n_qk), BF16),
            pltpu.VMEM((d, n_vo), BF16),
            pltpu.VMEM((d, n_gt), BF16),
            pltpu.VMEM((d, CHUNK), BF16),
            pltpu.VMEM((CONV_DIM, d), BF16),
            pltpu.VMEM((MLSTM_DIM, d), BF16),
            pltpu.VMEM((d, d), BF16),
            pltpu.VMEM((STAGE_SLOTS, STAGE_ROWS, d), F32),
            pltpu.SemaphoreType.DMA((STAGE_SLOTS,)),
            pltpu.VMEM((tm, D_MODEL), BF16),
            pltpu.VMEM((tm + SUBLANES, CONV_DIM), F32),
            pltpu.VMEM((tm, CONV_DIM), BF16),
            pltpu.VMEM((tm, 2 * D_MODEL), F32),
            pltpu.VMEM((tm, HEADS * DQK), F32),
            pltpu.VMEM((tm, HEADS * DQK), F32),
            pltpu.VMEM((tm, HEADS * DQK), BF16),
            pltpu.VMEM((tm, HEADS * DQK), BF16),
            pltpu.VMEM((tm, MLSTM_DIM), BF16),
            pltpu.VMEM((tm, MLSTM_DIM), F32),
            pltpu.VMEM((tm, MLSTM_DIM), F32),
            pltpu.VMEM((HEADS, DQK, DV), F32),
            pltpu.VMEM((SUBLANES, DQK), F32),
            pltpu.VMEM((SUBLANES, CHUNK), F32),
        ],
        compiler_params=pltpu.CompilerParams(
            dimension_semantics=("arbitrary", "arbitrary"),
            vmem_limit_bytes=VMEM_LIMIT_BYTES),
        name="mixer_sublayer",
    )(x, mod, g_norm, w_t, b_in, b_in, b_in, b_gt, b_in, conv_w, head_g, w_pc, w_pm, w_o)


def _ffn_kernel(x_ref, mod_ref, g_ref, w_up_hbm, cw_ref, w_d_hbm, gf_ref, o_ref,
                w_a, w_g, w_d, stage, stage_sem, abuf, *, final_norm):
    tm = TM_FFN
    j = pl.program_id(1)

    def weight_groups():
        groups = []
        for half, (key, dst) in enumerate((("a", w_a), ("g", w_g))):
            jobs = []
            for c in range(0, D_FF, FFN_STAGE_COLS):
                for r in range(0, D_MODEL, STAGE_ROWS):
                    jobs.append((w_up_hbm.at[pl.ds(r, STAGE_ROWS), pl.ds(half * D_FF + c, FFN_STAGE_COLS)],
                                 dst.at[pl.ds(r, STAGE_ROWS), pl.ds(c, FFN_STAGE_COLS)]))
            groups.append((key, jobs))
        groups.append(("d", _row_jobs(w_d_hbm, 0, w_d, D_FF)))
        return groups

    @pl.when(j == 0)
    def _():
        abuf[0:SUBLANES, :] = jnp.zeros((SUBLANES, D_FF), F32)

    def step(need):
        x = x_ref[0]
        mod_row = pl.ds(pl.program_id(0), 1)
        sh2 = mod_ref[mod_row, 3 * D_MODEL:4 * D_MODEL]
        sc2 = mod_ref[mod_row, 4 * D_MODEL:5 * D_MODEL]
        gt2 = mod_ref[mod_row, 5 * D_MODEL:6 * D_MODEL]
        h = (_rms_scale(x) * g_ref[...]) * (1.0 + sc2) + sh2
        hb = h.astype(BF16)
        need("a")
        abuf[SUBLANES:SUBLANES + tm, :] = _dot(hb, w_a[...])
        need("g")
        gate = _dot(hb, w_g[...])
        a0 = abuf[SUBLANES:SUBLANES + tm, :]
        a1 = abuf[SUBLANES - 1:SUBLANES - 1 + tm, :]
        a2 = abuf[SUBLANES - 2:SUBLANES - 2 + tm, :]
        ac = a2 * cw_ref[0] + a1 * cw_ref[1] + a0 * cw_ref[2]
        abuf[0:SUBLANES, :] = abuf[tm:tm + SUBLANES, :]
        act = (ac * jax.nn.sigmoid(ac)) * gate
        need("d")
        x2 = x + gt2 * _dot(act.astype(BF16), w_d[...])
        o_ref[0] = _rms_scale(x2) * gf_ref[...] if final_norm else x2

    first = jnp.logical_and(pl.program_id(0) == 0, j == 0)

    @pl.when(first)
    def _():
        step(_LazyStage(weight_groups(), stage, stage_sem).need)

    @pl.when(jnp.logical_not(first))
    def _():
        step(lambda key: None)


def _ffn_call(x, mod, g_norm, w_up, conv_w, w_d, g_final, final_norm):
    bsz, seq, d = x.shape
    tm = TM_FFN
    hbm = pl.BlockSpec(memory_space=pl.ANY)
    return pl.pallas_call(
        functools.partial(_ffn_kernel, final_norm=final_norm),
        grid=(bsz, seq // tm),
        in_specs=[
            pl.BlockSpec((1, tm, d), lambda b, j: (b, j, 0)),
            _const_spec(mod.shape), _const_spec(g_norm.shape), hbm, _const_spec(conv_w.shape), hbm,
            _const_spec(g_final.shape),
        ],
        out_specs=pl.BlockSpec((1, tm, d), lambda b, j: (b, j, 0)),
        out_shape=jax.ShapeDtypeStruct(x.shape, F32),
        scratch_shapes=[
            pltpu.VMEM((d, D_FF), BF16),
            pltpu.VMEM((d, D_FF), BF16),
            pltpu.VMEM((D_FF, d), BF16),
            pltpu.VMEM((STAGE_SLOTS, STAGE_ROWS, FFN_STAGE_COLS), F32),
            pltpu.SemaphoreType.DMA((STAGE_SLOTS,)),
            pltpu.VMEM((tm + SUBLANES, D_FF), F32),
        ],
        compiler_params=pltpu.CompilerParams(
            dimension_semantics=("arbitrary", "arbitrary"),
            vmem_limit_bytes=VMEM_LIMIT_BYTES),
        name="ffn_sublayer",
    )(x, mod, g_norm, w_up, conv_w, w_d, g_final)


def kernel(x, c, w_ada, b_ada, g_norm_mix, w_in, b_in, conv_mix_w, mlstm_head_g,
           w_proj_conv, w_proj_mlstm, w_out, g_norm_ffn, w_up, conv_ffn_w, w_down, g_final):
    depth = w_ada.shape[0]
    bsz = x.shape[0]
    assert bsz <= SUBLANES
    o_gt = 3 * CONV_DIM + 2 * HEADS * DQK + 2 * MLSTM_DIM + 2 * HEADS

    for l in range(depth):
        mod = _ada_call(c, w_ada[l], b_ada[l][None, :])

        bi = b_in[l]
        x = _mixer_call(
            x, mod, g_norm_mix[l][None, :],
            w_in[l].T, bi[None, :], bi[None, o_gt:],
            conv_mix_w[l][:, None, :], mlstm_head_g[l],
            w_proj_conv[l], w_proj_mlstm[l], w_out[l])

        x = _ffn_call(
            x, mod, g_norm_ffn[l][None, :], w_up[l],
            conv_ffn_w[l][:, None, :], w_down[l], g_final[None, :],
            final_norm=(l == depth - 1))
    return x
```

```python
import functools

import jax
import jax.numpy as jnp
from jax import lax
from jax.experimental import pallas as pl
from jax.experimental.pallas import tpu as pltpu

F32 = jnp.float32
BF16 = jnp.bfloat16

D_MODEL = 1024
CONV_DIM = 1024
HEADS = 4
DQK = 128
DV = 256
MLSTM_DIM = HEADS * DV
CHUNK = 128
D_FF = 2816
N_MOD = 6
EPS = 1e-6

SUBLANES = 8
TM_MIX = 512
TM_FFN = 512
MIX_BLOCK = 256
STAGE_SLOTS = 8
STAGE_ROWS = 256
FFN_STAGE_COLS = D_FF // 2
ADA_BLOCK = 3072
V7X_VMEM_BYTES = 64 * 1024 * 1024
VMEM_LIMIT_BYTES = V7X_VMEM_BYTES * 7 // 8


def _dot(a, b):
    return jnp.dot(a, b, preferred_element_type=F32)


def _dot_nt(a, b):
    return lax.dot_general(a, b, (((1,), (1,)), ((), ())), preferred_element_type=F32)


def _rms_scale(x):
    return x * lax.rsqrt(jnp.mean(x * x, axis=-1, keepdims=True) + EPS)


def _log_sigmoid(x):
    return -(jnp.maximum(-x, 0.0) + jnp.log1p(jnp.exp(-jnp.abs(x))))


def _stage_to_bf16(jobs, stage, sem):
    def copy(i):
        src = jobs[i][0]
        rows, cols = src.shape
        slot = i % STAGE_SLOTS
        return pltpu.make_async_copy(src, stage.at[slot, pl.ds(0, rows), pl.ds(0, cols)], sem.at[slot])

    for i in range(min(STAGE_SLOTS, len(jobs))):
        copy(i).start()
    for i, (src, dst) in enumerate(jobs):
        rows, cols = src.shape
        copy(i).wait()
        chunk = stage[i % STAGE_SLOTS, 0:rows, 0:cols]
        if dst.shape != src.shape:
            assert dst.shape == (cols, rows)
            chunk = chunk.T
        dst[...] = chunk.astype(BF16)
        if i + STAGE_SLOTS < len(jobs):
            copy(i + STAGE_SLOTS).start()


def _row_jobs(src, src_row0, dst, n_rows):
    step = min(STAGE_ROWS, n_rows)
    return [(src.at[pl.ds(src_row0 + r, step), :], dst.at[pl.ds(r, step), :])
            for r in range(0, n_rows, step)]


def _transposed_jobs(src, src_row0, dst, n_rows):
    step = min(STAGE_ROWS, n_rows)
    return [(src.at[pl.ds(src_row0 + r, step), :], dst.at[:, pl.ds(r, step)])
            for r in range(0, n_rows, step)]


def _ada_kernel(c_ref, w_ref, b_ref, o_ref):
    c = c_ref[...]
    act = c * jax.nn.sigmoid(c)
    pad = jnp.zeros((SUBLANES - act.shape[0], act.shape[1]), F32)
    act = jnp.concatenate([act, pad], axis=0)
    o_ref[...] = _dot(act.astype(BF16), w_ref[...].astype(BF16)) + b_ref[...]


def _ada_call(c, w_ada, b_ada):
    n = w_ada.shape[1]
    tn = ADA_BLOCK
    return pl.pallas_call(
        _ada_kernel,
        grid=(n // tn,),
        in_specs=[
            pl.BlockSpec(c.shape, lambda i: (0, 0)),
            pl.BlockSpec((D_MODEL, tn), lambda i: (0, i)),
            pl.BlockSpec((1, tn), lambda i: (0, i)),
        ],
        out_specs=pl.BlockSpec((SUBLANES, tn), lambda i: (0, i)),
        out_shape=jax.ShapeDtypeStruct((SUBLANES, n), F32),
        compiler_params=pltpu.CompilerParams(dimension_semantics=("arbitrary",)),
        name="adaln_mod",
    )(c, w_ada, b_ada)


def _mixer_kernel(x_ref, mod_ref, g_ref, w_in_hbm, b_xbc, b_qk, b_vo, b_gt, b_if,
                  cw_ref, hg_ref, w_pc_hbm, w_pm_hbm, w_o_hbm,
                  o_ref,
                  w_xbc, w_qk, w_vo, w_gt, w_if, w_pc, w_pm, w_o, stage, stage_sem,
                  hb_s, ubuf, y_s, sg_s, qs_s, k_s, qb_s, kb_s, vb_s, og_s, hm_s, c_s, n_s, m_s):
    tm = TM_MIX
    n_chunks = tm // CHUNK
    j = pl.program_id(1)
    n_xbc, n_qk, n_vo, n_gt = 3 * CONV_DIM, 2 * HEADS * DQK, 2 * MLSTM_DIM, 2 * D_MODEL

    @pl.when(jnp.logical_and(pl.program_id(0) == 0, j == 0))
    def _():
        o_if = n_xbc + n_qk + n_vo
        jobs = (_transposed_jobs(w_in_hbm, o_if, w_if, CHUNK)
                + _transposed_jobs(w_in_hbm, n_xbc, w_qk, n_qk)
                + _transposed_jobs(w_in_hbm, n_xbc + n_qk, w_vo, n_vo)
                + _transposed_jobs(w_in_hbm, 0, w_xbc, n_xbc)
                + _transposed_jobs(w_in_hbm, o_if + 2 * HEADS, w_gt, n_gt)
                + _row_jobs(w_pc_hbm, 0, w_pc, CONV_DIM)
                + _row_jobs(w_pm_hbm, 0, w_pm, MLSTM_DIM)
                + _row_jobs(w_o_hbm, 0, w_o, D_MODEL))
        _stage_to_bf16(jobs, stage, stage_sem)

    @pl.when(j == 0)
    def _():
        ubuf[0:SUBLANES, :] = jnp.zeros((SUBLANES, CONV_DIM), F32)
        c_s[...] = jnp.zeros_like(c_s)
        n_s[...] = jnp.zeros_like(n_s)
        m_s[...] = jnp.zeros_like(m_s)

    x = x_ref[0]
    mod_row = pl.ds(pl.program_id(0), 1)
    sh1 = mod_ref[mod_row, 0:D_MODEL]
    sc1 = mod_ref[mod_row, D_MODEL:2 * D_MODEL]
    gt1 = mod_ref[mod_row, 2 * D_MODEL:3 * D_MODEL]
    h = (_rms_scale(x) * g_ref[...]) * (1.0 + sc1) + sh1
    hb_s[...] = h.astype(BF16)


    gif = _dot(hb_s[...], w_if[...]) + b_if[...]
    lane8 = lax.broadcasted_iota(jnp.int32, (SUBLANES, CHUNK), 1)
    head_rows = lax.broadcasted_iota(jnp.int32, (SUBLANES, CHUNK), 0) < HEADS

    def lane_scan(v, op, fill):
        d = 1
        while d < CHUNK:
            v = op(v, jnp.where(lane8 >= d, pltpu.roll(v, d, axis=1), fill))
            d *= 2
        return v

    m_run = m_s[...]
    gate_rows = []
    for ci in range(n_chunks):
        gi = gif[ci * CHUNK:(ci + 1) * CHUNK, :].T[0:SUBLANES, :]
        li = jnp.where(head_rows, gi, 0.0)
        lf = jnp.where(head_rows, _log_sigmoid(pltpu.roll(gi, HEADS, axis=0)), 0.0)
        b = lane_scan(lf, jnp.add, 0.0)
        g = b[:, CHUNK - 1:CHUNK]
        r = li - b
        cm = lane_scan(r, jnp.maximum, -jnp.inf)
        r_max = cm[:, CHUNK - 1:CHUNK]
        m_prev = m_run[:, 0:1]
        big_m = jnp.maximum(m_prev, cm)
        m_loc = g + r_max
        m_new = jnp.maximum(g + m_prev, m_loc)
        gate_rows.append(dict(r=r, big_m=big_m, w=jnp.exp(r - r_max), em=jnp.exp(-(b + big_m)), m_prev=m_prev,
                              s_old=jnp.exp(g + m_prev - m_new), s_new=jnp.exp(m_loc - m_new)))
        m_run = jnp.broadcast_to(m_new, (SUBLANES, CHUNK))
    m_s[...] = m_run

    nq = HEADS * DQK
    qs = (_dot(hb_s[...], w_qk[:, 0:nq]) + b_qk[:, 0:nq]) * (DQK ** -0.5)
    qs_s[...] = qs
    qb_s[...] = qs.astype(BF16)
    k = _dot(hb_s[...], w_qk[:, nq:2 * nq]) + b_qk[:, nq:2 * nq]
    k_s[...] = k
    kb_s[...] = k.astype(BF16)
    vb_s[...] = (_dot(hb_s[...], w_vo[:, 0:MLSTM_DIM]) + b_vo[:, 0:MLSTM_DIM]).astype(BF16)
    og_s[...] = jax.nn.sigmoid(_dot(hb_s[...], w_vo[:, MLSTM_DIM:2 * MLSTM_DIM])
                               + b_vo[:, MLSTM_DIM:2 * MLSTM_DIM])

    row = lax.broadcasted_iota(jnp.int32, (CHUNK, CHUNK), 0)
    col = lax.broadcasted_iota(jnp.int32, (CHUNK, CHUNK), 1)
    tri = col <= row

    def token_major(row_vec):
        return jnp.broadcast_to(row_vec, (CHUNK, CHUNK)).T

    def pair_weights(ci, hd):
        rs = slice(ci * CHUNK, (ci + 1) * CHUNK)
        hq = slice(hd * DQK, (hd + 1) * DQK)
        gr = gate_rows[ci]
        m_bc = token_major(gr["big_m"][hd:hd + 1, :])
        w_bc = token_major(gr["w"][hd:hd + 1, :])
        r_row = gr["r"][hd:hd + 1, :]
        wts = jnp.where(tri, jnp.exp(r_row - m_bc), 0.0) * _dot_nt(qb_s[rs, hq], kb_s[rs, hq])
        kw = w_bc * k_s[rs, hq]
        return wts, kw.T.astype(BF16), jnp.sum(kw, axis=0, keepdims=True), m_bc

    def mlstm_pair(ci, hd, wts, kw_t, n_loc, m_bc):
        rs = slice(ci * CHUNK, (ci + 1) * CHUNK)
        hq = slice(hd * DQK, (hd + 1) * DQK)
        hv = slice(hd * DV, (hd + 1) * DV)
        gr = gate_rows[ci]
        em_bc = token_major(gr["em"][hd:hd + 1, :])
        si_bc = jnp.exp(gr["m_prev"][hd:hd + 1, :] - m_bc)
        s_old = gr["s_old"][hd:hd + 1, :]
        s_new = gr["s_new"][hd:hd + 1, :]
        n_prev = n_s[hd:hd + 1, :]
        c_prev = c_s[hd]
        qb = qb_s[rs, hq]
        vb = vb_s[rs, hv]

        num = (_dot(wts.astype(BF16), vb)
               + jnp.concatenate([si_bc, si_bc], axis=1) * _dot(qb, c_prev.astype(BF16)))
        den = (jnp.sum(wts, axis=1, keepdims=True)
               + si_bc * jnp.sum(qs_s[rs, hq] * n_prev, axis=1, keepdims=True))
        inv = 1.0 / jnp.maximum(jnp.abs(den), em_bc)
        hm_s[rs, hv] = num * jnp.concatenate([inv, inv], axis=1)

        c_loc = _dot(kw_t, vb)
        c_s[hd] = s_old * c_prev + s_new * c_loc
        n_s[hd:hd + 1, :] = s_old * n_prev + s_new * n_loc

    def conv_block(jb):
        lo = jb * MIX_BLOCK
        cs = slice(lo, lo + MIX_BLOCK)

        def proj(off):
            return (_dot(hb_s[...], w_xbc[:, off + lo:off + lo + MIX_BLOCK])
                    + b_xbc[:, off + lo:off + lo + MIX_BLOCK])

        xin = proj(0)
        bg = proj(CONV_DIM)
        cg = proj(2 * CONV_DIM)
        ubuf[SUBLANES:SUBLANES + tm, cs] = cg * xin
        u0 = ubuf[SUBLANES:SUBLANES + tm, cs]
        u1 = ubuf[SUBLANES - 1:SUBLANES - 1 + tm, cs]
        u2 = ubuf[SUBLANES - 2:SUBLANES - 2 + tm, cs]
        y = bg * (u2 * cw_ref[0, :, cs] + u1 * cw_ref[1, :, cs] + u0 * cw_ref[2, :, cs])
        y_s[:, cs] = y.astype(BF16)
        ubuf[0:SUBLANES, cs] = ubuf[tm:tm + SUBLANES, cs]

    def gate_block(jb):
        cs = slice(jb * MIX_BLOCK, (jb + 1) * MIX_BLOCK)
        sg_s[:, cs] = jax.nn.sigmoid(_dot(hb_s[...], w_gt[:, cs]) + b_gt[:, cs])

    def pconv_block(jb):
        cs = slice(jb * MIX_BLOCK, (jb + 1) * MIX_BLOCK)
        sg_s[:, cs] = sg_s[:, cs] * _dot(y_s[...], w_pc[:, cs])

    nb = CONV_DIM // MIX_BLOCK
    mxu_items = ([functools.partial(conv_block, jb) for jb in range(nb)]
                 + [functools.partial(gate_block, jb) for jb in range(2 * nb)]
                 + [functools.partial(pconv_block, jb) for jb in range(nb)])

    pairs = [(ci, hd) for ci in range(n_chunks) for hd in range(HEADS)]
    ahead = pair_weights(*pairs[0])
    for idx, (ci, hd) in enumerate(pairs):
        ready = ahead
        if idx + 1 < len(pairs):
            ahead = pair_weights(*pairs[idx + 1])
        mlstm_pair(ci, hd, *ready)
        lo = idx * len(mxu_items) // len(pairs)
        hi = (idx + 1) * len(mxu_items) // len(pairs)
        for item in mxu_items[lo:hi]:
            item()

    parts = []
    for hd in range(HEADS):
        hm_h = hm_s[:, hd * DV:(hd + 1) * DV]
        parts.append(_rms_scale(hm_h) * hg_ref[hd:hd + 1, :])
    hm = og_s[...] * jnp.concatenate(parts, axis=1)
    p_m = _dot(hm.astype(BF16), w_pm[...])
    z = sg_s[:, 0:D_MODEL] + sg_s[:, D_MODEL:2 * D_MODEL] * p_m
    o_ref[0] = x + gt1 * _dot(z.astype(BF16), w_o[...])


def _const_spec(shape):
    nd = len(shape)
    return pl.BlockSpec(shape, lambda b, j: (0,) * nd, pipeline_mode=pl.Buffered(1))


def _lane_window_spec(width, block):
    return pl.BlockSpec((1, width), lambda b, j: (0, block), pipeline_mode=pl.Buffered(1))


def _mixer_call(x, mod, g_norm, w_t, b_in, b_gt, conv_w, head_g, w_pc, w_pm, w_o):
    bsz, seq, d = x.shape
    tm = TM_MIX
    n_xbc, n_qk, n_vo, n_gt = 3 * CONV_DIM, 2 * HEADS * DQK, 2 * MLSTM_DIM, 2 * D_MODEL
    bias_windows = [_lane_window_spec(n_xbc, 0), _lane_window_spec(n_qk, n_xbc // n_qk),
                    _lane_window_spec(n_vo, (n_xbc + n_qk) // n_vo),
                    _const_spec(b_gt.shape),
                    _lane_window_spec(CHUNK, (n_xbc + n_qk + n_vo) // CHUNK)]
    hbm = pl.BlockSpec(memory_space=pl.ANY)
    return pl.pallas_call(
        _mixer_kernel,
        grid=(bsz, seq // tm),
        in_specs=[
            pl.BlockSpec((1, tm, d), lambda b, j: (b, j, 0)),
            _const_spec(mod.shape), _const_spec(g_norm.shape), hbm,
        ] + bias_windows + [_const_spec(conv_w.shape), _const_spec(head_g.shape), hbm, hbm, hbm],
        out_specs=pl.BlockSpec((1, tm, d), lambda b, j: (b, j, 0)),
        out_shape=jax.ShapeDtypeStruct(x.shape, F32),
        scratch_shapes=[
            pltpu.VMEM((d, n_xbc), BF16),
            pltpu.VMEM((d, n_qk), BF16),
            pltpu.VMEM((d, n_vo), BF16),
            pltpu.VMEM((d, n_gt), BF16),
            pltpu.VMEM((d, CHUNK), BF16),
            pltpu.VMEM((CONV_DIM, d), BF16),
            pltpu.VMEM((MLSTM_DIM, d), BF16),
            pltpu.VMEM((d, d), BF16),
            pltpu.VMEM((STAGE_SLOTS, STAGE_ROWS, d), F32),
            pltpu.SemaphoreType.DMA((STAGE_SLOTS,)),
            pltpu.VMEM((tm, D_MODEL), BF16),
            pltpu.VMEM((tm + SUBLANES, CONV_DIM), F32),
            pltpu.VMEM((tm, CONV_DIM), BF16),
            pltpu.VMEM((tm, 2 * D_MODEL), F32),
            pltpu.VMEM((tm, HEADS * DQK), F32),
            pltpu.VMEM((tm, HEADS * DQK), F32),
            pltpu.VMEM((tm, HEADS * DQK), BF16),
            pltpu.VMEM((tm, HEADS * DQK), BF16),
            pltpu.VMEM((tm, MLSTM_DIM), BF16),
            pltpu.VMEM((tm, MLSTM_DIM), F32),
            pltpu.VMEM((tm, MLSTM_DIM), F32),
            pltpu.VMEM((HEADS, DQK, DV), F32),
            pltpu.VMEM((SUBLANES, DQK), F32),
            pltpu.VMEM((SUBLANES, CHUNK), F32),
        ],
        compiler_params=pltpu.CompilerParams(
            dimension_semantics=("arbitrary", "arbitrary"),
            vmem_limit_bytes=VMEM_LIMIT_BYTES),
        name="mixer_sublayer",
    )(x, mod, g_norm, w_t, b_in, b_in, b_in, b_gt, b_in, conv_w, head_g, w_pc, w_pm, w_o)


def _ffn_stage_weights(w_up_hbm, w_d_hbm, w_up, w_d, stage, sem):
    n_rb = D_MODEL // STAGE_ROWS
    n_up = n_rb * (2 * D_FF // FFN_STAGE_COLS)
    n_all = n_up + D_FF // STAGE_ROWS

    def up_window(i):
        r = pl.multiple_of((i % n_rb) * STAGE_ROWS, STAGE_ROWS)
        c = pl.multiple_of((i // n_rb) * FFN_STAGE_COLS, 128)
        return pl.ds(r, STAGE_ROWS), pl.ds(c, FFN_STAGE_COLS)

    def down_rows(i):
        return pl.ds(pl.multiple_of((i - n_up) * STAGE_ROWS, STAGE_ROWS), STAGE_ROWS)

    def up_copy(i, slot):
        rows, cols = up_window(i)
        return pltpu.make_async_copy(w_up_hbm.at[rows, cols], stage.at[slot], sem.at[slot])

    def down_copy(i, slot):
        return pltpu.make_async_copy(w_d_hbm.at[down_rows(i), :],
                                     stage.at[slot, :, pl.ds(0, D_MODEL)], sem.at[slot])

    def start(i, slot):
        @pl.when(i < n_up)
        def _():
            up_copy(i, slot).start()

        @pl.when(i >= n_up)
        def _():
            down_copy(i, slot).start()

    def prime(i, carry):
        start(i, i)
        return carry

    def finish(i, carry):
        slot = i % STAGE_SLOTS

        @pl.when(i < n_up)
        def _():
            up_copy(i, slot).wait()
            rows, cols = up_window(i)
            w_up[rows, cols] = stage[slot].astype(BF16)

        @pl.when(i >= n_up)
        def _():
            down_copy(i, slot).wait()
            w_d[down_rows(i), :] = stage[slot, :, 0:D_MODEL].astype(BF16)

        @pl.when(i + STAGE_SLOTS < n_all)
        def _():
            start(i + STAGE_SLOTS, slot)

        return carry

    lax.fori_loop(0, STAGE_SLOTS, prime, 0)
    lax.fori_loop(0, n_all, finish, 0)


def _ffn_kernel(x_ref, mod_ref, g_ref, w_up_hbm, cw_ref, w_d_hbm, gf_ref, o_ref,
                w_up, w_d, stage, stage_sem, abuf, *, final_norm):
    tm = TM_FFN
    j = pl.program_id(1)

    @pl.when(jnp.logical_and(pl.program_id(0) == 0, j == 0))
    def _():
        _ffn_stage_weights(w_up_hbm, w_d_hbm, w_up, w_d, stage, stage_sem)

    @pl.when(j == 0)
    def _():
        abuf[0:SUBLANES, :] = jnp.zeros((SUBLANES, D_FF), F32)

    x = x_ref[0]
    mod_row = pl.ds(pl.program_id(0), 1)
    sh2 = mod_ref[mod_row, 3 * D_MODEL:4 * D_MODEL]
    sc2 = mod_ref[mod_row, 4 * D_MODEL:5 * D_MODEL]
    gt2 = mod_ref[mod_row, 5 * D_MODEL:6 * D_MODEL]
    h = (_rms_scale(x) * g_ref[...]) * (1.0 + sc2) + sh2
    hb = h.astype(BF16)
    abuf[SUBLANES:SUBLANES + tm, :] = _dot(hb, w_up[:, 0:D_FF])
    gate = _dot(hb, w_up[:, D_FF:2 * D_FF])
    a0 = abuf[SUBLANES:SUBLANES + tm, :]
    a1 = abuf[SUBLANES - 1:SUBLANES - 1 + tm, :]
    a2 = abuf[SUBLANES - 2:SUBLANES - 2 + tm, :]
    ac = a2 * cw_ref[0] + a1 * cw_ref[1] + a0 * cw_ref[2]
    abuf[0:SUBLANES, :] = abuf[tm:tm + SUBLANES, :]
    act = (ac * jax.nn.sigmoid(ac)) * gate
    x2 = x + gt2 * _dot(act.astype(BF16), w_d[...])
    o_ref[0] = _rms_scale(x2) * gf_ref[...] if final_norm else x2


def _ffn_call(x, mod, g_norm, w_up, conv_w, w_d, g_final, final_norm):
    bsz, seq, d = x.shape
    tm = TM_FFN
    hbm = pl.BlockSpec(memory_space=pl.ANY)
    return pl.pallas_call(
        functools.partial(_ffn_kernel, final_norm=final_norm),
        grid=(bsz, seq // tm),
        in_specs=[
            pl.BlockSpec((1, tm, d), lambda b, j: (b, j, 0)),
            _const_spec(mod.shape), _const_spec(g_norm.shape), hbm, _const_spec(conv_w.shape), hbm,
            _const_spec(g_final.shape),
        ],
        out_specs=pl.BlockSpec((1, tm, d), lambda b, j: (b, j, 0)),
        out_shape=jax.ShapeDtypeStruct(x.shape, F32),
        scratch_shapes=[
            pltpu.VMEM((d, 2 * D_FF), BF16),
            pltpu.VMEM((D_FF, d), BF16),
            pltpu.VMEM((STAGE_SLOTS, STAGE_ROWS, FFN_STAGE_COLS), F32),
            pltpu.SemaphoreType.DMA((STAGE_SLOTS,)),
            pltpu.VMEM((tm + SUBLANES, D_FF), F32),
        ],
        compiler_params=pltpu.CompilerParams(
            dimension_semantics=("arbitrary", "arbitrary"),
            vmem_limit_bytes=VMEM_LIMIT_BYTES),
        name="ffn_sublayer",
    )(x, mod, g_norm, w_up, conv_w, w_d, g_final)


def kernel(x, c, w_ada, b_ada, g_norm_mix, w_in, b_in, conv_mix_w, mlstm_head_g,
           w_proj_conv, w_proj_mlstm, w_out, g_norm_ffn, w_up, conv_ffn_w, w_down, g_final):
    depth = w_ada.shape[0]
    bsz = x.shape[0]
    assert bsz <= SUBLANES
    o_gt = 3 * CONV_DIM + 2 * HEADS * DQK + 2 * MLSTM_DIM + 2 * HEADS

    for l in range(depth):
        mod = _ada_call(c, w_ada[l], b_ada[l][None, :])

        bi = b_in[l]
        x = _mixer_call(
            x, mod, g_norm_mix[l][None, :],
            w_in[l].T, bi[None, :], bi[None, o_gt:],
            conv_mix_w[l][:, None, :], mlstm_head_g[l],
            w_proj_conv[l], w_proj_mlstm[l], w_out[l])

        x = _ffn_call(
            x, mod, g_norm_ffn[l][None, :], w_up[l],
            conv_ffn_w[l][:, None, :], w_down[l], g_final[None, :],
            final_norm=(l == depth - 1))
    return x
```

```python
import functools

import jax
import jax.numpy as jnp
from jax import lax
from jax.experimental import pallas as pl
from jax.experimental.pallas import tpu as pltpu

F32 = jnp.float32
BF16 = jnp.bfloat16

D_MODEL = 1024
CONV_DIM = 1024
HEADS = 4
DQK = 128
DV = 256
MLSTM_DIM = HEADS * DV
CHUNK = 128
D_FF = 2816
N_MOD = 6
EPS = 1e-6

SUBLANES = 8
TM_MIX = 512
TM_FFN = 512
MIX_BLOCK = 256
STAGE_SLOTS = 8
STAGE_ROWS = 256
FFN_STAGE_COLS = D_FF // 2
ADA_BLOCK = 3072
V7X_VMEM_BYTES = 64 * 1024 * 1024
VMEM_LIMIT_BYTES = V7X_VMEM_BYTES * 7 // 8


def _dot(a, b):
    return jnp.dot(a, b, preferred_element_type=F32)


def _dot_nt(a, b):
    return lax.dot_general(a, b, (((1,), (1,)), ((), ())), preferred_element_type=F32)


def _rms_scale(x):
    return x * lax.rsqrt(jnp.mean(x * x, axis=-1, keepdims=True) + EPS)


def _log_sigmoid(x):
    return -(jnp.maximum(-x, 0.0) + jnp.log1p(jnp.exp(-jnp.abs(x))))


def _ada_kernel(c_ref, w_ref, b_ref, o_ref):
    c = c_ref[...]
    act = c * jax.nn.sigmoid(c)
    pad = jnp.zeros((SUBLANES - act.shape[0], act.shape[1]), F32)
    act = jnp.concatenate([act, pad], axis=0)
    o_ref[...] = _dot(act.astype(BF16), w_ref[...].astype(BF16)) + b_ref[...]


def _ada_call(c, w_ada, b_ada):
    n = w_ada.shape[1]
    tn = ADA_BLOCK
    return pl.pallas_call(
        _ada_kernel,
        grid=(n // tn,),
        in_specs=[
            pl.BlockSpec(c.shape, lambda i: (0, 0)),
            pl.BlockSpec((D_MODEL, tn), lambda i: (0, i)),
            pl.BlockSpec((1, tn), lambda i: (0, i)),
        ],
        out_specs=pl.BlockSpec((SUBLANES, tn), lambda i: (0, i)),
        out_shape=jax.ShapeDtypeStruct((SUBLANES, n), F32),
        compiler_params=pltpu.CompilerParams(dimension_semantics=("arbitrary",)),
        name="adaln_mod",
    )(c, w_ada, b_ada)


def _mixer_stage_weights(w_in_hbm, w_pc_hbm, w_pm_hbm, w_o_hbm,
                         w_if, w_qk, w_vo, w_xbc, w_gt, w_pc, w_pm, w_o, stage, sem):
    rows = STAGE_ROWS
    n_xbc, n_qk, n_vo, n_gt = 3 * CONV_DIM, 2 * HEADS * DQK, 2 * MLSTM_DIM, 2 * D_MODEL
    o_if = n_xbc + n_qk + n_vo
    groups = [(w_in_hbm, o_if, w_if, 1, True),
              (w_in_hbm, n_xbc, w_qk, n_qk // rows, True),
              (w_in_hbm, n_xbc + n_qk, w_vo, n_vo // rows, True),
              (w_in_hbm, 0, w_xbc, n_xbc // rows, True),
              (w_in_hbm, o_if + 2 * HEADS, w_gt, n_gt // rows, True),
              (w_pc_hbm, 0, w_pc, CONV_DIM // rows, False),
              (w_pm_hbm, 0, w_pm, MLSTM_DIM // rows, False),
              (w_o_hbm, 0, w_o, D_MODEL // rows, False)]
    spans, n_all = [], 0
    for src, row0, dst, count, transposed in groups:
        spans.append((n_all, count, src, row0, dst, transposed))
        n_all += count

    def copy(src, row0, lo, i, slot):
        r = pl.multiple_of(row0 + (i - lo) * rows, SUBLANES)
        return pltpu.make_async_copy(src.at[pl.ds(r, rows), :], stage.at[slot], sem.at[slot])

    def start(i, slot):
        for lo, count, src, row0, _, _ in spans:
            @pl.when(jnp.logical_and(i >= lo, i < lo + count))
            def _():
                copy(src, row0, lo, i, slot).start()

    def prime(i, carry):
        start(i, i)
        return carry

    def finish(i, carry):
        slot = i % STAGE_SLOTS
        for lo, count, src, row0, dst, transposed in spans:
            @pl.when(jnp.logical_and(i >= lo, i < lo + count))
            def _():
                copy(src, row0, lo, i, slot).wait()
                chunk = stage[slot]
                off = pl.multiple_of((i - lo) * rows, rows)
                if transposed:
                    width = min(rows, dst.shape[1])
                    dst[:, pl.ds(off, width)] = chunk.T[:, 0:width].astype(BF16)
                else:
                    dst[pl.ds(off, rows), :] = chunk.astype(BF16)

        @pl.when(i + STAGE_SLOTS < n_all)
        def _():
            start(i + STAGE_SLOTS, slot)

        return carry

    lax.fori_loop(0, STAGE_SLOTS, prime, 0)
    lax.fori_loop(0, n_all, finish, 0)


def _mixer_kernel(x_ref, mod_ref, g_ref, w_in_hbm, b_xbc, b_qk, b_vo, b_gt, b_if,
                  cw_ref, hg_ref, w_pc_hbm, w_pm_hbm, w_o_hbm,
                  o_ref,
                  w_xbc, w_qk, w_vo, w_gt, w_if, w_pc, w_pm, w_o, stage, stage_sem,
                  hb_s, ubuf, y_s, sg_s, qs_s, k_s, qb_s, kb_s, vb_s, og_s, hm_s, c_s, n_s, m_s):
    tm = TM_MIX
    n_chunks = tm // CHUNK
    j = pl.program_id(1)
    n_xbc, n_qk, n_vo, n_gt = 3 * CONV_DIM, 2 * HEADS * DQK, 2 * MLSTM_DIM, 2 * D_MODEL

    @pl.when(jnp.logical_and(pl.program_id(0) == 0, j == 0))
    def _():
        _mixer_stage_weights(w_in_hbm, w_pc_hbm, w_pm_hbm, w_o_hbm,
                             w_if, w_qk, w_vo, w_xbc, w_gt, w_pc, w_pm, w_o, stage, stage_sem)

    @pl.when(j == 0)
    def _():
        ubuf[0:SUBLANES, :] = jnp.zeros((SUBLANES, CONV_DIM), F32)
        c_s[...] = jnp.zeros_like(c_s)
        n_s[...] = jnp.zeros_like(n_s)
        m_s[...] = jnp.zeros_like(m_s)

    x = x_ref[0]
    mod_row = pl.ds(pl.program_id(0), 1)
    sh1 = mod_ref[mod_row, 0:D_MODEL]
    sc1 = mod_ref[mod_row, D_MODEL:2 * D_MODEL]
    gt1 = mod_ref[mod_row, 2 * D_MODEL:3 * D_MODEL]
    h = (_rms_scale(x) * g_ref[...]) * (1.0 + sc1) + sh1
    hb_s[...] = h.astype(BF16)


    gif = _dot(hb_s[...], w_if[...]) + b_if[...]
    lane8 = lax.broadcasted_iota(jnp.int32, (SUBLANES, CHUNK), 1)
    head_rows = lax.broadcasted_iota(jnp.int32, (SUBLANES, CHUNK), 0) < HEADS

    def lane_scan(v, op, fill):
        d = 1
        while d < CHUNK:
            v = op(v, jnp.where(lane8 >= d, pltpu.roll(v, d, axis=1), fill))
            d *= 2
        return v

    m_run = m_s[...]
    gate_rows = []
    for ci in range(n_chunks):
        gi = gif[ci * CHUNK:(ci + 1) * CHUNK, :].T[0:SUBLANES, :]
        li = jnp.where(head_rows, gi, 0.0)
        lf = jnp.where(head_rows, _log_sigmoid(pltpu.roll(gi, HEADS, axis=0)), 0.0)
        b = lane_scan(lf, jnp.add, 0.0)
        g = b[:, CHUNK - 1:CHUNK]
        r = li - b
        cm = lane_scan(r, jnp.maximum, -jnp.inf)
        r_max = cm[:, CHUNK - 1:CHUNK]
        m_prev = m_run[:, 0:1]
        big_m = jnp.maximum(m_prev, cm)
        m_loc = g + r_max
        m_new = jnp.maximum(g + m_prev, m_loc)
        gate_rows.append(dict(r=r, big_m=big_m, w=jnp.exp(r - r_max), em=jnp.exp(-(b + big_m)), m_prev=m_prev,
                              s_old=jnp.exp(g + m_prev - m_new), s_new=jnp.exp(m_loc - m_new)))
        m_run = jnp.broadcast_to(m_new, (SUBLANES, CHUNK))
    m_s[...] = m_run

    nq = HEADS * DQK
    qs = (_dot(hb_s[...], w_qk[:, 0:nq]) + b_qk[:, 0:nq]) * (DQK ** -0.5)
    qs_s[...] = qs
    qb_s[...] = qs.astype(BF16)
    k = _dot(hb_s[...], w_qk[:, nq:2 * nq]) + b_qk[:, nq:2 * nq]
    k_s[...] = k
    kb_s[...] = k.astype(BF16)
    vb_s[...] = (_dot(hb_s[...], w_vo[:, 0:MLSTM_DIM]) + b_vo[:, 0:MLSTM_DIM]).astype(BF16)
    og_s[...] = jax.nn.sigmoid(_dot(hb_s[...], w_vo[:, MLSTM_DIM:2 * MLSTM_DIM])
                               + b_vo[:, MLSTM_DIM:2 * MLSTM_DIM])

    row = lax.broadcasted_iota(jnp.int32, (CHUNK, CHUNK), 0)
    col = lax.broadcasted_iota(jnp.int32, (CHUNK, CHUNK), 1)
    tri = col <= row

    def token_major(row_vec):
        return jnp.broadcast_to(row_vec, (CHUNK, CHUNK)).T

    def pair_weights(ci, hd):
        rs = slice(ci * CHUNK, (ci + 1) * CHUNK)
        hq = slice(hd * DQK, (hd + 1) * DQK)
        gr = gate_rows[ci]
        m_bc = token_major(gr["big_m"][hd:hd + 1, :])
        w_bc = token_major(gr["w"][hd:hd + 1, :])
        r_row = gr["r"][hd:hd + 1, :]
        wts = jnp.where(tri, jnp.exp(r_row - m_bc), 0.0) * _dot_nt(qb_s[rs, hq], kb_s[rs, hq])
        kw = w_bc * k_s[rs, hq]
        return wts, kw.T.astype(BF16), jnp.sum(kw, axis=0, keepdims=True), m_bc

    def mlstm_pair(ci, hd, wts, kw_t, n_loc, m_bc):
        rs = slice(ci * CHUNK, (ci + 1) * CHUNK)
        hq = slice(hd * DQK, (hd + 1) * DQK)
        hv = slice(hd * DV, (hd + 1) * DV)
        gr = gate_rows[ci]
        em_bc = token_major(gr["em"][hd:hd + 1, :])
        si_bc = jnp.exp(gr["m_prev"][hd:hd + 1, :] - m_bc)
        s_old = gr["s_old"][hd:hd + 1, :]
        s_new = gr["s_new"][hd:hd + 1, :]
        n_prev = n_s[hd:hd + 1, :]
        c_prev = c_s[hd]
        qb = qb_s[rs, hq]
        vb = vb_s[rs, hv]

        num = (_dot(wts.astype(BF16), vb)
               + jnp.concatenate([si_bc, si_bc], axis=1) * _dot(qb, c_prev.astype(BF16)))
        den = (jnp.sum(wts, axis=1, keepdims=True)
               + si_bc * jnp.sum(qs_s[rs, hq] * n_prev, axis=1, keepdims=True))
        inv = 1.0 / jnp.maximum(jnp.abs(den), em_bc)
        hm_s[rs, hv] = num * jnp.concatenate([inv, inv], axis=1)

        c_loc = _dot(kw_t, vb)
        c_s[hd] = s_old * c_prev + s_new * c_loc
        n_s[hd:hd + 1, :] = s_old * n_prev + s_new * n_loc

    def conv_block(jb):
        lo = jb * MIX_BLOCK
        cs = slice(lo, lo + MIX_BLOCK)

        def proj(off):
            return (_dot(hb_s[...], w_xbc[:, off + lo:off + lo + MIX_BLOCK])
                    + b_xbc[:, off + lo:off + lo + MIX_BLOCK])

        xin = proj(0)
        bg = proj(CONV_DIM)
        cg = proj(2 * CONV_DIM)
        ubuf[SUBLANES:SUBLANES + tm, cs] = cg * xin
        u0 = ubuf[SUBLANES:SUBLANES + tm, cs]
        u1 = ubuf[SUBLANES - 1:SUBLANES - 1 + tm, cs]
        u2 = ubuf[SUBLANES - 2:SUBLANES - 2 + tm, cs]
        y = bg * (u2 * cw_ref[0, :, cs] + u1 * cw_ref[1, :, cs] + u0 * cw_ref[2, :, cs])
        y_s[:, cs] = y.astype(BF16)
        ubuf[0:SUBLANES, cs] = ubuf[tm:tm + SUBLANES, cs]

    def gate_block(jb):
        cs = slice(jb * MIX_BLOCK, (jb + 1) * MIX_BLOCK)
        sg_s[:, cs] = jax.nn.sigmoid(_dot(hb_s[...], w_gt[:, cs]) + b_gt[:, cs])

    def pconv_block(jb):
        cs = slice(jb * MIX_BLOCK, (jb + 1) * MIX_BLOCK)
        sg_s[:, cs] = sg_s[:, cs] * _dot(y_s[...], w_pc[:, cs])

    nb = CONV_DIM // MIX_BLOCK
    mxu_items = ([functools.partial(conv_block, jb) for jb in range(nb)]
                 + [functools.partial(gate_block, jb) for jb in range(2 * nb)]
                 + [functools.partial(pconv_block, jb) for jb in range(nb)])

    pairs = [(ci, hd) for ci in range(n_chunks) for hd in range(HEADS)]
    ahead = pair_weights(*pairs[0])
    for idx, (ci, hd) in enumerate(pairs):
        ready = ahead
        if idx + 1 < len(pairs):
            ahead = pair_weights(*pairs[idx + 1])
        mlstm_pair(ci, hd, *ready)
        lo = idx * len(mxu_items) // len(pairs)
        hi = (idx + 1) * len(mxu_items) // len(pairs)
        for item in mxu_items[lo:hi]:
            item()

    parts = []
    for hd in range(HEADS):
        hm_h = hm_s[:, hd * DV:(hd + 1) * DV]
        parts.append(_rms_scale(hm_h) * hg_ref[hd:hd + 1, :])
    hm = og_s[...] * jnp.concatenate(parts, axis=1)
    p_m = _dot(hm.astype(BF16), w_pm[...])
    z = sg_s[:, 0:D_MODEL] + sg_s[:, D_MODEL:2 * D_MODEL] * p_m
    o_ref[0] = x + gt1 * _dot(z.astype(BF16), w_o[...])


def _const_spec(shape):
    nd = len(shape)
    return pl.BlockSpec(shape, lambda b, j: (0,) * nd, pipeline_mode=pl.Buffered(1))


def _lane_window_spec(width, block):
    return pl.BlockSpec((1, width), lambda b, j: (0, block), pipeline_mode=pl.Buffered(1))


def _mixer_call(x, mod, g_norm, w_t, b_in, b_gt, conv_w, head_g, w_pc, w_pm, w_o):
    bsz, seq, d = x.shape
    tm = TM_MIX
    n_xbc, n_qk, n_vo, n_gt = 3 * CONV_DIM, 2 * HEADS * DQK, 2 * MLSTM_DIM, 2 * D_MODEL
    bias_windows = [_lane_window_spec(n_xbc, 0), _lane_window_spec(n_qk, n_xbc // n_qk),
                    _lane_window_spec(n_vo, (n_xbc + n_qk) // n_vo),
                    _const_spec(b_gt.shape),
                    _lane_window_spec(CHUNK, (n_xbc + n_qk + n_vo) // CHUNK)]
    hbm = pl.BlockSpec(memory_space=pl.ANY)
    return pl.pallas_call(
        _mixer_kernel,
        grid=(bsz, seq // tm),
        in_specs=[
            pl.BlockSpec((1, tm, d), lambda b, j: (b, j, 0)),
            _const_spec(mod.shape), _const_spec(g_norm.shape), hbm,
        ] + bias_windows + [_const_spec(conv_w.shape), _const_spec(head_g.shape), hbm, hbm, hbm],
        out_specs=pl.BlockSpec((1, tm, d), lambda b, j: (b, j, 0)),
        out_shape=jax.ShapeDtypeStruct(x.shape, F32),
        scratch_shapes=[
            pltpu.VMEM((d, n_xbc), BF16),
            pltpu.VMEM((d, n_qk), BF16),
            pltpu.VMEM((d, n_vo), BF16),
            pltpu.VMEM((d, n_gt), BF16),
            pltpu.VMEM((d, CHUNK), BF16),
            pltpu.VMEM((CONV_DIM, d), BF16),
            pltpu.VMEM((MLSTM_DIM, d), BF16),
            pltpu.VMEM((d, d), BF16),
            pltpu.VMEM((STAGE_SLOTS, STAGE_ROWS, d), F32),
            pltpu.SemaphoreType.DMA((STAGE_SLOTS,)),
            pltpu.VMEM((tm, D_MODEL), BF16),
            pltpu.VMEM((tm + SUBLANES, CONV_DIM), F32),
            pltpu.VMEM((tm, CONV_DIM), BF16),
            pltpu.VMEM((tm, 2 * D_MODEL), F32),
            pltpu.VMEM((tm, HEADS * DQK), F32),
            pltpu.VMEM((tm, HEADS * DQK), F32),
            pltpu.VMEM((tm, HEADS * DQK), BF16),
            pltpu.VMEM((tm, HEADS * DQK), BF16),
            pltpu.VMEM((tm, MLSTM_DIM), BF16),
            pltpu.VMEM((tm, MLSTM_DIM), F32),
            pltpu.VMEM((tm, MLSTM_DIM), F32),
            pltpu.VMEM((HEADS, DQK, DV), F32),
            pltpu.VMEM((SUBLANES, DQK), F32),
            pltpu.VMEM((SUBLANES, CHUNK), F32),
        ],
        compiler_params=pltpu.CompilerParams(
            dimension_semantics=("arbitrary", "arbitrary"),
            vmem_limit_bytes=VMEM_LIMIT_BYTES),
        name="mixer_sublayer",
    )(x, mod, g_norm, w_t, b_in, b_in, b_in, b_gt, b_in, conv_w, head_g, w_pc, w_pm, w_o)


def _ffn_stage_weights(w_up_hbm, w_d_hbm, w_up, w_d, stage, sem):
    n_rb = D_MODEL // STAGE_ROWS
    n_up = n_rb * (2 * D_FF // FFN_STAGE_COLS)
    n_all = n_up + D_FF // STAGE_ROWS

    def up_window(i):
        r = pl.multiple_of((i % n_rb) * STAGE_ROWS, STAGE_ROWS)
        c = pl.multiple_of((i // n_rb) * FFN_STAGE_COLS, 128)
        return pl.ds(r, STAGE_ROWS), pl.ds(c, FFN_STAGE_COLS)

    def down_rows(i):
        return pl.ds(pl.multiple_of((i - n_up) * STAGE_ROWS, STAGE_ROWS), STAGE_ROWS)

    def up_copy(i, slot):
        rows, cols = up_window(i)
        return pltpu.make_async_copy(w_up_hbm.at[rows, cols], stage.at[slot], sem.at[slot])

    def down_copy(i, slot):
        return pltpu.make_async_copy(w_d_hbm.at[down_rows(i), :],
                                     stage.at[slot, :, pl.ds(0, D_MODEL)], sem.at[slot])

    def start(i, slot):
        @pl.when(i < n_up)
        def _():
            up_copy(i, slot).start()

        @pl.when(i >= n_up)
        def _():
            down_copy(i, slot).start()

    def prime(i, carry):
        start(i, i)
        return carry

    def finish(i, carry):
        slot = i % STAGE_SLOTS

        @pl.when(i < n_up)
        def _():
            up_copy(i, slot).wait()
            rows, cols = up_window(i)
            w_up[rows, cols] = stage[slot].astype(BF16)

        @pl.when(i >= n_up)
        def _():
            down_copy(i, slot).wait()
            w_d[down_rows(i), :] = stage[slot, :, 0:D_MODEL].astype(BF16)

        @pl.when(i + STAGE_SLOTS < n_all)
        def _():
            start(i + STAGE_SLOTS, slot)

        return carry

    lax.fori_loop(0, STAGE_SLOTS, prime, 0)
    lax.fori_loop(0, n_all, finish, 0)


def _ffn_kernel(x_ref, mod_ref, g_ref, w_up_hbm, cw_ref, w_d_hbm, gf_ref, o_ref,
                w_up, w_d, stage, stage_sem, abuf, *, final_norm):
    tm = TM_FFN
    j = pl.program_id(1)

    @pl.when(jnp.logical_and(pl.program_id(0) == 0, j == 0))
    def _():
        _ffn_stage_weights(w_up_hbm, w_d_hbm, w_up, w_d, stage, stage_sem)

    @pl.when(j == 0)
    def _():
        abuf[0:SUBLANES, :] = jnp.zeros((SUBLANES, D_FF), F32)

    x = x_ref[0]
    mod_row = pl.ds(pl.program_id(0), 1)
    sh2 = mod_ref[mod_row, 3 * D_MODEL:4 * D_MODEL]
    sc2 = mod_ref[mod_row, 4 * D_MODEL:5 * D_MODEL]
    gt2 = mod_ref[mod_row, 5 * D_MODEL:6 * D_MODEL]
    h = (_rms_scale(x) * g_ref[...]) * (1.0 + sc2) + sh2
    hb = h.astype(BF16)
    abuf[SUBLANES:SUBLANES + tm, :] = _dot(hb, w_up[:, 0:D_FF])
    gate = _dot(hb, w_up[:, D_FF:2 * D_FF])
    a0 = abuf[SUBLANES:SUBLANES + tm, :]
    a1 = abuf[SUBLANES - 1:SUBLANES - 1 + tm, :]
    a2 = abuf[SUBLANES - 2:SUBLANES - 2 + tm, :]
    ac = a2 * cw_ref[0] + a1 * cw_ref[1] + a0 * cw_ref[2]
    abuf[0:SUBLANES, :] = abuf[tm:tm + SUBLANES, :]
    act = (ac * jax.nn.sigmoid(ac)) * gate
    x2 = x + gt2 * _dot(act.astype(BF16), w_d[...])
    o_ref[0] = _rms_scale(x2) * gf_ref[...] if final_norm else x2


def _ffn_call(x, mod, g_norm, w_up, conv_w, w_d, g_final, final_norm):
    bsz, seq, d = x.shape
    tm = TM_FFN
    hbm = pl.BlockSpec(memory_space=pl.ANY)
    return pl.pallas_call(
        functools.partial(_ffn_kernel, final_norm=final_norm),
        grid=(bsz, seq // tm),
        in_specs=[
            pl.BlockSpec((1, tm, d), lambda b, j: (b, j, 0)),
            _const_spec(mod.shape), _const_spec(g_norm.shape), hbm, _const_spec(conv_w.shape), hbm,
            _const_spec(g_final.shape),
        ],
        out_specs=pl.BlockSpec((1, tm, d), lambda b, j: (b, j, 0)),
        out_shape=jax.ShapeDtypeStruct(x.shape, F32),
        scratch_shapes=[
            pltpu.VMEM((d, 2 * D_FF), BF16),
            pltpu.VMEM((D_FF, d), BF16),
            pltpu.VMEM((STAGE_SLOTS, STAGE_ROWS, FFN_STAGE_COLS), F32),
            pltpu.SemaphoreType.DMA((STAGE_SLOTS,)),
            pltpu.VMEM((tm + SUBLANES, D_FF), F32),
        ],
        compiler_params=pltpu.CompilerParams(
            dimension_semantics=("arbitrary", "arbitrary"),
            vmem_limit_bytes=VMEM_LIMIT_BYTES),
        name="ffn_sublayer",
    )(x, mod, g_norm, w_up, conv_w, w_d, g_final)


def kernel(x, c, w_ada, b_ada, g_norm_mix, w_in, b_in, conv_mix_w, mlstm_head_g,
           w_proj_conv, w_proj_mlstm, w_out, g_norm_ffn, w_up, conv_ffn_w, w_down, g_final):
    depth = w_ada.shape[0]
    bsz = x.shape[0]
    assert bsz <= SUBLANES
    o_gt = 3 * CONV_DIM + 2 * HEADS * DQK + 2 * MLSTM_DIM + 2 * HEADS

    for l in range(depth):
        mod = _ada_call(c, w_ada[l], b_ada[l][None, :])

        bi = b_in[l]
        x = _mixer_call(
            x, mod, g_norm_mix[l][None, :],
            w_in[l].T, bi[None, :], bi[None, o_gt:],
            conv_mix_w[l][:, None, :], mlstm_head_g[l],
            w_proj_conv[l], w_proj_mlstm[l], w_out[l])

        x = _ffn_call(
            x, mod, g_norm_ffn[l][None, :], w_up[l],
            conv_ffn_w[l][:, None, :], w_down[l], g_final[None, :],
            final_norm=(l == depth - 1))
    return x
```

```python
import functools

import jax
import jax.numpy as jnp
from jax import lax
from jax.experimental import pallas as pl
from jax.experimental.pallas import tpu as pltpu

F32 = jnp.float32
BF16 = jnp.bfloat16

D_MODEL = 1024
CONV_DIM = 1024
HEADS = 4
DQK = 128
DV = 256
MLSTM_DIM = HEADS * DV
CHUNK = 128
D_FF = 2816
N_MOD = 6
EPS = 1e-6

SUBLANES = 8
TM_MIX = 512
TM_FFN = 512
MIX_BLOCK = 256
STAGE_SLOTS = 8
STAGE_ROWS = 256
FFN_STAGE_COLS = D_FF // 2
ADA_BLOCK = 3072
V7X_VMEM_BYTES = 64 * 1024 * 1024
VMEM_LIMIT_BYTES = V7X_VMEM_BYTES * 7 // 8


def _dot(a, b):
    return jnp.dot(a, b, preferred_element_type=F32)


def _dot_nt(a, b):
    return lax.dot_general(a, b, (((1,), (1,)), ((), ())), preferred_element_type=F32)


def _rms_scale(x):
    return x * lax.rsqrt(jnp.mean(x * x, axis=-1, keepdims=True) + EPS)


def _log_sigmoid(x):
    return -(jnp.maximum(-x, 0.0) + jnp.log1p(jnp.exp(-jnp.abs(x))))


def _stage_to_bf16(jobs, stage, sem):
    def copy(i):
        src = jobs[i][0]
        rows, cols = src.shape
        slot = i % STAGE_SLOTS
        return pltpu.make_async_copy(src, stage.at[slot, pl.ds(0, rows), pl.ds(0, cols)], sem.at[slot])

    for i in range(min(STAGE_SLOTS, len(jobs))):
        copy(i).start()
    for i, (src, dst) in enumerate(jobs):
        rows, cols = src.shape
        copy(i).wait()
        chunk = stage[i % STAGE_SLOTS, 0:rows, 0:cols]
        if dst.shape != src.shape:
            assert dst.shape == (cols, rows)
            chunk = chunk.T
        dst[...] = chunk.astype(BF16)
        if i + STAGE_SLOTS < len(jobs):
            copy(i + STAGE_SLOTS).start()


def _row_jobs(src, src_row0, dst, n_rows):
    step = min(STAGE_ROWS, n_rows)
    return [(src.at[pl.ds(src_row0 + r, step), :], dst.at[pl.ds(r, step), :])
            for r in range(0, n_rows, step)]


def _transposed_jobs(src, src_row0, dst, n_rows):
    step = min(STAGE_ROWS, n_rows)
    return [(src.at[pl.ds(src_row0 + r, step), :], dst.at[:, pl.ds(r, step)])
            for r in range(0, n_rows, step)]


def _ada_kernel(c_ref, w_ref, b_ref, o_ref):
    c = c_ref[...]
    act = c * jax.nn.sigmoid(c)
    pad = jnp.zeros((SUBLANES - act.shape[0], act.shape[1]), F32)
    act = jnp.concatenate([act, pad], axis=0)
    o_ref[...] = _dot(act.astype(BF16), w_ref[...].astype(BF16)) + b_ref[...]


def _ada_call(c, w_ada, b_ada):
    n = w_ada.shape[1]
    tn = ADA_BLOCK
    return pl.pallas_call(
        _ada_kernel,
        grid=(n // tn,),
        in_specs=[
            pl.BlockSpec(c.shape, lambda i: (0, 0)),
            pl.BlockSpec((D_MODEL, tn), lambda i: (0, i)),
            pl.BlockSpec((1, tn), lambda i: (0, i)),
        ],
        out_specs=pl.BlockSpec((SUBLANES, tn), lambda i: (0, i)),
        out_shape=jax.ShapeDtypeStruct((SUBLANES, n), F32),
        compiler_params=pltpu.CompilerParams(dimension_semantics=("arbitrary",)),
        name="adaln_mod",
    )(c, w_ada, b_ada)


def _mixer_kernel(x_ref, mod_ref, g_ref, w_in_hbm, b_xbc, b_qk, b_vo, b_gt, b_if,
                  cw_ref, hg_ref, w_pc_hbm, w_pm_hbm, w_o_hbm,
                  o_ref,
                  w_xbc, w_qk, w_vo, w_gt, w_if, w_pc, w_pm, w_o, stage, stage_sem,
                  hb_s, ubuf, y_s, sg_s, qs_s, k_s, qb_s, kb_s, vb_s, og_s, hm_s, pm_s, c_s, n_s, m_s):
    tm = TM_MIX
    n_chunks = tm // CHUNK
    j = pl.program_id(1)
    n_xbc, n_qk, n_vo, n_gt = 3 * CONV_DIM, 2 * HEADS * DQK, 2 * MLSTM_DIM, 2 * D_MODEL

    @pl.when(jnp.logical_and(pl.program_id(0) == 0, j == 0))
    def _():
        o_if = n_xbc + n_qk + n_vo
        jobs = (_transposed_jobs(w_in_hbm, o_if, w_if, CHUNK)
                + _transposed_jobs(w_in_hbm, n_xbc, w_qk, n_qk)
                + _transposed_jobs(w_in_hbm, n_xbc + n_qk, w_vo, n_vo)
                + _transposed_jobs(w_in_hbm, 0, w_xbc, n_xbc)
                + _transposed_jobs(w_in_hbm, o_if + 2 * HEADS, w_gt, n_gt)
                + _row_jobs(w_pc_hbm, 0, w_pc, CONV_DIM)
                + _row_jobs(w_pm_hbm, 0, w_pm, MLSTM_DIM)
                + _row_jobs(w_o_hbm, 0, w_o, D_MODEL))
        _stage_to_bf16(jobs, stage, stage_sem)

    @pl.when(j == 0)
    def _():
        ubuf[0:SUBLANES, :] = jnp.zeros((SUBLANES, CONV_DIM), F32)
        c_s[...] = jnp.zeros_like(c_s)
        n_s[...] = jnp.zeros_like(n_s)
        m_s[...] = jnp.zeros_like(m_s)

    x = x_ref[0]
    mod_row = pl.ds(pl.program_id(0), 1)
    sh1 = mod_ref[mod_row, 0:D_MODEL]
    sc1 = mod_ref[mod_row, D_MODEL:2 * D_MODEL]
    gt1 = mod_ref[mod_row, 2 * D_MODEL:3 * D_MODEL]
    h = (_rms_scale(x) * g_ref[...]) * (1.0 + sc1) + sh1
    hb_s[...] = h.astype(BF16)


    gif = _dot(hb_s[...], w_if[...]) + b_if[...]
    lane8 = lax.broadcasted_iota(jnp.int32, (SUBLANES, CHUNK), 1)
    head_rows = lax.broadcasted_iota(jnp.int32, (SUBLANES, CHUNK), 0) < HEADS

    def lane_scan(v, op, fill):
        d = 1
        while d < CHUNK:
            v = op(v, jnp.where(lane8 >= d, pltpu.roll(v, d, axis=1), fill))
            d *= 2
        return v

    m_run = m_s[...]
    gate_rows = []
    for ci in range(n_chunks):
        gi = gif[ci * CHUNK:(ci + 1) * CHUNK, :].T[0:SUBLANES, :]
        li = jnp.where(head_rows, gi, 0.0)
        lf = jnp.where(head_rows, _log_sigmoid(pltpu.roll(gi, HEADS, axis=0)), 0.0)
        b = lane_scan(lf, jnp.add, 0.0)
        g = b[:, CHUNK - 1:CHUNK]
        r = li - b
        cm = lane_scan(r, jnp.maximum, -jnp.inf)
        r_max = cm[:, CHUNK - 1:CHUNK]
        m_prev = m_run[:, 0:1]
        big_m = jnp.maximum(m_prev, cm)
        m_loc = g + r_max
        m_new = jnp.maximum(g + m_prev, m_loc)
        gate_rows.append(dict(r=r, big_m=big_m, w=jnp.exp(r - r_max), em=jnp.exp(-(b + big_m)), m_prev=m_prev,
                              s_old=jnp.exp(g + m_prev - m_new), s_new=jnp.exp(m_loc - m_new)))
        m_run = jnp.broadcast_to(m_new, (SUBLANES, CHUNK))
    m_s[...] = m_run

    nq = HEADS * DQK
    qs = (_dot(hb_s[...], w_qk[:, 0:nq]) + b_qk[:, 0:nq]) * (DQK ** -0.5)
    qs_s[...] = qs
    qb_s[...] = qs.astype(BF16)
    k = _dot(hb_s[...], w_qk[:, nq:2 * nq]) + b_qk[:, nq:2 * nq]
    k_s[...] = k
    kb_s[...] = k.astype(BF16)
    vb_s[...] = (_dot(hb_s[...], w_vo[:, 0:MLSTM_DIM]) + b_vo[:, 0:MLSTM_DIM]).astype(BF16)
    og_s[...] = jax.nn.sigmoid(_dot(hb_s[...], w_vo[:, MLSTM_DIM:2 * MLSTM_DIM])
                               + b_vo[:, MLSTM_DIM:2 * MLSTM_DIM])

    row = lax.broadcasted_iota(jnp.int32, (CHUNK, CHUNK), 0)
    col = lax.broadcasted_iota(jnp.int32, (CHUNK, CHUNK), 1)
    tri = col <= row

    def token_major(row_vec):
        return jnp.broadcast_to(row_vec, (CHUNK, CHUNK)).T

    def pair_weights(ci, hd):
        rs = slice(ci * CHUNK, (ci + 1) * CHUNK)
        hq = slice(hd * DQK, (hd + 1) * DQK)
        gr = gate_rows[ci]
        m_bc = token_major(gr["big_m"][hd:hd + 1, :])
        w_bc = token_major(gr["w"][hd:hd + 1, :])
        r_row = gr["r"][hd:hd + 1, :]
        wts = jnp.where(tri, jnp.exp(r_row - m_bc), 0.0) * _dot_nt(qb_s[rs, hq], kb_s[rs, hq])
        kw = w_bc * k_s[rs, hq]
        return wts, kw.T.astype(BF16), jnp.sum(kw, axis=0, keepdims=True), m_bc

    def mlstm_pair(ci, hd, wts, kw_t, n_loc, m_bc):
        rs = slice(ci * CHUNK, (ci + 1) * CHUNK)
        hq = slice(hd * DQK, (hd + 1) * DQK)
        hv = slice(hd * DV, (hd + 1) * DV)
        gr = gate_rows[ci]
        em_bc = token_major(gr["em"][hd:hd + 1, :])
        si_bc = jnp.exp(gr["m_prev"][hd:hd + 1, :] - m_bc)
        s_old = gr["s_old"][hd:hd + 1, :]
        s_new = gr["s_new"][hd:hd + 1, :]
        n_prev = n_s[hd:hd + 1, :]
        c_prev = c_s[hd]
        qb = qb_s[rs, hq]
        vb = vb_s[rs, hv]

        num = (_dot(wts.astype(BF16), vb)
               + jnp.concatenate([si_bc, si_bc], axis=1) * _dot(qb, c_prev.astype(BF16)))
        den = (jnp.sum(wts, axis=1, keepdims=True)
               + si_bc * jnp.sum(qs_s[rs, hq] * n_prev, axis=1, keepdims=True))
        inv = 1.0 / jnp.maximum(jnp.abs(den), em_bc)
        hm_s[rs, hv] = num * jnp.concatenate([inv, inv], axis=1)

        c_loc = _dot(kw_t, vb)
        c_s[hd] = s_old * c_prev + s_new * c_loc
        n_s[hd:hd + 1, :] = s_old * n_prev + s_new * n_loc

    def conv_block(jb):
        lo = jb * MIX_BLOCK
        cs = slice(lo, lo + MIX_BLOCK)

        def proj(off):
            return (_dot(hb_s[...], w_xbc[:, off + lo:off + lo + MIX_BLOCK])
                    + b_xbc[:, off + lo:off + lo + MIX_BLOCK])

        xin = proj(0)
        bg = proj(CONV_DIM)
        cg = proj(2 * CONV_DIM)
        ubuf[SUBLANES:SUBLANES + tm, cs] = cg * xin
        u0 = ubuf[SUBLANES:SUBLANES + tm, cs]
        u1 = ubuf[SUBLANES - 1:SUBLANES - 1 + tm, cs]
        u2 = ubuf[SUBLANES - 2:SUBLANES - 2 + tm, cs]
        y = bg * (u2 * cw_ref[0, :, cs] + u1 * cw_ref[1, :, cs] + u0 * cw_ref[2, :, cs])
        y_s[:, cs] = y.astype(BF16)
        ubuf[0:SUBLANES, cs] = ubuf[tm:tm + SUBLANES, cs]

    def gate_block(jb):
        cs = slice(jb * MIX_BLOCK, (jb + 1) * MIX_BLOCK)
        sg_s[:, cs] = jax.nn.sigmoid(_dot(hb_s[...], w_gt[:, cs]) + b_gt[:, cs])

    def pconv_block(jb):
        cs = slice(jb * MIX_BLOCK, (jb + 1) * MIX_BLOCK)
        sg_s[:, cs] = sg_s[:, cs] * _dot(y_s[...], w_pc[:, cs])

    def pm_block(rb):
        rs = slice(rb * (tm // 2), (rb + 1) * (tm // 2))
        parts = []
        for hd in range(HEADS):
            hm_h = hm_s[rs, hd * DV:(hd + 1) * DV]
            parts.append(_rms_scale(hm_h) * hg_ref[hd:hd + 1, :])
        hm = og_s[rs, :] * jnp.concatenate(parts, axis=1)
        pm_s[rs, :] = _dot(hm.astype(BF16), w_pm[...])

    nb = CONV_DIM // MIX_BLOCK
    mxu_items = ([functools.partial(conv_block, jb) for jb in range(nb)]
                 + [functools.partial(gate_block, jb) for jb in range(2 * nb)]
                 + [functools.partial(pconv_block, jb) for jb in range(nb)])

    pairs = [(ci, hd) for ci in range(n_chunks) for hd in range(HEADS)]
    ahead = pair_weights(*pairs[0])
    for idx, (ci, hd) in enumerate(pairs):
        ready = ahead
        if idx + 1 < len(pairs):
            ahead = pair_weights(*pairs[idx + 1])
        mlstm_pair(ci, hd, *ready)
        lo = idx * len(mxu_items) // len(pairs)
        hi = (idx + 1) * len(mxu_items) // len(pairs)
        for item in mxu_items[lo:hi]:
            item()

    pm_block(0)
    pm_block(1)
    z = sg_s[:, 0:D_MODEL] + sg_s[:, D_MODEL:2 * D_MODEL] * pm_s[...]
    o_ref[0] = x + gt1 * _dot(z.astype(BF16), w_o[...])


def _const_spec(shape):
    nd = len(shape)
    return pl.BlockSpec(shape, lambda b, j: (0,) * nd, pipeline_mode=pl.Buffered(1))


def _lane_window_spec(width, block):
    return pl.BlockSpec((1, width), lambda b, j: (0, block), pipeline_mode=pl.Buffered(1))


def _mixer_call(x, mod, g_norm, w_t, b_in, b_gt, conv_w, head_g, w_pc, w_pm, w_o):
    bsz, seq, d = x.shape
    tm = TM_MIX
    n_xbc, n_qk, n_vo, n_gt = 3 * CONV_DIM, 2 * HEADS * DQK, 2 * MLSTM_DIM, 2 * D_MODEL
    bias_windows = [_lane_window_spec(n_xbc, 0), _lane_window_spec(n_qk, n_xbc // n_qk),
                    _lane_window_spec(n_vo, (n_xbc + n_qk) // n_vo),
                    _const_spec(b_gt.shape),
                    _lane_window_spec(CHUNK, (n_xbc + n_qk + n_vo) // CHUNK)]
    hbm = pl.BlockSpec(memory_space=pl.ANY)
    return pl.pallas_call(
        _mixer_kernel,
        grid=(bsz, seq // tm),
        in_specs=[
            pl.BlockSpec((1, tm, d), lambda b, j: (b, j, 0)),
            _const_spec(mod.shape), _const_spec(g_norm.shape), hbm,
        ] + bias_windows + [_const_spec(conv_w.shape), _const_spec(head_g.shape), hbm, hbm, hbm],
        out_specs=pl.BlockSpec((1, tm, d), lambda b, j: (b, j, 0)),
        out_shape=jax.ShapeDtypeStruct(x.shape, F32),
        scratch_shapes=[
            pltpu.VMEM((d, n_xbc), BF16),
            pltpu.VMEM((d, n_qk), BF16),
            pltpu.VMEM((d, n_vo), BF16),
            pltpu.VMEM((d, n_gt), BF16),
            pltpu.VMEM((d, CHUNK), BF16),
            pltpu.VMEM((CONV_DIM, d), BF16),
            pltpu.VMEM((MLSTM_DIM, d), BF16),
            pltpu.VMEM((d, d), BF16),
            pltpu.VMEM((STAGE_SLOTS, STAGE_ROWS, d), F32),
            pltpu.SemaphoreType.DMA((STAGE_SLOTS,)),
            pltpu.VMEM((tm, D_MODEL), BF16),
            pltpu.VMEM((tm + SUBLANES, CONV_DIM), F32),
            pltpu.VMEM((tm, CONV_DIM), BF16),
            pltpu.VMEM((tm, 2 * D_MODEL), F32),
            pltpu.VMEM((tm, HEADS * DQK), F32),
            pltpu.VMEM((tm, HEADS * DQK), F32),
            pltpu.VMEM((tm, HEADS * DQK), BF16),
            pltpu.VMEM((tm, HEADS * DQK), BF16),
            pltpu.VMEM((tm, MLSTM_DIM), BF16),
            pltpu.VMEM((tm, MLSTM_DIM), F32),
            pltpu.VMEM((tm, MLSTM_DIM), F32),
            pltpu.VMEM((tm, D_MODEL), F32),
            pltpu.VMEM((HEADS, DQK, DV), F32),
            pltpu.VMEM((SUBLANES, DQK), F32),
            pltpu.VMEM((SUBLANES, CHUNK), F32),
        ],
        compiler_params=pltpu.CompilerParams(
            dimension_semantics=("arbitrary", "arbitrary"),
            vmem_limit_bytes=VMEM_LIMIT_BYTES),
        name="mixer_sublayer",
    )(x, mod, g_norm, w_t, b_in, b_in, b_in, b_gt, b_in, conv_w, head_g, w_pc, w_pm, w_o)


def _ffn_stage_weights(w_up_hbm, w_d_hbm, w_up, w_d, stage, sem):
    n_rb = D_MODEL // STAGE_ROWS
    n_up = n_rb * (2 * D_FF // FFN_STAGE_COLS)
    n_all = n_up + D_FF // STAGE_ROWS

    def up_window(i):
        r = pl.multiple_of((i % n_rb) * STAGE_ROWS, STAGE_ROWS)
        c = pl.multiple_of((i // n_rb) * FFN_STAGE_COLS, 128)
        return pl.ds(r, STAGE_ROWS), pl.ds(c, FFN_STAGE_COLS)

    def down_rows(i):
        return pl.ds(pl.multiple_of((i - n_up) * STAGE_ROWS, STAGE_ROWS), STAGE_ROWS)

    def up_copy(i, slot):
        rows, cols = up_window(i)
        return pltpu.make_async_copy(w_up_hbm.at[rows, cols], stage.at[slot], sem.at[slot])

    def down_copy(i, slot):
        return pltpu.make_async_copy(w_d_hbm.at[down_rows(i), :],
                                     stage.at[slot, :, pl.ds(0, D_MODEL)], sem.at[slot])

    def start(i, slot):
        @pl.when(i < n_up)
        def _():
            up_copy(i, slot).start()

        @pl.when(i >= n_up)
        def _():
            down_copy(i, slot).start()

    def prime(i, carry):
        start(i, i)
        return carry

    def finish(i, carry):
        slot = i % STAGE_SLOTS

        @pl.when(i < n_up)
        def _():
            up_copy(i, slot).wait()
            rows, cols = up_window(i)
            w_up[rows, cols] = stage[slot].astype(BF16)

        @pl.when(i >= n_up)
        def _():
            down_copy(i, slot).wait()
            w_d[down_rows(i), :] = stage[slot, :, 0:D_MODEL].astype(BF16)

        @pl.when(i + STAGE_SLOTS < n_all)
        def _():
            start(i + STAGE_SLOTS, slot)

        return carry

    lax.fori_loop(0, STAGE_SLOTS, prime, 0)
    lax.fori_loop(0, n_all, finish, 0)


def _ffn_kernel(x_ref, mod_ref, g_ref, w_up_hbm, cw_ref, w_d_hbm, gf_ref, o_ref,
                w_up, w_d, stage, stage_sem, abuf, *, final_norm):
    tm = TM_FFN
    j = pl.program_id(1)

    @pl.when(jnp.logical_and(pl.program_id(0) == 0, j == 0))
    def _():
        _ffn_stage_weights(w_up_hbm, w_d_hbm, w_up, w_d, stage, stage_sem)

    @pl.when(j == 0)
    def _():
        abuf[0:SUBLANES, :] = jnp.zeros((SUBLANES, D_FF), F32)

    x = x_ref[0]
    mod_row = pl.ds(pl.program_id(0), 1)
    sh2 = mod_ref[mod_row, 3 * D_MODEL:4 * D_MODEL]
    sc2 = mod_ref[mod_row, 4 * D_MODEL:5 * D_MODEL]
    gt2 = mod_ref[mod_row, 5 * D_MODEL:6 * D_MODEL]
    h = (_rms_scale(x) * g_ref[...]) * (1.0 + sc2) + sh2
    hb = h.astype(BF16)
    abuf[SUBLANES:SUBLANES + tm, :] = _dot(hb, w_up[:, 0:D_FF])
    gate = _dot(hb, w_up[:, D_FF:2 * D_FF])
    a0 = abuf[SUBLANES:SUBLANES + tm, :]
    a1 = abuf[SUBLANES - 1:SUBLANES - 1 + tm, :]
    a2 = abuf[SUBLANES - 2:SUBLANES - 2 + tm, :]
    ac = a2 * cw_ref[0] + a1 * cw_ref[1] + a0 * cw_ref[2]
    abuf[0:SUBLANES, :] = abuf[tm:tm + SUBLANES, :]
    act = (ac * jax.nn.sigmoid(ac)) * gate
    x2 = x + gt2 * _dot(act.astype(BF16), w_d[...])
    o_ref[0] = _rms_scale(x2) * gf_ref[...] if final_norm else x2


def _ffn_call(x, mod, g_norm, w_up, conv_w, w_d, g_final, final_norm):
    bsz, seq, d = x.shape
    tm = TM_FFN
    hbm = pl.BlockSpec(memory_space=pl.ANY)
    return pl.pallas_call(
        functools.partial(_ffn_kernel, final_norm=final_norm),
        grid=(bsz, seq // tm),
        in_specs=[
            pl.BlockSpec((1, tm, d), lambda b, j: (b, j, 0)),
            _const_spec(mod.shape), _const_spec(g_norm.shape), hbm, _const_spec(conv_w.shape), hbm,
            _const_spec(g_final.shape),
        ],
        out_specs=pl.BlockSpec((1, tm, d), lambda b, j: (b, j, 0)),
        out_shape=jax.ShapeDtypeStruct(x.shape, F32),
        scratch_shapes=[
            pltpu.VMEM((d, 2 * D_FF), BF16),
            pltpu.VMEM((D_FF, d), BF16),
            pltpu.VMEM((STAGE_SLOTS, STAGE_ROWS, FFN_STAGE_COLS), F32),
            pltpu.SemaphoreType.DMA((STAGE_SLOTS,)),
            pltpu.VMEM((tm + SUBLANES, D_FF), F32),
        ],
        compiler_params=pltpu.CompilerParams(
            dimension_semantics=("arbitrary", "arbitrary"),
            vmem_limit_bytes=VMEM_LIMIT_BYTES),
        name="ffn_sublayer",
    )(x, mod, g_norm, w_up, conv_w, w_d, g_final)


def kernel(x, c, w_ada, b_ada, g_norm_mix, w_in, b_in, conv_mix_w, mlstm_head_g,
           w_proj_conv, w_proj_mlstm, w_out, g_norm_ffn, w_up, conv_ffn_w, w_down, g_final):
    depth = w_ada.shape[0]
    bsz = x.shape[0]
    assert bsz <= SUBLANES
    o_gt = 3 * CONV_DIM + 2 * HEADS * DQK + 2 * MLSTM_DIM + 2 * HEADS

    for l in range(depth):
        mod = _ada_call(c, w_ada[l], b_ada[l][None, :])

        bi = b_in[l]
        x = _mixer_call(
            x, mod, g_norm_mix[l][None, :],
            w_in[l].T, bi[None, :], bi[None, o_gt:],
            conv_mix_w[l][:, None, :], mlstm_head_g[l],
            w_proj_conv[l], w_proj_mlstm[l], w_out[l])

        x = _ffn_call(
            x, mod, g_norm_ffn[l][None, :], w_up[l],
            conv_ffn_w[l][:, None, :], w_down[l], g_final[None, :],
            final_norm=(l == depth - 1))
    return x
```
